```python
import jax, jax.numpy as jnp
from jax import lax
import numpy as np

D_MODEL = 1024
BATCH = 8
SEQ = 2048
DEPTH = 2

GRID_W = 64
CTX_LEN = 256
N_DIR = 2
CONV_W = 4
EPS = 1e-6
D_MIX = D_MODEL
D_FF = 4 * D_MODEL
GDN_HD = 64
GDN_WIDTH = 3 * D_MODEL // 8
GDN_HEADS = GDN_WIDTH // GDN_HD
GDN_CHUNK = 64
LRU_WIDTH = D_MODEL // 4
LRU_BLOCKS = 4
LRU_BW = LRU_WIDTH // LRU_BLOCKS
LRU_C = 8.0
RWKV_WIDTH = D_MIX - GDN_WIDTH - LRU_WIDTH
RWKV_HD = 64
RWKV_HEADS = RWKV_WIDTH // RWKV_HD
RWKV_DECAY_RANK = 64
RWKV_A_RANK = 64
RWKV_G_RANK = 128
RWKV_GN_EPS = 6.4e-4
RWKV_SIZES = (RWKV_WIDTH, RWKV_WIDTH, RWKV_WIDTH, N_DIR * RWKV_DECAY_RANK, N_DIR * RWKV_A_RANK, RWKV_G_RANK)
RWKV_IN = sum(RWKV_SIZES)
IN_SIZES = (3 * GDN_WIDTH, GDN_WIDTH, N_DIR * GDN_HEADS, N_DIR * GDN_HEADS, LRU_WIDTH, LRU_WIDTH, RWKV_IN)
D_IN = sum(IN_SIZES)

kernel_name = "hybrid_gdn_rglru_rwkv7_prefix_dit"

F32 = jnp.float32


def split_last(x, sizes):
    idx, acc = [], 0
    for s in sizes[:-1]:
        acc += s
        idx.append(acc)
    return jnp.split(x, idx, axis=-1)


def rms_norm(x, g):
    xf = x.astype(F32)
    y = xf * lax.rsqrt(jnp.mean(xf * xf, axis=-1, keepdims=True) + EPS)
    return (y * g.astype(F32)).astype(x.dtype)


def l2_normalize(x):
    xf = x.astype(F32)
    return xf * lax.rsqrt(jnp.sum(xf * xf, axis=-1, keepdims=True) + 1e-6)


def modulate(h, shift, scale):
    return h * (1.0 + scale) + shift


def centred_conv(x, w):
    pad_l = CONV_W // 2
    return lax.conv_general_dilated(
        x, w[:, None, :].astype(x.dtype), window_strides=(1,),
        padding=[(pad_l, CONV_W - 1 - pad_l)],
        dimension_numbers=("NWC", "WIO", "NWC"), feature_group_count=x.shape[-1])


def neighbour_lerp(p, mu):
    pp = jnp.pad(p, ((0, 0), (1, 1), (0, 0)))
    nb = 0.5 * (pp[:, :-2] + pp[:, 2:])
    return p + mu * (nb - p)


def flip_t(a):
    return jnp.flip(a, axis=1)


def bidirectional(run, ctx_fwd, lat_fwd, ctx_bwd, lat_bwd, s0):
    oc_f, sc_f = run(ctx_fwd, s0)
    ol_f, _ = run(lat_fwd, sc_f)
    oc_b, sc_b = run(tuple(flip_t(a) for a in ctx_bwd), s0)
    ol_b, _ = run(tuple(flip_t(a) for a in lat_bwd), sc_b)
    return oc_f + flip_t(oc_b), ol_f + flip_t(ol_b)


def gated_delta_chunked(q, k, v, beta, g, s0):
    Bs, T, H, DK = q.shape
    DV = v.shape[-1]
    C = GDN_CHUNK
    n = T // C

    def blocks(a):
        a = a.reshape((Bs, n, C, H) + a.shape[3:])
        return jnp.moveaxis(a, 2, 3).swapaxes(0, 1)

    q, k, v, beta, g = blocks(q), blocks(k), blocks(v), blocks(beta), blocks(g)
    q = q * (DK ** -0.5)
    gc = jnp.cumsum(g, axis=-1)
    i = jnp.arange(C)
    lower = i[:, None] >= i[None, :]
    strict = i[:, None] > i[None, :]
    diff = gc[..., :, None] - gc[..., None, :]
    decay = jnp.where(lower, jnp.exp(jnp.where(lower, diff, 0.0)), 0.0)
    kb = k * beta[..., None]
    L = jnp.where(strict, jnp.einsum('nbhid,nbhjd->nbhij', kb, k) * decay, 0.0)
    rhs = jnp.concatenate([v * beta[..., None], kb * jnp.exp(gc)[..., None]], axis=-1)
    sol = lax.linalg.triangular_solve(L + jnp.eye(C, dtype=L.dtype), rhs,
                                      left_side=True, lower=True, unit_diagonal=True)
    u, w = sol[..., :DV], sol[..., DV:]
    attn = jnp.einsum('nbhid,nbhjd->nbhij', q, k) * decay
    qg = q * jnp.exp(gc)[..., None]
    g_last = gc[..., -1]
    kd = k * jnp.exp(g_last[..., None] - gc)[..., None]

    def step(S, xs):
        u_c, w_c, a_c, qg_c, kd_c, gl_c = xs
        v_new = u_c - jnp.einsum('bhcd,bhde->bhce', w_c, S)
        o = jnp.einsum('bhcd,bhde->bhce', qg_c, S) + jnp.einsum('bhij,bhje->bhie', a_c, v_new)
        S = S * jnp.exp(gl_c)[..., None, None] + jnp.einsum('bhcd,bhce->bhde', kd_c, v_new)
        return S, o

    S, o = lax.scan(step, s0, (u, w, attn, qg, kd, g_last))
    o = jnp.moveaxis(o.swapaxes(0, 1), 3, 2).reshape(Bs, T, H, DV)
    return o, S


def gdn_run(inp, s0):
    return gated_delta_chunked(*inp, s0)


def gdn_prep(qkv, beta_raw, alpha_raw, conv_w, a_log, dt_bias):
    Bs, T, _ = qkv.shape
    qkv = jax.nn.silu(centred_conv(qkv, conv_w).astype(F32))
    q, k, v = jnp.split(qkv, 3, axis=-1)
    heads = lambda t: t.reshape(Bs, T, GDN_HEADS, GDN_HD)
    q, k, v = l2_normalize(heads(q)), l2_normalize(heads(k)), heads(v)
    beta = jax.nn.sigmoid(beta_raw.astype(F32)).reshape(Bs, T, N_DIR, GDN_HEADS)
    g = -jnp.exp(a_log) * jax.nn.softplus(alpha_raw.astype(F32).reshape(Bs, T, N_DIR, GDN_HEADS) + dt_bias)
    return (q, k, v, beta[:, :, 0], g[:, :, 0]), (q, k, v, beta[:, :, 1], g[:, :, 1])


def gdn_output(o, z, norm_w):
    Bs, T = z.shape[:2]
    o = o * lax.rsqrt(jnp.mean(o * o, axis=-1, keepdims=True) + EPS) * norm_w
    return (o * jax.nn.silu(z.astype(F32).reshape(Bs, T, GDN_HEADS, GDN_HD))).reshape(Bs, T, GDN_WIDTH)


def gdn_mixer(ctx_in, lat_in, conv_w, a_log, dt_bias, norm_w):
    qkv_c, z_c, b_c, al_c = ctx_in
    qkv_l, z_l, b_l, al_l = lat_in
    fc, bc = gdn_prep(qkv_c, b_c, al_c, conv_w, a_log, dt_bias)
    fl, bl = gdn_prep(qkv_l, b_l, al_l, conv_w, a_log, dt_bias)
    s0 = jnp.zeros((qkv_l.shape[0], GDN_HEADS, GDN_HD, GDN_HD), F32)
    oc, ol = bidirectional(gdn_run, fc, fl, bc, bl, s0)
    return gdn_output(oc, z_c, norm_w), gdn_output(ol, z_l, norm_w)


def to_col_major(a, rows):
    Bs, T, C = a.shape
    return a.reshape(Bs, rows, GRID_W, C).swapaxes(1, 2).reshape(Bs, T, C)


def from_col_major(a, rows):
    Bs, T, C = a.shape
    return a.reshape(Bs, GRID_W, rows, C).swapaxes(1, 2).reshape(Bs, T, C)


def block_diag(x, w):
    Bs, T, _ = x.shape
    xb = x.reshape(Bs, T, LRU_BLOCKS, LRU_BW)
    return jnp.einsum('btnk,nkj->btnj', xb, w).reshape(Bs, T, LRU_WIDTH)


def lru_gates(x, w_a, b_a, w_x, b_x, lam):
    r = jax.nn.sigmoid(block_diag(x, w_a) + b_a)
    i = jax.nn.sigmoid(block_diag(x, w_x) + b_x)
    log_a = -LRU_C * r * jax.nn.softplus(-lam)
    a = jnp.exp(log_a)
    mult = jnp.sqrt(-jnp.expm1(2.0 * log_a))
    return (a, mult * (i * x))


def lru_run(inp, h0):
    a, bx = inp

    def combine(l, r):
        return (l[0] * r[0], r[0] * l[1] + r[1])

    a_cum, h = lax.associative_scan(combine, (a, bx), axis=1)
    h = h + a_cum * h0[:, None, :]
    return h, h[:, -1]


def lru_mixer(x_c, gate_c, x_l, gate_l, rows, conv_w, conv_b, w_a, b_a, w_x, b_x, lam):
    xc = (centred_conv(x_c, conv_w) + conv_b).astype(F32)
    xl = (centred_conv(to_col_major(x_l, rows), conv_w) + conv_b).astype(F32)
    fc = lru_gates(xc, w_a[0], b_a[0], w_x[0], b_x[0], lam[0])
    bc = lru_gates(xc, w_a[1], b_a[1], w_x[1], b_x[1], lam[1])
    fl = lru_gates(xl, w_a[0], b_a[0], w_x[0], b_x[0], lam[0])
    bl = lru_gates(xl, w_a[1], b_a[1], w_x[1], b_x[1], lam[1])
    s0 = jnp.zeros((x_l.shape[0], LRU_WIDTH), F32)
    hc, hl = bidirectional(lru_run, fc, fl, bc, bl, s0)
    hl = from_col_major(hl, rows)
    gelu = lambda t: jax.nn.gelu(t.astype(F32), approximate=True)
    return hc * gelu(gate_c), hl * gelu(gate_l)


def rwkv_run(inp, s0):
    xs = tuple(jnp.swapaxes(t, 0, 1) for t in inp)

    def step(S, xt):
        w, kk, a, k, v, r = xt
        S = (S * w[:, :, None, :]
             - jnp.einsum('bhvk,bhk->bhv', S, kk)[..., None] * (kk * a)[:, :, None, :]
             + v[..., :, None] * k[:, :, None, :])
        y = jnp.einsum('bhvk,bhk->bhv', S, r)
        return S, y

    S, y = lax.scan(step, s0, xs)
    return jnp.swapaxes(y, 0, 1), S


def rwkv_prep(p, mu, w0, w_up, a0, a_up, g_up, k_k, k_a, r_k):
    Bs, T, _ = p.shape
    p = neighbour_lerp(p.astype(F32), mu)
    r, k, v, wd, ad, gd = split_last(p, RWKV_SIZES)
    heads = lambda t: t.reshape(Bs, T, RWKV_HEADS, RWKV_HD)
    dirs = lambda t: t.reshape(Bs, T, N_DIR, -1)
    logw = -jax.nn.softplus(-(w0 + jnp.einsum('btdr,drc->btdc', jnp.tanh(dirs(wd)), w_up))) - 0.5
    w = jnp.exp(-jnp.exp(logw))
    a = jax.nn.sigmoid(a0 + jnp.einsum('btdr,drc->btdc', dirs(ad), a_up))
    g = jax.nn.sigmoid(gd) @ g_up
    kk = l2_normalize(heads(k * k_k))
    k_dir = k[:, :, None, :] * (1.0 + (a - 1.0) * k_a)
    r, v = heads(r), heads(v)
    per_dir = lambda d: (heads(w[:, :, d]), kk, heads(a[:, :, d]), heads(k_dir[:, :, d]), v, r)
    bonus = jnp.sum(r[:, :, None] * k_dir.reshape(Bs, T, N_DIR, RWKV_HEADS, RWKV_HD) * r_k, axis=(2, 4))
    return per_dir(0), per_dir(1), bonus, v, g


def rwkv_output(y, bonus, v, g, gn_w, gn_b):
    Bs, T = y.shape[:2]
    mean = jnp.mean(y, axis=-1, keepdims=True)
    var = jnp.mean(jnp.square(y - mean), axis=-1, keepdims=True)
    yn = ((y - mean) * lax.rsqrt(var + RWKV_GN_EPS)).reshape(Bs, T, RWKV_WIDTH) * gn_w + gn_b
    return (yn + (bonus[..., None] * v).reshape(Bs, T, RWKV_WIDTH)) * g


def rwkv_mixer(p_c, p_l, mu, w0, w_up, a0, a_up, g_up, k_k, k_a, r_k, gn_w, gn_b):
    fc, bc, bonus_c, v_c, g_c = rwkv_prep(p_c, mu, w0, w_up, a0, a_up, g_up, k_k, k_a, r_k)
    fl, bl, bonus_l, v_l, g_l = rwkv_prep(p_l, mu, w0, w_up, a0, a_up, g_up, k_k, k_a, r_k)
    s0 = jnp.zeros((p_l.shape[0], RWKV_HEADS, RWKV_HD, RWKV_HD), F32)
    yc, yl = bidirectional(rwkv_run, fc, fl, bc, bl, s0)
    return (rwkv_output(yc, bonus_c, v_c, g_c, gn_w, gn_b),
            rwkv_output(yl, bonus_l, v_l, g_l, gn_w, gn_b))


def trunk_layer(xl, xc, mod_l, mod_c, rows, update_ctx, p):
    ml = jnp.split(mod_l[:, None, :], 6, axis=-1)
    mc = jnp.split(mod_c[None, None, :], 6, axis=-1)
    hl = modulate(rms_norm(xl, p["norm_mix_pre"]), ml[0], ml[1])
    hc = modulate(rms_norm(xc, p["norm_mix_pre"]), mc[0], mc[1])
    qkv_l, z_l, beta_l, alpha_l, lx_l, lg_l, rw_l = split_last(hl @ p["w_in"], IN_SIZES)
    qkv_c, z_c, beta_c, alpha_c, lx_c, lg_c, rw_c = split_last(hc @ p["w_in"], IN_SIZES)

    gdn_c, gdn_l = gdn_mixer((qkv_c, z_c, beta_c, alpha_c), (qkv_l, z_l, beta_l, alpha_l),
                             p["gdn_conv"], p["gdn_a_log"], p["gdn_dt_bias"], p["gdn_norm"])
    lru_c, lru_l = lru_mixer(lx_c, lg_c, lx_l, lg_l, rows, p["lru_conv"], p["lru_conv_b"],
                             p["lru_wa"], p["lru_ba"], p["lru_wx"], p["lru_bx"], p["lru_lambda"])
    rwk_c, rwk_l = rwkv_mixer(rw_c, rw_l, p["rwkv_mu"], p["rwkv_w0"], p["rwkv_w_up"], p["rwkv_a0"],
                              p["rwkv_a_up"], p["rwkv_g_up"], p["rwkv_k_k"], p["rwkv_k_a"],
                              p["rwkv_r_k"], p["rwkv_gn_w"], p["rwkv_gn_b"])

    def finish(x, parts, m):
        o = jnp.concatenate(parts, axis=-1).astype(x.dtype) @ p["w_out"]
        x = x + m[2] * rms_norm(o, p["norm_mix_post"])
        h = modulate(rms_norm(x, p["norm_ffn_pre"]), m[3], m[4])
        f = jnp.square(jax.nn.relu(h @ p["ffn_up"])) @ p["ffn_down"]
        return x + m[5] * rms_norm(f, p["norm_ffn_post"])

    xl = finish(xl, (gdn_l, lru_l, rwk_l), ml)
    if update_ctx:
        xc = finish(xc, (gdn_c, lru_c, rwk_c), mc)
    return xl, xc


def setup_inputs(seed: int = 0) -> dict:
    key = jax.random.key(seed)
    ks = iter(jax.random.split(key, 48))
    nrm = lambda shape, s: jax.random.normal(next(ks), shape, F32) * s
    uni = lambda shape, lo, hi: jax.random.uniform(next(ks), shape, F32, lo, hi)
    gain = lambda shape: 1.0 + nrm(shape, 0.02)
    L = DEPTH
    dt = jnp.exp(uni((L, N_DIR, GDN_HEADS), float(np.log(1e-3)), float(np.log(1e-1))))
    s = uni((L, N_DIR, LRU_WIDTH), 0.9, 0.999) ** (1.0 / LRU_C)
    return {
        "x": nrm((BATCH, SEQ, D_MODEL), 1.0),
        "c": nrm((BATCH, D_MODEL), 1.0),
        "ctx": nrm((BATCH, CTX_LEN, D_MODEL), 1.0),
        "c_ctx": nrm((D_MODEL,), 1.0),
        "ada_w": nrm((L, D_MODEL, 6 * D_MODEL), 0.5 * D_MODEL ** -0.5),
        "ada_b": nrm((L, 6 * D_MODEL), 0.02),
        "norm_mix_pre": gain((L, D_MODEL)),
        "norm_mix_post": gain((L, D_MODEL)),
        "norm_ffn_pre": gain((L, D_MODEL)),
        "norm_ffn_post": gain((L, D_MODEL)),
        "w_in": nrm((L, D_MODEL, D_IN), D_MODEL ** -0.5),
        "gdn_conv": nrm((L, CONV_W, 3 * GDN_WIDTH), CONV_W ** -0.5),
        "gdn_a_log": jnp.log(uni((L, N_DIR, GDN_HEADS), 1.0, 16.0)),
        "gdn_dt_bias": dt + jnp.log(-jnp.expm1(-dt)),
        "gdn_norm": gain((L, GDN_HD)),
        "lru_conv": nrm((L, CONV_W, LRU_WIDTH), CONV_W ** -0.5),
        "lru_conv_b": nrm((L, LRU_WIDTH), 0.02),
        "lru_wa": nrm((L, N_DIR, LRU_BLOCKS, LRU_BW, LRU_BW), LRU_BW ** -0.5),
        "lru_ba": nrm((L, N_DIR, LRU_WIDTH), 0.02),
        "lru_wx": nrm((L, N_DIR, LRU_BLOCKS, LRU_BW, LRU_BW), LRU_BW ** -0.5),
        "lru_bx": nrm((L, N_DIR, LRU_WIDTH), 0.02),
        "lru_lambda": jnp.log(s) - jnp.log1p(-s),
        "rwkv_mu": uni((L, RWKV_IN), 0.0, 1.0),
        "rwkv_w0": nrm((L, N_DIR, RWKV_WIDTH), 0.5),
        "rwkv_w_up": nrm((L, N_DIR, RWKV_DECAY_RANK, RWKV_WIDTH), 0.1),
        "rwkv_a0": nrm((L, N_DIR, RWKV_WIDTH), 0.5),
        "rwkv_a_up": nrm((L, N_DIR, RWKV_A_RANK, RWKV_WIDTH), 0.5 * RWKV_A_RANK ** -0.5),
        "rwkv_g_up": nrm((L, RWKV_G_RANK, RWKV_WIDTH), RWKV_G_RANK ** -0.5),
        "rwkv_k_k": 0.85 + nrm((L, RWKV_WIDTH), 0.02),
        "rwkv_k_a": gain((L, RWKV_WIDTH)),
        "rwkv_r_k": nrm((L, RWKV_HEADS, RWKV_HD), 0.1),
        "rwkv_gn_w": gain((L, RWKV_WIDTH)),
        "rwkv_gn_b": nrm((L, RWKV_WIDTH), 0.02),
        "w_out": nrm((L, D_MIX, D_MODEL), D_MIX ** -0.5),
        "ffn_up": nrm((L, D_MODEL, D_FF), D_MODEL ** -0.5),
        "ffn_down": nrm((L, D_FF, D_MODEL), D_FF ** -0.5),
    }


def reference(x, c, ctx, c_ctx, ada_w, ada_b, norm_mix_pre, norm_mix_post, norm_ffn_pre, norm_ffn_post,
              w_in, gdn_conv, gdn_a_log, gdn_dt_bias, gdn_norm, lru_conv, lru_conv_b, lru_wa, lru_ba,
              lru_wx, lru_bx, lru_lambda, rwkv_mu, rwkv_w0, rwkv_w_up, rwkv_a0, rwkv_a_up, rwkv_g_up,
              rwkv_k_k, rwkv_k_a, rwkv_r_k, rwkv_gn_w, rwkv_gn_b, w_out, ffn_up, ffn_down):
    rows = x.shape[1] // GRID_W
    silu_c = jax.nn.silu(c)
    silu_cc = jax.nn.silu(c_ctx)
    xl, xc = x, ctx
    for i in range(DEPTH):
        p = {
            "norm_mix_pre": norm_mix_pre[i], "norm_mix_post": norm_mix_post[i],
            "norm_ffn_pre": norm_ffn_pre[i], "norm_ffn_post": norm_ffn_post[i],
            "w_in": w_in[i], "gdn_conv": gdn_conv[i], "gdn_a_log": gdn_a_log[i],
            "gdn_dt_bias": gdn_dt_bias[i], "gdn_norm": gdn_norm[i],
            "lru_conv": lru_conv[i], "lru_conv_b": lru_conv_b[i], "lru_wa": lru_wa[i],
            "lru_ba": lru_ba[i], "lru_wx": lru_wx[i], "lru_bx": lru_bx[i], "lru_lambda": lru_lambda[i],
            "rwkv_mu": rwkv_mu[i], "rwkv_w0": rwkv_w0[i], "rwkv_w_up": rwkv_w_up[i],
            "rwkv_a0": rwkv_a0[i], "rwkv_a_up": rwkv_a_up[i], "rwkv_g_up": rwkv_g_up[i],
            "rwkv_k_k": rwkv_k_k[i], "rwkv_k_a": rwkv_k_a[i], "rwkv_r_k": rwkv_r_k[i],
            "rwkv_gn_w": rwkv_gn_w[i], "rwkv_gn_b": rwkv_gn_b[i],
            "w_out": w_out[i], "ffn_up": ffn_up[i], "ffn_down": ffn_down[i],
        }
        mod_l = silu_c @ ada_w[i] + ada_b[i]
        mod_c = silu_cc @ ada_w[i] + ada_b[i]
        xl, xc = trunk_layer(xl, xc, mod_l, mod_c, rows, i < DEPTH - 1, p)
    return xl
```

```python
import functools

import numpy as np
import jax
import jax.numpy as jnp
from jax import lax
from jax.experimental import pallas as pl
from jax.experimental.pallas import tpu as pltpu

F32 = jnp.float32
BF16 = jnp.bfloat16

LANES = 128
SUBLANES = 8
VMEM_LIMIT = 56 * 1024 * 1024

D_MODEL = 1024
DEPTH = 2
GRID_W = 64
CONV_W = 4
EPS = 1e-6
D_FF = 4 * D_MODEL
HD = 64
GDN_WIDTH = 3 * D_MODEL // 8
GDN_HEADS = GDN_WIDTH // HD
LRU_WIDTH = D_MODEL // 4
LRU_BLOCKS = 4
LRU_BW = LRU_WIDTH // LRU_BLOCKS
LRU_C = 8.0
RWKV_WIDTH = D_MODEL - GDN_WIDTH - LRU_WIDTH
RWKV_HEADS = RWKV_WIDTH // HD
RWKV_RANK = 64
RWKV_G_RANK = 128
RWKV_GN_EPS = 6.4e-4
RWKV_IN = 3 * RWKV_WIDTH + 2 * RWKV_RANK + 2 * RWKV_RANK + RWKV_G_RANK
N_PAIRS = GDN_HEADS // 2
BA_W = LANES
P_OFF = np.cumsum([0, 3 * GDN_WIDTH, GDN_WIDTH, BA_W, LRU_WIDTH, LRU_WIDTH, RWKV_IN])
D_INP = int(P_OFF[-1])

CH = 64
ROWT = 256
GAP = SUBLANES


def _mm(a, b):
    return jnp.dot(a.astype(BF16), b.astype(BF16), preferred_element_type=F32)


def _mm_nt(a, b):
    return lax.dot_general(a.astype(BF16), b.astype(BF16), (((1,), (1,)), ((), ())),
                           preferred_element_type=F32)


def _mm_tn(a, b):
    return lax.dot_general(a.astype(BF16), b.astype(BF16), (((0,), (0,)), ((), ())),
                           preferred_element_type=F32)


def _split2(x):
    hi = x.astype(BF16)
    lo = (x - hi.astype(F32)).astype(BF16)
    return hi, lo


def _split3(x):
    hi = x.astype(BF16)
    r = x - hi.astype(F32)
    mid = r.astype(BF16)
    lo = (r - mid.astype(F32)).astype(BF16)
    return hi, mid, lo


def _mm3(a, b):
    ah, al = _split2(a)
    bh, bl = _split2(b)
    d = functools.partial(jnp.dot, preferred_element_type=F32)
    return d(ah, bh) + (d(ah, bl) + d(al, bh))


def _mm_sel_l(m01, x):
    mb = m01.astype(BF16)
    h, m, l = _split3(x)
    d = functools.partial(jnp.dot, preferred_element_type=F32)
    return d(mb, h) + (d(mb, m) + d(mb, l))


def _mm_sel_r(x, m01):
    mb = m01.astype(BF16)
    h, m, l = _split3(x)
    d = functools.partial(jnp.dot, preferred_element_type=F32)
    return d(h, mb) + (d(m, mb) + d(l, mb))


def _sigmoid(x):
    return 1.0 / (1.0 + jnp.exp(-x))


def _silu(x):
    return x * _sigmoid(x)


def _softplus(x):
    return jnp.maximum(x, 0.0) + jnp.log(1.0 + jnp.exp(-jnp.abs(x)))


def _gelu_tanh(x):
    return 0.5 * x * (1.0 + jnp.tanh(0.7978845608028654 * (x + 0.044715 * (x * x * x))))


def _bd(y, m0, m1):
    return jnp.concatenate([y * m0, y * m1], axis=0)


def _pair_consts():
    i = np.arange(CH)[:, None]
    j = (np.arange(LANES) % HD)[None, :]
    t = np.arange(CH)[None, :]
    incl = np.stack([i >= j, i <= j]).astype(np.float32)
    strict = np.stack([i > j, i < j]).astype(np.float32)
    eye = (i == j).astype(np.float32)
    tri = np.stack([i >= t, i <= t]).astype(np.float32)
    lvls = []
    for d in range(2):
        per = []
        for m in (1, 2, 4, 8, 16, 32):
            same = (i // (2 * m)) == (j // (2 * m))
            lo_i, lo_j = (i % (2 * m)) < m, (j % (2 * m)) < m
            off = same & (~lo_i) & lo_j if d == 0 else same & lo_i & (~lo_j)
            per.append(off)
        lvls.append(np.stack(per))
    lvl = np.stack(lvls).astype(np.float32)
    lane = np.arange(LANES)
    hm = np.stack([lane < HD, lane >= HD]).astype(np.float32)[:, None, :]
    bdm = ((np.arange(LANES)[:, None] // HD) == (lane[None, :] // HD)).astype(np.float32)
    return dict(incl=incl, strict=strict, eye=eye, tri=tri, lvl=lvl, hm=hm, bdm=bdm)


def _inv_unit_tri(a, lvl_ref, d, eye, m0, m1):
    t = eye - a * lvl_ref[d, 0]
    for k in range(1, 6):
        off = a * lvl_ref[d, k]
        x = _mm3(t, _bd(off, m0, m1))
        t = t - _mm3(x, _bd(t, m0, m1))
    return t


def _chunk_order(s, n_ctx, n_all, d):
    if d == 0:
        return s
    return jnp.where(s < n_ctx, n_ctx - 1 - s, n_all + n_ctx - 1 - s)


def _fill_padded(dst_ref, src_c_ref, src_l_ref, tc, tl):
    w = dst_ref.shape[1]
    z = jnp.zeros((GAP, w), F32)
    dst_ref[0:GAP, :] = z
    dst_ref[GAP:GAP + tc, :] = src_c_ref[...]
    dst_ref[GAP + tc:2 * GAP + tc, :] = z
    dst_ref[2 * GAP + tc:2 * GAP + tc + tl, :] = src_l_ref[...]
    dst_ref[2 * GAP + tc + tl:3 * GAP + tc + tl, :] = z


def _row_tiles(tc, tl):
    out = []
    for base_p, base_u, n in ((GAP, 0, tc), (2 * GAP + tc, tc, tl)):
        for t0 in range(0, n, ROWT):
            out.append((base_p + t0, base_u + t0, min(ROWT, n - t0)))
    return out


def _ada_kernel(c_ref, w_ref, b_ref, o_ref):
    c = c_ref[...]
    o_ref[...] = _mm(_silu(c), w_ref[...]) + b_ref[...]


def _ada_mod(cvec, ada_w, ada_b):
    L = ada_w.shape[0]
    n = ada_w.shape[2]
    tn = 1536
    return pl.pallas_call(
        _ada_kernel,
        grid=(L, n // tn),
        in_specs=[pl.BlockSpec((16, D_MODEL), lambda l, j: (0, 0)),
                  pl.BlockSpec((None, D_MODEL, tn), lambda l, j: (l, 0, j)),
                  pl.BlockSpec((None, 1, tn), lambda l, j: (l, 0, j))],
        out_specs=pl.BlockSpec((None, 16, tn), lambda l, j: (l, 0, j)),
        out_shape=jax.ShapeDtypeStruct((L, 16, n), F32),
        compiler_params=pltpu.CompilerParams(dimension_semantics=("arbitrary", "arbitrary"),
                                             vmem_limit_bytes=VMEM_LIMIT),
        name="ada_mod",
    )(cvec, ada_w, ada_b.reshape(L, 1, n))


def _rms(x, g):
    return x * lax.rsqrt(jnp.mean(x * x, axis=-1, keepdims=True) + EPS) * g


def _inproj_kernel(x_ref, sh_ref, sc_ref, g_ref, w_ref, qkv_ref, z_ref, ba_ref, lx_ref, lg_ref, rw_ref):
    h = _rms(x_ref[...], g_ref[...]) * (1.0 + sc_ref[...]) + sh_ref[...]
    p = jnp.dot(h.astype(BF16), w_ref[...], preferred_element_type=F32)
    for ref, k in zip((qkv_ref, z_ref, ba_ref, lx_ref, lg_ref, rw_ref), range(6)):
        ref[...] = p[:, int(P_OFF[k]):int(P_OFF[k + 1])]


def _inproj(x2, mod_rows, rows_per_mod, g, w_bf):
    n = x2.shape[0]
    tm = 256
    tiles_per_mod = rows_per_mod // tm
    widths = [int(P_OFF[k + 1] - P_OFF[k]) for k in range(6)]
    return pl.pallas_call(
        _inproj_kernel,
        grid=(n // tm,),
        in_specs=[pl.BlockSpec((tm, D_MODEL), lambda i: (i, 0)),
                  pl.BlockSpec((None, 1, D_MODEL), lambda i: (6 * (i // tiles_per_mod), 0, 0)),
                  pl.BlockSpec((None, 1, D_MODEL), lambda i: (6 * (i // tiles_per_mod) + 1, 0, 0)),
                  pl.BlockSpec((1, D_MODEL), lambda i: (0, 0)),
                  pl.BlockSpec((D_MODEL, D_INP), lambda i: (0, 0))],
        out_specs=[pl.BlockSpec((tm, w), lambda i: (i, 0)) for w in widths],
        out_shape=[jax.ShapeDtypeStruct((n, w), F32) for w in widths],
        compiler_params=pltpu.CompilerParams(dimension_semantics=("arbitrary",),
                                             vmem_limit_bytes=VMEM_LIMIT),
        name="inproj",
    )(x2, mod_rows, mod_rows, g.reshape(1, D_MODEL), w_bf)


def _gdn_phase1(q, k, v, beta, g, d, cst):
    incl_ref, strict_ref, eye_ref, tri_ref, lvl_ref, hm_ref = cst
    m0, m1 = hm_ref[0], hm_ref[1]
    incl, strict = incl_ref[d], strict_ref[d]
    gc = _mm_sel_l(tri_ref[d], g)
    ones = jnp.ones((CH, CH), F32)
    gr = _mm_sel_l(ones, g * incl_ref[1 - d])
    decay = jnp.exp(jnp.minimum(gc - gr, 0.0)) * incl
    kb = k * beta
    la = _mm_nt(jnp.concatenate([kb, q], axis=0), _bd(k, m0, m1))
    lmat = la[:CH] * decay * strict
    attn = la[CH:] * decay
    t = _inv_unit_tri(lmat, lvl_ref, d, eye_ref[...], m0, m1)
    eg = jnp.exp(gc)
    rhs = jnp.concatenate([v * beta, kb * eg], axis=1)
    m0w = jnp.concatenate([m0, m0], axis=1)
    m1w = jnp.concatenate([m1, m1], axis=1)
    sol = _mm(t, _bd(rhs, m0w, m1w))
    u, w = sol[:, :LANES], sol[:, LANES:]
    last = CH - 1 if d == 0 else 0
    glast = gc[last:last + 1, :]
    qg = q * eg
    kd = k * jnp.exp(glast - gc)
    egl = jnp.broadcast_to(jnp.exp(glast), (SUBLANES, LANES))
    return u, w, attn, qg, kd, egl


def _gdn_kernel(qc_ref, kc_ref, vc_ref, zc_ref, bac_ref, ql_ref, kl_ref, vl_ref, zl_ref, bal_ref,
                cwq_ref, cwk_ref, cwv_ref, aux_ref, e_ref,
                incl_ref, strict_ref, eye_ref, tri_ref, lvl_ref, hm_ref, bdm_ref,
                oc_ref, ol_ref,
                qp_ref, kp_ref, vp_ref, qs_ref, ks_ref, vs_ref, bg_ref,
                p1_ref, egl_ref, of_ref, ob_ref):
    tc, tl = qc_ref.shape[0], ql_ref.shape[0]
    tall = tc + tl
    n_ctx, n_all = tc // CH, tall // CH
    m0, m1 = hm_ref[0], hm_ref[1]
    bdm = bdm_ref[...]
    cst = (incl_ref, strict_ref, eye_ref, tri_ref, lvl_ref, hm_ref)

    _fill_padded(qp_ref, qc_ref, ql_ref, tc, tl)
    _fill_padded(kp_ref, kc_ref, kl_ref, tc, tl)
    _fill_padded(vp_ref, vc_ref, vl_ref, tc, tl)
    lane = lax.broadcasted_iota(jnp.int32, (1, LANES), 1)
    alog, dtb = aux_ref[0:1, :], aux_ref[1:2, :]
    for (po, uo, n) in _row_tiles(tc, tl):
        def conv(src, cw):
            acc = cw[0:1, :] * src[po - 2:po - 2 + n, :]
            for tap in range(1, CONV_W):
                acc = acc + cw[tap:tap + 1, :] * src[po - 2 + tap:po - 2 + tap + n, :]
            return _silu(acc)
        q = conv(qp_ref, cwq_ref)
        k = conv(kp_ref, cwk_ref)
        v = conv(vp_ref, cwv_ref)
        q = q * lax.rsqrt(_mm_sel_r(q * q, bdm) + 1e-6) * (HD ** -0.5)
        k = k * lax.rsqrt(_mm_sel_r(k * k, bdm) + 1e-6)
        qs_ref[uo:uo + n, :] = q
        ks_ref[uo:uo + n, :] = k
        vs_ref[uo:uo + n, :] = v
        ba = bac_ref[uo:uo + n, :] if uo < tc else bal_ref[uo - tc:uo - tc + n, :]
        beta = _sigmoid(ba)
        gval = -jnp.exp(alog) * _softplus(ba + dtb)
        bgv = jnp.where(lane < 2 * GDN_HEADS, beta, gval)
        bg_ref[uo:uo + n, :] = _mm_sel_r(bgv, e_ref[...])

    def p1_body(s, carry):
        for d in range(2):
            c = _chunk_order(s, n_ctx, n_all, d)
            t0 = pl.multiple_of(c * CH, CH)
            rows = pl.ds(t0, CH)
            q, k, v = qs_ref[rows, :], ks_ref[rows, :], vs_ref[rows, :]
            beta = bg_ref[rows, d * LANES:(d + 1) * LANES]
            g = bg_ref[rows, (2 + d) * LANES:(3 + d) * LANES]
            u, w, attn, qg, kd, egl = _gdn_phase1(q, k, v, beta, g, d, cst)
            for idx, val in enumerate((u, w, attn, qg, kd)):
                p1_ref[d, s, idx] = val
            egl_ref[d, s] = egl
        return carry
    lax.fori_loop(0, n_all, p1_body, 0)

    def p2_body(s, carry):
        new = []
        for d in range(2):
            st = carry[d]
            c = _chunk_order(s, n_ctx, n_all, d)
            t0 = pl.multiple_of(c * CH, CH)
            u, w, attn, qg, kd = (p1_ref[d, s, i] for i in range(5))
            wq = _mm(jnp.concatenate([w, qg], axis=0), st)
            vn = u - wq[:CH]
            o = wq[CH:] + _mm(attn, _bd(vn, m0, m1))
            st = st * egl_ref[d, s][0:1, :] + _mm_tn(kd, vn) * bdm
            oref = of_ref if d == 0 else ob_ref
            oref[pl.ds(t0, CH), :] = o
            new.append(st)
        return tuple(new)
    z = jnp.zeros((LANES, LANES), F32)
    lax.fori_loop(0, n_all, p2_body, (z, z))

    nw = aux_ref[2:3, :]
    for (po, uo, n) in _row_tiles(tc, tl):
        o = of_ref[uo:uo + n, :] + ob_ref[uo:uo + n, :]
        ms = _mm_sel_r(o * o, bdm) * (1.0 / HD)
        if uo < tc:
            oc_ref[uo:uo + n, :] = o * lax.rsqrt(ms + EPS) * nw * _silu(zc_ref[uo:uo + n, :])
        else:
            lo = uo - tc
            ol_ref[lo:lo + n, :] = o * lax.rsqrt(ms + EPS) * nw * _silu(zl_ref[lo:lo + n, :])


def _gdn_expand_consts():
    e = np.zeros((N_PAIRS, BA_W, 4 * LANES), np.float32)
    for p in range(N_PAIRS):
        for blk in range(4):
            d, is_g = blk % 2, blk // 2
            for h in range(2):
                col = is_g * 2 * GDN_HEADS + d * GDN_HEADS + 2 * p + h
                e[p, col, blk * LANES + h * HD:blk * LANES + (h + 1) * HD] = 1.0
    return e


def _gdn_mixer(qkv_c, z_c, ba_c, qkv_l, z_l, ba_l, conv_w, a_log, dt_bias, norm_w, consts):
    bsz, tc, _ = qkv_c.shape
    tl = qkv_l.shape[1]
    tall = tc + tl
    n_all = tall // CH
    aux = jnp.zeros((SUBLANES, LANES), F32)
    aux = aux.at[0, 2 * GDN_HEADS:4 * GDN_HEADS].set(a_log.reshape(-1))
    aux = aux.at[1, 2 * GDN_HEADS:4 * GDN_HEADS].set(dt_bias.reshape(-1))
    aux = aux.at[2, :].set(jnp.tile(norm_w, 2))
    e = jnp.asarray(_gdn_expand_consts())
    cm = lambda *blk: pl.BlockSpec(blk, lambda b, p: (0,) * len(blk))

    def tok(t, col0):
        return pl.BlockSpec((None, t, LANES), lambda b, p: (b, 0, col0 + p))
    in_specs = [tok(tc, 0), tok(tc, N_PAIRS), tok(tc, 2 * N_PAIRS), tok(tc, 0),
                pl.BlockSpec((None, tc, BA_W), lambda b, p: (b, 0, 0)),
                tok(tl, 0), tok(tl, N_PAIRS), tok(tl, 2 * N_PAIRS), tok(tl, 0),
                pl.BlockSpec((None, tl, BA_W), lambda b, p: (b, 0, 0)),
                pl.BlockSpec((CONV_W, LANES), lambda b, p: (0, p)),
                pl.BlockSpec((CONV_W, LANES), lambda b, p: (0, N_PAIRS + p)),
                pl.BlockSpec((CONV_W, LANES), lambda b, p: (0, 2 * N_PAIRS + p)),
                cm(SUBLANES, LANES),
                pl.BlockSpec((None, BA_W, 4 * LANES), lambda b, p: (p, 0, 0)),
                cm(2, CH, LANES), cm(2, CH, LANES), cm(CH, LANES), cm(2, CH, CH),
                cm(2, 6, CH, LANES), cm(2, 1, LANES), cm(LANES, LANES)]
    pad_rows = tall + 3 * GAP
    scratch = [pltpu.VMEM((pad_rows, LANES), F32)] * 3 + [pltpu.VMEM((tall, LANES), F32)] * 3 + [
        pltpu.VMEM((tall, 4 * LANES), F32),
        pltpu.VMEM((2, n_all, 5, CH, LANES), F32),
        pltpu.VMEM((2, n_all, SUBLANES, LANES), F32),
        pltpu.VMEM((tall, LANES), F32), pltpu.VMEM((tall, LANES), F32)]
    return pl.pallas_call(
        _gdn_kernel,
        grid=(bsz, N_PAIRS),
        in_specs=in_specs,
        out_specs=[pl.BlockSpec((None, tc, LANES), lambda b, p: (b, 0, p)),
                   pl.BlockSpec((None, tl, LANES), lambda b, p: (b, 0, p))],
        out_shape=[jax.ShapeDtypeStruct((bsz, tc, GDN_WIDTH), F32),
                   jax.ShapeDtypeStruct((bsz, tl, GDN_WIDTH), F32)],
        scratch_shapes=scratch,
        compiler_params=pltpu.CompilerParams(dimension_semantics=("arbitrary", "arbitrary"),
                                             vmem_limit_bytes=VMEM_LIMIT),
        name="gdn_mixer",
    )(qkv_c, qkv_c, qkv_c, z_c, ba_c, qkv_l, qkv_l, qkv_l, z_l, ba_l,
      conv_w, conv_w, conv_w, aux, e,
      consts["incl"], consts["strict"], consts["eye"], consts["tri"], consts["lvl"], consts["hm"],
      consts["bdm"])


def _rwkv_phase1(r, v, kk, lw, b, kdir, d, cst):
    incl_ref, strict_ref, eye_ref, tri_ref, lvl_ref, hm_ref = cst
    m0, m1 = hm_ref[0], hm_ref[1]
    incl, strict = incl_ref[d], strict_ref[d]
    cum = _mm_sel_l(tri_ref[d], lw)
    einv = jnp.exp(-cum)
    kkq = kk * jnp.exp(cum - lw)
    rq = r * jnp.exp(cum)
    lhs = jnp.concatenate([kkq, rq], axis=0)
    rhs = jnp.concatenate([_bd(b * einv, m0, m1), _bd(kdir * einv, m0, m1)], axis=0)
    m = _mm_nt(lhs, rhs)
    a_b = m[:CH, :LANES] * strict
    a_k = m[:CH, LANES:] * strict
    a_rb = m[CH:, :LANES] * incl
    a_rk = m[CH:, LANES:] * incl
    t = _inv_unit_tri(a_b, lvl_ref, d, eye_ref[...], m0, m1)
    vbd = _bd(v, m0, m1)
    akv = _mm(a_k, vbd)
    m0w = jnp.concatenate([m0, m0], axis=1)
    m1w = jnp.concatenate([m1, m1], axis=1)
    sol = _mm(t, _bd(jnp.concatenate([kkq, akv], axis=1), m0w, m1w))
    wt, u0 = sol[:, :LANES], -sol[:, LANES:]
    y0 = _mm(a_rk, vbd)
    return wt, u0, y0, a_rb, cum


def _rwkv_kernel(rc_ref, kc_ref, vc_ref, wdc_ref, adc_ref, gdc_ref,
                 rl_ref, kl_ref, vl_ref, wdl_ref, adl_ref, gdl_ref,
                 mur_ref, muk_ref, muv_ref, mux_ref, wup_ref, aup_ref, gup_ref, vec_ref,
                 incl_ref, strict_ref, eye_ref, tri_ref, lvl_ref, hm_ref, bdm_ref,
                 oc_ref, ol_ref,
                 rp_ref, kp_ref, vp_ref, wdp_ref, adp_ref, gdp_ref,
                 rs_ref, vs_ref, kks_ref, gs_ref, bvs_ref, dir_ref,
                 p1_ref, yf_ref, yb_ref):
    tc, tl = rc_ref.shape[0], rl_ref.shape[0]
    tall = tc + tl
    n_ctx, n_all = tc // CH, tall // CH
    m0, m1 = hm_ref[0], hm_ref[1]
    bdm = bdm_ref[...]
    cst = (incl_ref, strict_ref, eye_ref, tri_ref, lvl_ref, hm_ref)

    for dst, sc, sl in ((rp_ref, rc_ref, rl_ref), (kp_ref, kc_ref, kl_ref), (vp_ref, vc_ref, vl_ref),
                        (wdp_ref, wdc_ref, wdl_ref), (adp_ref, adc_ref, adl_ref), (gdp_ref, gdc_ref, gdl_ref)):
        _fill_padded(dst, sc, sl, tc, tl)
    kkw, kaw, rkw = vec_ref[0:1, :], vec_ref[1:2, :], vec_ref[2:3, :]
    for (po, uo, n) in _row_tiles(tc, tl):
        def lerp(src, mu):
            x = src[po:po + n, :]
            nb = 0.5 * (src[po - 1:po - 1 + n, :] + src[po + 1:po + 1 + n, :])
            return x + mu * (nb - x)
        r = lerp(rp_ref, mur_ref[...])
        k = lerp(kp_ref, muk_ref[...])
        v = lerp(vp_ref, muv_ref[...])
        wd = lerp(wdp_ref, mux_ref[0:1, :])
        ad = lerp(adp_ref, mux_ref[1:2, :])
        gd = lerp(gdp_ref, mux_ref[2:3, :])
        tw = jnp.tanh(wd)
        g = _mm(_sigmoid(gd), gup_ref[...])
        kkv = k * kkw
        kk = kkv * lax.rsqrt(_mm_sel_r(kkv * kkv, bdm) + 1e-6)
        ksum = jnp.zeros_like(k)
        for d in range(2):
            logw = -_softplus(-(vec_ref[3 + d:4 + d, :] + _mm(tw, wup_ref[d]))) - 0.5
            lw = -jnp.exp(logw)
            a = _sigmoid(vec_ref[5 + d:6 + d, :] + _mm(ad, aup_ref[d]))
            kdir = k * (1.0 + (a - 1.0) * kaw)
            ksum = ksum + kdir
            dir_ref[d, 0, uo:uo + n, :] = lw
            dir_ref[d, 1, uo:uo + n, :] = kk * a
            dir_ref[d, 2, uo:uo + n, :] = kdir
        bonus = _mm_sel_r(r * ksum * rkw, bdm)
        rs_ref[uo:uo + n, :] = r
        vs_ref[uo:uo + n, :] = v
        kks_ref[uo:uo + n, :] = kk
        gs_ref[uo:uo + n, :] = g
        bvs_ref[uo:uo + n, :] = bonus * v

    def p1_body(s, carry):
        for d in range(2):
            c = _chunk_order(s, n_ctx, n_all, d)
            rows = pl.ds(pl.multiple_of(c * CH, CH), CH)
            outs = _rwkv_phase1(rs_ref[rows, :], vs_ref[rows, :], kks_ref[rows, :],
                                dir_ref[d, 0, rows, :], dir_ref[d, 1, rows, :], dir_ref[d, 2, rows, :], d, cst)
            for idx, val in enumerate(outs):
                p1_ref[d, s, idx] = val
        return carry
    lax.fori_loop(0, n_all, p1_body, 0)

    def p2_body(s, carry):
        new = []
        for d in range(2):
            st = carry[d]
            c = _chunk_order(s, n_ctx, n_all, d)
            rows = pl.ds(pl.multiple_of(c * CH, CH), CH)
            wt, u0, y0, a_rb, cum = (p1_ref[d, s, i] for i in range(5))
            last = CH - 1 if d == 0 else 0
            ctot = cum[last:last + 1, :]
            r, v = rs_ref[rows, :], vs_ref[rows, :]
            b, kdir = dir_ref[d, 1, rows, :], dir_ref[d, 2, rows, :]
            dec = jnp.exp(ctot - cum)
            rq = r * jnp.exp(cum)
            ws = _mm_nt(jnp.concatenate([wt, rq], axis=0), st)
            u = u0 - ws[:CH]
            y = ws[CH:] + y0 + _mm(a_rb, _bd(u, m0, m1))
            upd = _mm_tn(jnp.concatenate([u, v], axis=0), jnp.concatenate([b * dec, kdir * dec], axis=0))
            st = st * jnp.exp(ctot) + upd * bdm
            yref = yf_ref if d == 0 else yb_ref
            yref[rows, :] = y
            new.append(st)
        return tuple(new)
    z = jnp.zeros((LANES, LANES), F32)
    lax.fori_loop(0, n_all, p2_body, (z, z))

    gnw, gnb = vec_ref[7:8, :], vec_ref[8:9, :]
    for (po, uo, n) in _row_tiles(tc, tl):
        y = yf_ref[uo:uo + n, :] + yb_ref[uo:uo + n, :]
        mean = _mm_sel_r(y, bdm) * (1.0 / HD)
        yc = y - mean
        var = _mm_sel_r(yc * yc, bdm) * (1.0 / HD)
        out = (yc * lax.rsqrt(var + RWKV_GN_EPS) * gnw + gnb + bvs_ref[uo:uo + n, :]) * gs_ref[uo:uo + n, :]
        if uo < tc:
            oc_ref[uo:uo + n, :] = out
        else:
            ol_ref[uo - tc:uo - tc + n, :] = out


def _rwkv_mixer(rw_c, rw_l, mu, w0, w_up, a0, a_up, g_up, k_k, k_a, r_k, gn_w, gn_b, consts):
    bsz, tc, _ = rw_c.shape
    tl = rw_l.shape[1]
    tall = tc + tl
    n_all = tall // CH
    W = RWKV_WIDTH
    nb = W // LANES
    wup = jnp.zeros((2, LANES, W), F32)
    aup = jnp.zeros((2, LANES, W), F32)
    for d in range(2):
        wup = wup.at[d, d * RWKV_RANK:(d + 1) * RWKV_RANK, :].set(w_up[d])
        aup = aup.at[d, d * RWKV_RANK:(d + 1) * RWKV_RANK, :].set(a_up[d])
    vec = jnp.zeros((16, W), F32)
    vec = vec.at[0].set(k_k).at[1].set(k_a).at[2].set(r_k.reshape(-1))
    vec = vec.at[3].set(w0[0]).at[4].set(w0[1]).at[5].set(a0[0]).at[6].set(a0[1])
    vec = vec.at[7].set(gn_w).at[8].set(gn_b)
    mu_rkv = mu[:3 * W].reshape(3, 1, W)
    mux = jnp.zeros((SUBLANES, LANES), F32).at[0:3, :].set(mu[3 * W:].reshape(3, LANES))
    cm = lambda *blk: pl.BlockSpec(blk, lambda b, p: (0,) * len(blk))

    def tok(t, col0, per_pair=True):
        if per_pair:
            return pl.BlockSpec((None, t, LANES), lambda b, p: (b, 0, col0 + p))
        return pl.BlockSpec((None, t, LANES), lambda b, p: (b, 0, col0))
    stream = lambda t: [tok(t, 0), tok(t, nb), tok(t, 2 * nb), tok(t, 3 * nb, False),
                        tok(t, 3 * nb + 1, False), tok(t, 3 * nb + 2, False)]
    in_specs = stream(tc) + stream(tl) + [
        pl.BlockSpec((None, 1, LANES), lambda b, p: (0, 0, p)),
        pl.BlockSpec((None, 1, LANES), lambda b, p: (1, 0, p)),
        pl.BlockSpec((None, 1, LANES), lambda b, p: (2, 0, p)),
        cm(SUBLANES, LANES),
        pl.BlockSpec((2, LANES, LANES), lambda b, p: (0, 0, p)),
        pl.BlockSpec((2, LANES, LANES), lambda b, p: (0, 0, p)),
        pl.BlockSpec((RWKV_G_RANK, LANES), lambda b, p: (0, p)),
        pl.BlockSpec((16, LANES), lambda b, p: (0, p)),
        cm(2, CH, LANES), cm(2, CH, LANES), cm(CH, LANES), cm(2, CH, CH),
        cm(2, 6, CH, LANES), cm(2, 1, LANES), cm(LANES, LANES)]
    pad_rows = tall + 3 * GAP
    scratch = [pltpu.VMEM((pad_rows, LANES), F32)] * 6 + [pltpu.VMEM((tall, LANES), F32)] * 5 + [
        pltpu.VMEM((2, 3, tall, LANES), F32),
        pltpu.VMEM((2, n_all, 5, CH, LANES), F32),
        pltpu.VMEM((tall, LANES), F32), pltpu.VMEM((tall, LANES), F32)]
    return pl.pallas_call(
        _rwkv_kernel,
        grid=(bsz, N_PAIRS),
        in_specs=in_specs,
        out_specs=[pl.BlockSpec((None, tc, LANES), lambda b, p: (b, 0, p)),
                   pl.BlockSpec((None, tl, LANES), lambda b, p: (b, 0, p))],
        out_shape=[jax.ShapeDtypeStruct((bsz, tc, W), F32),
                   jax.ShapeDtypeStruct((bsz, tl, W), F32)],
        scratch_shapes=scratch,
        compiler_params=pltpu.CompilerParams(dimension_semantics=("arbitrary", "arbitrary"),
                                             vmem_limit_bytes=VMEM_LIMIT),
        name="rwkv_mixer",
    )(*([rw_c] * 6), *([rw_l] * 6), mu_rkv, mu_rkv, mu_rkv, mux, wup, aup, g_up, vec,
      consts["incl"], consts["strict"], consts["eye"], consts["tri"], consts["lvl"], consts["hm"],
      consts["bdm"])


def _lru_kernel(xc_ref, gc_ref, xl_ref, gl_ref, cw_ref, vec_ref, wcat_ref, bcat_ref,
                oc_ref, ol_ref, xp_ref, xv_ref, ab_ref, hf_ref, hb_ref):
    tc, tl = xc_ref.shape[0], xl_ref.shape[0]
    tall = tc + tl
    rows = tl // GRID_W
    W = xc_ref.shape[1]
    z = jnp.zeros((GAP, W), F32)
    xp_ref[0:GAP, :] = z
    xp_ref[GAP:GAP + tc, :] = xc_ref[...]
    xp_ref[GAP + tc:2 * GAP + tc, :] = z
    base = 2 * GAP + tc
    for c in range(GRID_W):
        xp_ref[base + c * rows:base + (c + 1) * rows, :] = xl_ref[pl.ds(c, rows, stride=GRID_W), :]
    xp_ref[base + tl:base + tl + GAP, :] = z
    cb = vec_ref[0:1, :]
    for (po, uo, n) in _row_tiles(tc, tl):
        acc = cb + cw_ref[0:1, :] * xp_ref[po - 2:po - 2 + n, :]
        for tap in range(1, CONV_W):
            acc = acc + cw_ref[tap:tap + 1, :] * xp_ref[po - 2 + tap:po - 2 + tap + n, :]
        xv_ref[uo:uo + n, :] = acc
        gates = _mm(acc, wcat_ref[...]) + bcat_ref[...]
        for d in range(2):
            rg = _sigmoid(gates[:, (2 * d) * W:(2 * d + 1) * W])
            ig = _sigmoid(gates[:, (2 * d + 1) * W:(2 * d + 2) * W])
            log_a = -LRU_C * rg * _softplus(-vec_ref[1 + d:2 + d, :])
            a = jnp.exp(log_a)
            mult = jnp.sqrt(-jnp.tanh(log_a) * (1.0 + a * a))
            ab_ref[d, 0, uo:uo + n, :] = a
            ab_ref[d, 1, uo:uo + n, :] = mult * (ig * acc)

    sub = lax.broadcasted_iota(jnp.int32, (SUBLANES, W), 0)
    n_tiles_c, n_tiles = tc // SUBLANES, tall // SUBLANES

    def tile_scan(a, b, d):
        for sh in (1, 2, 4):
            if d == 0:
                ok = sub >= sh
                a_s = jnp.where(ok, pltpu.roll(a, sh, 0), 1.0)
                b_s = jnp.where(ok, pltpu.roll(b, sh, 0), 0.0)
            else:
                ok = sub < SUBLANES - sh
                a_s = jnp.where(ok, pltpu.roll(a, SUBLANES - sh, 0), 1.0)
                b_s = jnp.where(ok, pltpu.roll(b, SUBLANES - sh, 0), 0.0)
            b = b + a * b_s
            a = a * a_s
        return a, b

    def scan_body(s, carry):
        hf, hb = carry
        r0 = pl.multiple_of(s * SUBLANES, SUBLANES)
        a, b = tile_scan(ab_ref[0, 0, pl.ds(r0, SUBLANES), :], ab_ref[0, 1, pl.ds(r0, SUBLANES), :], 0)
        h = b + a * hf
        hf_ref[pl.ds(r0, SUBLANES), :] = h
        hf = jnp.broadcast_to(h[SUBLANES - 1:SUBLANES, :], (SUBLANES, W))
        tb = jnp.where(s < n_tiles_c, n_tiles_c - 1 - s, n_tiles + n_tiles_c - 1 - s)
        r1 = pl.multiple_of(tb * SUBLANES, SUBLANES)
        a, b = tile_scan(ab_ref[1, 0, pl.ds(r1, SUBLANES), :], ab_ref[1, 1, pl.ds(r1, SUBLANES), :], 1)
        h = b + a * hb
        hb_ref[pl.ds(r1, SUBLANES), :] = h
        hb = jnp.broadcast_to(h[0:1, :], (SUBLANES, W))
        return hf, hb
    z8 = jnp.zeros((SUBLANES, W), F32)
    lax.fori_loop(0, n_tiles, scan_body, (z8, z8))

    oc_ref[...] = (hf_ref[0:tc, :] + hb_ref[0:tc, :]) * _gelu_tanh(gc_ref[...])
    for c in range(GRID_W):
        h = hf_ref[tc + c * rows:tc + (c + 1) * rows, :] + hb_ref[tc + c * rows:tc + (c + 1) * rows, :]
        ol_ref[pl.ds(c, rows, stride=GRID_W), :] = h * _gelu_tanh(gl_ref[pl.ds(c, rows, stride=GRID_W), :])


def _lru_mixer(x_c, g_c, x_l, g_l, conv_w, conv_b, w_a, b_a, w_x, b_x, lam):
    bsz, tc, W = x_c.shape
    tl = x_l.shape[1]
    tall = tc + tl
    nh = W // LANES
    bph = LANES // LRU_BW
    wcat = jnp.zeros((nh, LANES, 4 * LANES), F32)
    bcat = jnp.zeros((nh, 1, 4 * LANES), F32)
    for hh in range(nh):
        for d in range(2):
            for gi, (wsrc, bsrc) in enumerate(((w_a, b_a), (w_x, b_x))):
                c0 = (2 * d + gi) * LANES
                bcat = bcat.at[hh, 0, c0:c0 + LANES].set(bsrc[d, hh * LANES:(hh + 1) * LANES])
                for n in range(bph):
                    wcat = wcat.at[hh, n * LRU_BW:(n + 1) * LRU_BW,
                                   c0 + n * LRU_BW:c0 + (n + 1) * LRU_BW].set(wsrc[d, hh * bph + n])
    vec = jnp.zeros((SUBLANES, W), F32).at[0].set(conv_b).at[1].set(lam[0]).at[2].set(lam[1])
    tok = lambda t: pl.BlockSpec((None, t, LANES), lambda b, h: (b, 0, h))
    return pl.pallas_call(
        _lru_kernel,
        grid=(bsz, nh),
        in_specs=[tok(tc), tok(tc), tok(tl), tok(tl),
                  pl.BlockSpec((CONV_W, LANES), lambda b, h: (0, h)),
                  pl.BlockSpec((SUBLANES, LANES), lambda b, h: (0, h)),
                  pl.BlockSpec((None, LANES, 4 * LANES), lambda b, h: (h, 0, 0)),
                  pl.BlockSpec((None, 1, 4 * LANES), lambda b, h: (h, 0, 0))],
        out_specs=[tok(tc), tok(tl)],
        out_shape=[jax.ShapeDtypeStruct((bsz, tc, W), F32), jax.ShapeDtypeStruct((bsz, tl, W), F32)],
        scratch_shapes=[pltpu.VMEM((tall + 3 * GAP, LANES), F32), pltpu.VMEM((tall, LANES), F32),
                        pltpu.VMEM((2, 2, tall, LANES), F32),
                        pltpu.VMEM((tall, LANES), F32), pltpu.VMEM((tall, LANES), F32)],
        compiler_params=pltpu.CompilerParams(dimension_semantics=("arbitrary", "arbitrary"),
                                             vmem_limit_bytes=VMEM_LIMIT),
        name="lru_mixer",
    )(x_c, g_c, x_l, g_l, conv_w, vec, wcat.astype(BF16), bcat)


def _finish_kernel(x_ref, gdn_ref, lru_ref, rwk_ref, m2_ref, m3_ref, m4_ref, m5_ref, nrm_ref,
                   wo_ref, up_ref, dn_ref, o_ref):
    x = x_ref[...]
    o = (jnp.dot(gdn_ref[...].astype(BF16), wo_ref[0:GDN_WIDTH, :], preferred_element_type=F32)
         + jnp.dot(lru_ref[...].astype(BF16), wo_ref[GDN_WIDTH:GDN_WIDTH + LRU_WIDTH, :],
                   preferred_element_type=F32)
         + jnp.dot(rwk_ref[...].astype(BF16), wo_ref[GDN_WIDTH + LRU_WIDTH:, :], preferred_element_type=F32))
    x = x + m2_ref[...] * _rms(o, nrm_ref[0:1, :])
    h = (_rms(x, nrm_ref[1:2, :]) * (1.0 + m4_ref[...]) + m3_ref[...]).astype(BF16)
    f = jnp.zeros_like(x)
    fc = 1024
    for j in range(D_FF // fc):
        a = jnp.maximum(jnp.dot(h, up_ref[:, j * fc:(j + 1) * fc], preferred_element_type=F32), 0.0)
        f = f + jnp.dot((a * a).astype(BF16), dn_ref[j * fc:(j + 1) * fc, :], preferred_element_type=F32)
    o_ref[...] = x + m5_ref[...] * _rms(f, nrm_ref[2:3, :])


def _finish(x2, gdn, lru, rwk, mod_rows, rows_per_mod, norms, wo_bf, up_bf, dn_bf):
    n = x2.shape[0]
    tm = 256
    tiles_per_mod = rows_per_mod // tm
    modspec = lambda k: pl.BlockSpec((None, 1, D_MODEL), lambda i: (6 * (i // tiles_per_mod) + k, 0, 0))
    cm = lambda *blk: pl.BlockSpec(blk, lambda i: (0,) * len(blk))
    return pl.pallas_call(
        _finish_kernel,
        grid=(n // tm,),
        in_specs=[pl.BlockSpec((tm, D_MODEL), lambda i: (i, 0)),
                  pl.BlockSpec((tm, GDN_WIDTH), lambda i: (i, 0)),
                  pl.BlockSpec((tm, LRU_WIDTH), lambda i: (i, 0)),
                  pl.BlockSpec((tm, RWKV_WIDTH), lambda i: (i, 0)),
                  modspec(2), modspec(3), modspec(4), modspec(5),
                  cm(SUBLANES, D_MODEL), cm(D_MODEL, D_MODEL), cm(D_MODEL, D_FF), cm(D_FF, D_MODEL)],
        out_specs=pl.BlockSpec((tm, D_MODEL), lambda i: (i, 0)),
        out_shape=jax.ShapeDtypeStruct((n, D_MODEL), F32),
        compiler_params=pltpu.CompilerParams(dimension_semantics=("arbitrary",),
                                             vmem_limit_bytes=VMEM_LIMIT),
        name="finish",
    )(x2, gdn, lru, rwk, mod_rows, mod_rows, mod_rows, mod_rows, norms, wo_bf, up_bf, dn_bf)


def _arrange_w_in(w):
    s = np.cumsum([0, 3 * GDN_WIDTH, GDN_WIDTH, 2 * GDN_HEADS, 2 * GDN_HEADS, LRU_WIDTH, LRU_WIDTH, RWKV_IN])
    pad = jnp.zeros((w.shape[0], BA_W - 4 * GDN_HEADS), w.dtype)
    return jnp.concatenate([w[:, s[0]:s[2]], w[:, s[2]:s[4]], pad, w[:, s[4]:]], axis=1)


def kernel(x, c, ctx, c_ctx, ada_w, ada_b, norm_mix_pre, norm_mix_post, norm_ffn_pre, norm_ffn_post, w_in, gdn_conv, gdn_a_log, gdn_dt_bias, gdn_norm, lru_conv, lru_conv_b, lru_wa, lru_ba, lru_wx, lru_bx, lru_lambda, rwkv_mu, rwkv_w0, rwkv_w_up, rwkv_a0, rwkv_a_up, rwkv_g_up, rwkv_k_k, rwkv_k_a, rwkv_r_k, rwkv_gn_w, rwkv_gn_b, w_out, ffn_up, ffn_down):
    bsz, tl, _ = x.shape
    tc = ctx.shape[1]
    depth = w_in.shape[0]
    consts = {k: jnp.asarray(v) for k, v in _pair_consts().items()}

    cvec = jnp.zeros((16, D_MODEL), F32).at[0:bsz].set(c).at[bsz].set(c_ctx)
    mods = _ada_mod(cvec, ada_w, ada_b).reshape(depth, 16, 6, D_MODEL)

    xl = x.reshape(bsz * tl, D_MODEL)
    xc = ctx.reshape(bsz * tc, D_MODEL)
    for i in range(depth):
        mod_l = mods[i, 0:bsz].reshape(bsz * 6, 1, D_MODEL)
        mod_c = mods[i, bsz:bsz + 1].reshape(6, 1, D_MODEL)
        w_bf = _arrange_w_in(w_in[i]).astype(BF16)
        pl_ = _inproj(xl, mod_l, tl, norm_mix_pre[i], w_bf)
        pc_ = _inproj(xc, mod_c, bsz * tc, norm_mix_pre[i], w_bf)
        r3 = lambda a, t: a.reshape(bsz, t, a.shape[-1])
        qkv_l, z_l, ba_l, lx_l, lg_l, rw_l = (r3(a, tl) for a in pl_)
        qkv_c, z_c, ba_c, lx_c, lg_c, rw_c = (r3(a, tc) for a in pc_)

        gdn_c, gdn_l = _gdn_mixer(qkv_c, z_c, ba_c, qkv_l, z_l, ba_l, gdn_conv[i], gdn_a_log[i],
                                  gdn_dt_bias[i], gdn_norm[i], consts)
        lru_c, lru_l = _lru_mixer(lx_c, lg_c, lx_l, lg_l, lru_conv[i], lru_conv_b[i], lru_wa[i], lru_ba[i],
                                  lru_wx[i], lru_bx[i], lru_lambda[i])
        rwk_c, rwk_l = _rwkv_mixer(rw_c, rw_l, rwkv_mu[i], rwkv_w0[i], rwkv_w_up[i], rwkv_a0[i],
                                   rwkv_a_up[i], rwkv_g_up[i], rwkv_k_k[i], rwkv_k_a[i], rwkv_r_k[i],
                                   rwkv_gn_w[i], rwkv_gn_b[i], consts)

        norms = jnp.zeros((SUBLANES, D_MODEL), F32).at[0].set(norm_mix_post[i]).at[1].set(
            norm_ffn_pre[i]).at[2].set(norm_ffn_post[i])
        wo_bf, up_bf, dn_bf = w_out[i].astype(BF16), ffn_up[i].astype(BF16), ffn_down[i].astype(BF16)
        f2 = lambda a: a.reshape(-1, a.shape[-1])
        xl = _finish(xl, f2(gdn_l), f2(lru_l), f2(rwk_l), mod_l, tl, norms, wo_bf, up_bf, dn_bf)
        if i < depth - 1:
            xc = _finish(xc, f2(gdn_c), f2(lru_c), f2(rwk_c), mod_c, bsz * tc, norms, wo_bf, up_bf, dn_bf)
    return xl.reshape(bsz, tl, D_MODEL)
```

```python
import functools

import numpy as np
import jax
import jax.numpy as jnp
from jax import lax
from jax.experimental import pallas as pl
from jax.experimental.pallas import tpu as pltpu

F32 = jnp.float32
BF16 = jnp.bfloat16

LANES = 128
SUBLANES = 8
VMEM_LIMIT = 56 * 1024 * 1024

D_MODEL = 1024
DEPTH = 2
GRID_W = 64
CONV_W = 4
EPS = 1e-6
D_FF = 4 * D_MODEL
HD = 64
GDN_WIDTH = 3 * D_MODEL // 8
GDN_HEADS = GDN_WIDTH // HD
LRU_WIDTH = D_MODEL // 4
LRU_BLOCKS = 4
LRU_BW = LRU_WIDTH // LRU_BLOCKS
LRU_C = 8.0
RWKV_WIDTH = D_MODEL - GDN_WIDTH - LRU_WIDTH
RWKV_HEADS = RWKV_WIDTH // HD
RWKV_RANK = 64
RWKV_G_RANK = 128
RWKV_GN_EPS = 6.4e-4
RWKV_IN = 3 * RWKV_WIDTH + 2 * RWKV_RANK + 2 * RWKV_RANK + RWKV_G_RANK
N_PAIRS = GDN_HEADS // 2
BA_W = LANES
P_OFF = np.cumsum([0, 3 * GDN_WIDTH, GDN_WIDTH, BA_W, LRU_WIDTH, LRU_WIDTH, RWKV_IN])
D_INP = int(P_OFF[-1])

CH = 64
ROWT = 256
GAP = SUBLANES


def _mm(a, b):
    return jnp.dot(a.astype(BF16), b.astype(BF16), preferred_element_type=F32)


def _mm_nt(a, b):
    return lax.dot_general(a.astype(BF16), b.astype(BF16), (((1,), (1,)), ((), ())),
                           preferred_element_type=F32)


def _mm_tn(a, b):
    return lax.dot_general(a.astype(BF16), b.astype(BF16), (((0,), (0,)), ((), ())),
                           preferred_element_type=F32)


def _split2(x):
    hi = x.astype(BF16)
    lo = (x - hi.astype(F32)).astype(BF16)
    return hi, lo


def _split3(x):
    hi = x.astype(BF16)
    r = x - hi.astype(F32)
    mid = r.astype(BF16)
    lo = (r - mid.astype(F32)).astype(BF16)
    return hi, mid, lo


def _mm3(a, b):
    ah, al = _split2(a)
    bh, bl = _split2(b)
    d = functools.partial(jnp.dot, preferred_element_type=F32)
    return d(ah, bh) + (d(ah, bl) + d(al, bh))


def _mm_sel_l(m01, x):
    mb = m01.astype(BF16)
    h, m, l = _split3(x)
    d = functools.partial(jnp.dot, preferred_element_type=F32)
    return d(mb, h) + (d(mb, m) + d(mb, l))


def _mm_sel_r(x, m01):
    mb = m01.astype(BF16)
    h, m, l = _split3(x)
    d = functools.partial(jnp.dot, preferred_element_type=F32)
    return d(h, mb) + (d(m, mb) + d(l, mb))


def _sigmoid(x):
    return 1.0 / (1.0 + jnp.exp(-x))


def _silu(x):
    return x * _sigmoid(x)


def _softplus(x):
    return jnp.maximum(x, 0.0) + jnp.log(1.0 + jnp.exp(-jnp.abs(x)))


def _gelu_tanh(x):
    return 0.5 * x * (1.0 + jnp.tanh(0.7978845608028654 * (x + 0.044715 * (x * x * x))))


def _bd(y, m0, m1):
    return jnp.concatenate([y * m0, y * m1], axis=0)


def _pair_consts():
    i = np.arange(CH)[:, None]
    j = (np.arange(LANES) % HD)[None, :]
    t = np.arange(CH)[None, :]
    incl = np.stack([i >= j, i <= j]).astype(np.float32)
    strict = np.stack([i > j, i < j]).astype(np.float32)
    eye = (i == j).astype(np.float32)
    tri = np.stack([i >= t, i <= t]).astype(np.float32)
    lvls = []
    for d in range(2):
        per = []
        for m in (1, 2, 4, 8, 16, 32):
            same = (i // (2 * m)) == (j // (2 * m))
            lo_i, lo_j = (i % (2 * m)) < m, (j % (2 * m)) < m
            off = same & (~lo_i) & lo_j if d == 0 else same & lo_i & (~lo_j)
            per.append(off)
        lvls.append(np.stack(per))
    lvl = np.stack(lvls).astype(np.float32)
    lane = np.arange(LANES)
    hm = np.stack([lane < HD, lane >= HD]).astype(np.float32)[:, None, :]
    bdm = ((np.arange(LANES)[:, None] // HD) == (lane[None, :] // HD)).astype(np.float32)
    r = np.arange(ROWT)[:, None]
    c = np.arange(ROWT)[None, :]
    same = (r // CH) == (c // CH)
    tri_t = np.stack([same & (r >= c), same & (r <= c)]).astype(np.float32)
    ones_t = same.astype(np.float32)
    return dict(incl=incl, strict=strict, eye=eye, tri=tri, lvl=lvl, hm=hm, bdm=bdm, tri_t=tri_t, ones_t=ones_t)


def _device_consts():
    out = {k: jnp.asarray(v) for k, v in _pair_consts().items()}
    for k in ("tri_t", "ones_t"):
        out[k] = out[k].astype(BF16)
    return out


def _inv_levels(mats, dirs, lvl_ref, m0, m1):
    es = [-(a * lvl_ref[d, 0]) for a, d in zip(mats, dirs)]
    for k in range(1, 6):
        offs = [a * lvl_ref[d, k] for a, d in zip(mats, dirs)]
        xs = [off + _mm(e, _bd(off, m0, m1)) for e, off in zip(es, offs)]
        es = [e - x - _mm(x, _bd(e, m0, m1)) for e, x in zip(es, xs)]
    return es


def _group_size(n_ctx):
    return 4 if n_ctx % 4 == 0 else (2 if n_ctx % 2 == 0 else 1)


def _chunk_order(s, n_ctx, n_all, d):
    if d == 0:
        return s
    return jnp.where(s < n_ctx, n_ctx - 1 - s, n_all + n_ctx - 1 - s)


def _fill_padded(dst_ref, src_c_ref, src_l_ref, tc, tl):
    w = dst_ref.shape[1]
    z = jnp.zeros((GAP, w), F32)
    dst_ref[0:GAP, :] = z
    dst_ref[GAP:GAP + tc, :] = src_c_ref[...]
    dst_ref[GAP + tc:2 * GAP + tc, :] = z
    dst_ref[2 * GAP + tc:2 * GAP + tc + tl, :] = src_l_ref[...]
    dst_ref[2 * GAP + tc + tl:3 * GAP + tc + tl, :] = z


def _row_tiles(tc, tl):
    out = []
    for base_p, base_u, n in ((GAP, 0, tc), (2 * GAP + tc, tc, tl)):
        for t0 in range(0, n, ROWT):
            out.append((base_p + t0, base_u + t0, min(ROWT, n - t0)))
    return out


def _ada_kernel(c_ref, w_ref, b_ref, o_ref):
    c = c_ref[...]
    o_ref[...] = _mm(_silu(c), w_ref[...]) + b_ref[...]


def _ada_mod(cvec, ada_w, ada_b):
    L = ada_w.shape[0]
    n = ada_w.shape[2]
    tn = 1536
    return pl.pallas_call(
        _ada_kernel,
        grid=(L, n // tn),
        in_specs=[pl.BlockSpec((16, D_MODEL), lambda l, j: (0, 0)),
                  pl.BlockSpec((None, D_MODEL, tn), lambda l, j: (l, 0, j)),
                  pl.BlockSpec((None, 1, tn), lambda l, j: (l, 0, j))],
        out_specs=pl.BlockSpec((None, 16, tn), lambda l, j: (l, 0, j)),
        out_shape=jax.ShapeDtypeStruct((L, 16, n), F32),
        compiler_params=pltpu.CompilerParams(dimension_semantics=("arbitrary", "arbitrary"),
                                             vmem_limit_bytes=VMEM_LIMIT),
        name="ada_mod",
    )(cvec, ada_w, ada_b.reshape(L, 1, n))


def _rms(x, g):
    return x * lax.rsqrt(jnp.mean(x * x, axis=-1, keepdims=True) + EPS) * g


def _inproj_kernel(x_ref, sh_ref, sc_ref, g_ref, w_ref, qkv_ref, z_ref, ba_ref, lx_ref, lg_ref, rw_ref):
    h = _rms(x_ref[...], g_ref[...]) * (1.0 + sc_ref[...]) + sh_ref[...]
    p = jnp.dot(h.astype(BF16), w_ref[...], preferred_element_type=F32)
    for ref, k in zip((qkv_ref, z_ref, ba_ref, lx_ref, lg_ref, rw_ref), range(6)):
        ref[...] = p[:, int(P_OFF[k]):int(P_OFF[k + 1])]


def _inproj(x2, mod_rows, rows_per_mod, g, w_bf):
    n = x2.shape[0]
    tm = 256
    tiles_per_mod = rows_per_mod // tm
    widths = [int(P_OFF[k + 1] - P_OFF[k]) for k in range(6)]
    return pl.pallas_call(
        _inproj_kernel,
        grid=(n // tm,),
        in_specs=[pl.BlockSpec((tm, D_MODEL), lambda i: (i, 0)),
                  pl.BlockSpec((None, 1, D_MODEL), lambda i: (6 * (i // tiles_per_mod), 0, 0)),
                  pl.BlockSpec((None, 1, D_MODEL), lambda i: (6 * (i // tiles_per_mod) + 1, 0, 0)),
                  pl.BlockSpec((1, D_MODEL), lambda i: (0, 0)),
                  pl.BlockSpec((D_MODEL, D_INP), lambda i: (0, 0))],
        out_specs=[pl.BlockSpec((tm, w), lambda i: (i, 0)) for w in widths],
        out_shape=[jax.ShapeDtypeStruct((n, w), F32) for w in widths],
        compiler_params=pltpu.CompilerParams(dimension_semantics=("arbitrary",),
                                             vmem_limit_bytes=VMEM_LIMIT),
        name="inproj",
    )(x2, mod_rows, mod_rows, g.reshape(1, D_MODEL), w_bf)


def _gdn_kernel(qc_ref, kc_ref, vc_ref, zc_ref, bac_ref, ql_ref, kl_ref, vl_ref, zl_ref, bal_ref,
                cwq_ref, cwk_ref, cwv_ref, aux_ref, e_ref,
                incl_ref, strict_ref, lvl_ref, hm_ref, bdm_ref, trit_ref, onest_ref,
                oc_ref, ol_ref,
                qp_ref, kp_ref, vp_ref, qs_ref, ks_ref, vs_ref, dir_ref, of_ref, ob_ref):
    tc, tl = qc_ref.shape[0], ql_ref.shape[0]
    tall = tc + tl
    n_ctx, n_all = tc // CH, tall // CH
    grp = _group_size(n_ctx)
    m0, m1 = hm_ref[0], hm_ref[1]
    m0w = jnp.concatenate([m0, m0], axis=1)
    m1w = jnp.concatenate([m1, m1], axis=1)
    bdm = bdm_ref[...]

    _fill_padded(qp_ref, qc_ref, ql_ref, tc, tl)
    _fill_padded(kp_ref, kc_ref, kl_ref, tc, tl)
    _fill_padded(vp_ref, vc_ref, vl_ref, tc, tl)
    lane = lax.broadcasted_iota(jnp.int32, (1, LANES), 1)
    alog, dtb = aux_ref[0:1, :], aux_ref[1:2, :]
    for (po, uo, n) in _row_tiles(tc, tl):
        def conv(src, cw):
            acc = cw[0:1, :] * src[po - 2:po - 2 + n, :]
            for tap in range(1, CONV_W):
                acc = acc + cw[tap:tap + 1, :] * src[po - 2 + tap:po - 2 + tap + n, :]
            return _silu(acc)
        q = conv(qp_ref, cwq_ref)
        k = conv(kp_ref, cwk_ref)
        v = conv(vp_ref, cwv_ref)
        q = q * lax.rsqrt(_mm_sel_r(q * q, bdm) + 1e-6) * (HD ** -0.5)
        k = k * lax.rsqrt(_mm_sel_r(k * k, bdm) + 1e-6)
        qs_ref[uo:uo + n, :] = q
        ks_ref[uo:uo + n, :] = k
        vs_ref[uo:uo + n, :] = v
        ba = bac_ref[uo:uo + n, :] if uo < tc else bal_ref[uo - tc:uo - tc + n, :]
        beta = _sigmoid(ba)
        gval = -jnp.exp(alog) * _softplus(ba + dtb)
        bgv = jnp.where(lane < 2 * GDN_HEADS, beta, gval)
        ex = _mm_sel_r(bgv, e_ref[...])
        incl_t = [jnp.concatenate([incl_ref[dd]] * (n // CH), axis=0) for dd in range(2)]
        for d in range(2):
            g = ex[:, (2 + d) * LANES:(3 + d) * LANES]
            gc = _mm_sel_l(trit_ref[d, 0:n, 0:n], g)
            gr = _mm_sel_l(onest_ref[0:n, 0:n], g * incl_t[1 - d])
            dir_ref[d, 0, uo:uo + n, :] = ex[:, d * LANES:(d + 1) * LANES]
            dir_ref[d, 1, uo:uo + n, :] = gc
            dir_ref[d, 2, uo:uo + n, :] = jnp.exp(jnp.minimum(gc - gr, 0.0)) * incl_t[d]

    def group_body(gi, carry):
        probs = []
        for i in range(grp):
            for d in range(2):
                c = _chunk_order(gi * grp + i, n_ctx, n_all, d)
                rows = pl.ds(pl.multiple_of(c * CH, CH), CH)
                q, k, v = qs_ref[rows, :], ks_ref[rows, :], vs_ref[rows, :]
                beta, gc, dec = dir_ref[d, 0, rows, :], dir_ref[d, 1, rows, :], dir_ref[d, 2, rows, :]
                la = _mm_nt(jnp.concatenate([k, q], axis=0), _bd(k, m0, m1))
                probs.append(dict(d=d, rows=rows, q=q, k=k, v=v, beta=beta, gc=gc,
                                  lmat=la[:CH] * dec * strict_ref[d] * beta, attn=la[CH:] * dec))
        es = _inv_levels([p["lmat"] for p in probs], [p["d"] for p in probs], lvl_ref, m0, m1)
        for p, e in zip(probs, es):
            eg = jnp.exp(p["gc"])
            kb = p["k"] * p["beta"]
            rhs = jnp.concatenate([p["v"] * p["beta"], kb * eg], axis=1)
            sol = rhs + _mm(e, _bd(rhs, m0w, m1w))
            last = CH - 1 if p["d"] == 0 else 0
            glast = p["gc"][last:last + 1, :]
            p.update(u=sol[:, :LANES], w=sol[:, LANES:], qg=p["q"] * eg,
                     kd=p["k"] * jnp.exp(glast - p["gc"]), egl=jnp.exp(glast))
        st = list(carry)
        for p in probs:
            d = p["d"]
            wq = _mm(jnp.concatenate([p["w"], p["qg"]], axis=0), st[d])
            vn = p["u"] - wq[:CH]
            o = wq[CH:] + _mm(p["attn"], _bd(vn, m0, m1))
            st[d] = st[d] * p["egl"] + _mm_tn(p["kd"], vn) * bdm
            oref = of_ref if d == 0 else ob_ref
            oref[p["rows"], :] = o
        return tuple(st)
    z = jnp.zeros((LANES, LANES), F32)
    lax.fori_loop(0, n_all // grp, group_body, (z, z))

    nw = aux_ref[2:3, :]
    for (po, uo, n) in _row_tiles(tc, tl):
        o = of_ref[uo:uo + n, :] + ob_ref[uo:uo + n, :]
        ms = _mm_sel_r(o * o, bdm) * (1.0 / HD)
        if uo < tc:
            oc_ref[uo:uo + n, :] = o * lax.rsqrt(ms + EPS) * nw * _silu(zc_ref[uo:uo + n, :])
        else:
            lo = uo - tc
            ol_ref[lo:lo + n, :] = o * lax.rsqrt(ms + EPS) * nw * _silu(zl_ref[lo:lo + n, :])


def _gdn_expand_consts():
    e = np.zeros((N_PAIRS, BA_W, 4 * LANES), np.float32)
    for p in range(N_PAIRS):
        for blk in range(4):
            d, is_g = blk % 2, blk // 2
            for h in range(2):
                col = is_g * 2 * GDN_HEADS + d * GDN_HEADS + 2 * p + h
                e[p, col, blk * LANES + h * HD:blk * LANES + (h + 1) * HD] = 1.0
    return e


def _gdn_mixer(qkv_c, z_c, ba_c, qkv_l, z_l, ba_l, conv_w, a_log, dt_bias, norm_w, consts):
    bsz, tc, _ = qkv_c.shape
    tl = qkv_l.shape[1]
    tall = tc + tl
    n_all = tall // CH
    aux = jnp.zeros((SUBLANES, LANES), F32)
    aux = aux.at[0, 2 * GDN_HEADS:4 * GDN_HEADS].set(a_log.reshape(-1))
    aux = aux.at[1, 2 * GDN_HEADS:4 * GDN_HEADS].set(dt_bias.reshape(-1))
    aux = aux.at[2, :].set(jnp.tile(norm_w, 2))
    e = jnp.asarray(_gdn_expand_consts())
    cm = lambda *blk: pl.BlockSpec(blk, lambda b, p: (0,) * len(blk))

    def tok(t, col0):
        return pl.BlockSpec((None, t, LANES), lambda b, p: (b, 0, col0 + p))
    in_specs = [tok(tc, 0), tok(tc, N_PAIRS), tok(tc, 2 * N_PAIRS), tok(tc, 0),
                pl.BlockSpec((None, tc, BA_W), lambda b, p: (b, 0, 0)),
                tok(tl, 0), tok(tl, N_PAIRS), tok(tl, 2 * N_PAIRS), tok(tl, 0),
                pl.BlockSpec((None, tl, BA_W), lambda b, p: (b, 0, 0)),
                pl.BlockSpec((CONV_W, LANES), lambda b, p: (0, p)),
                pl.BlockSpec((CONV_W, LANES), lambda b, p: (0, N_PAIRS + p)),
                pl.BlockSpec((CONV_W, LANES), lambda b, p: (0, 2 * N_PAIRS + p)),
                cm(SUBLANES, LANES),
                pl.BlockSpec((None, BA_W, 4 * LANES), lambda b, p: (p, 0, 0)),
                cm(2, CH, LANES), cm(2, CH, LANES), cm(2, 6, CH, LANES), cm(2, 1, LANES), cm(LANES, LANES),
                cm(2, ROWT, ROWT), cm(ROWT, ROWT)]
    pad_rows = tall + 3 * GAP
    scratch = [pltpu.VMEM((pad_rows, LANES), F32)] * 3 + [pltpu.VMEM((tall, LANES), F32)] * 3 + [
        pltpu.VMEM((2, 3, tall, LANES), F32),
        pltpu.VMEM((tall, LANES), F32), pltpu.VMEM((tall, LANES), F32)]
    return pl.pallas_call(
        _gdn_kernel,
        grid=(bsz, N_PAIRS),
        in_specs=in_specs,
        out_specs=[pl.BlockSpec((None, tc, LANES), lambda b, p: (b, 0, p)),
                   pl.BlockSpec((None, tl, LANES), lambda b, p: (b, 0, p))],
        out_shape=[jax.ShapeDtypeStruct((bsz, tc, GDN_WIDTH), F32),
                   jax.ShapeDtypeStruct((bsz, tl, GDN_WIDTH), F32)],
        scratch_shapes=scratch,
        compiler_params=pltpu.CompilerParams(dimension_semantics=("arbitrary", "arbitrary"),
                                             vmem_limit_bytes=VMEM_LIMIT),
        name="gdn_mixer",
    )(qkv_c, qkv_c, qkv_c, z_c, ba_c, qkv_l, qkv_l, qkv_l, z_l, ba_l,
      conv_w, conv_w, conv_w, aux, e,
      consts["incl"], consts["strict"], consts["lvl"], consts["hm"], consts["bdm"],
      consts["tri_t"], consts["ones_t"])


def _rwkv_kernel(rc_ref, kc_ref, vc_ref, wdc_ref, adc_ref, gdc_ref,
                 rl_ref, kl_ref, vl_ref, wdl_ref, adl_ref, gdl_ref,
                 mur_ref, muk_ref, muv_ref, mux_ref, wup_ref, aup_ref, gup_ref, vec_ref,
                 incl_ref, strict_ref, lvl_ref, hm_ref, bdm_ref, trit_ref,
                 oc_ref, ol_ref,
                 vs_ref, gs_ref, bvs_ref, dir_ref, yf_ref, yb_ref):
    tc, tl = rc_ref.shape[0], rl_ref.shape[0]
    tall = tc + tl
    n_ctx, n_all = tc // CH, tall // CH
    grp = _group_size(n_ctx)
    m0, m1 = hm_ref[0], hm_ref[1]
    m0w = jnp.concatenate([m0, m0], axis=1)
    m1w = jnp.concatenate([m1, m1], axis=1)
    bdm = bdm_ref[...]

    kkw, kaw, rkw = vec_ref[0:1, :], vec_ref[1:2, :], vec_ref[2:3, :]
    for (_, uo, n) in _row_tiles(tc, tl):
        ctx_tile = uo < tc
        lo = uo if ctx_tile else uo - tc
        t_len = tc if ctx_tile else tl
        rowi = lax.broadcasted_iota(jnp.int32, (n, LANES), 0)

        def lerp(src_c, src_l, mu):
            src = src_c if ctx_tile else src_l
            x = src[lo:lo + n, :]
            prev = src[lo - 1:lo, :] if lo > 0 else jnp.zeros((1, LANES), F32)
            nxt = src[lo + n:lo + n + 1, :] if lo + n < t_len else jnp.zeros((1, LANES), F32)
            xm = jnp.where(rowi == 0, prev, pltpu.roll(x, 1, 0))
            xp = jnp.where(rowi == n - 1, nxt, pltpu.roll(x, n - 1, 0))
            return x + mu * (0.5 * (xm + xp) - x)
        r = lerp(rc_ref, rl_ref, mur_ref[...])
        k = lerp(kc_ref, kl_ref, muk_ref[...])
        v = lerp(vc_ref, vl_ref, muv_ref[...])
        wd = lerp(wdc_ref, wdl_ref, mux_ref[0:1, :])
        ad = lerp(adc_ref, adl_ref, mux_ref[1:2, :])
        gd = lerp(gdc_ref, gdl_ref, mux_ref[2:3, :])
        tw = jnp.tanh(wd)
        kkv = k * kkw
        kk = kkv * lax.rsqrt(_mm_sel_r(kkv * kkv, bdm) + 1e-6)
        ksum = jnp.zeros_like(k)
        for d in range(2):
            logw = -_softplus(-(vec_ref[3 + d:4 + d, :] + _mm(tw, wup_ref[d]))) - 0.5
            lw = -jnp.exp(logw)
            a = _sigmoid(vec_ref[5 + d:6 + d, :] + _mm(ad, aup_ref[d]))
            kdir = k * (1.0 + (a - 1.0) * kaw)
            ksum = ksum + kdir
            cum = _mm_sel_l(trit_ref[d, 0:n, 0:n], lw)
            einv = jnp.exp(-cum)
            dir_ref[d, 0, uo:uo + n, :] = cum
            dir_ref[d, 1, uo:uo + n, :] = kk * jnp.exp(cum - lw)
            dir_ref[d, 2, uo:uo + n, :] = r * jnp.exp(cum)
            dir_ref[d, 3, uo:uo + n, :] = kk * a * einv
            dir_ref[d, 4, uo:uo + n, :] = kdir * einv
        bonus = _mm_sel_r(r * ksum * rkw, bdm)
        vs_ref[uo:uo + n, :] = v
        gs_ref[uo:uo + n, :] = _mm(_sigmoid(gd), gup_ref[...])
        bvs_ref[uo:uo + n, :] = bonus * v

    def group_body(gi, carry):
        probs = []
        for i in range(grp):
            for d in range(2):
                c = _chunk_order(gi * grp + i, n_ctx, n_all, d)
                rows = pl.ds(pl.multiple_of(c * CH, CH), CH)
                v = vs_ref[rows, :]
                cum, kkq, rq, binv, kinv = (dir_ref[d, j, rows, :] for j in range(5))
                lhs = jnp.concatenate([kkq, rq], axis=0)
                rhs = jnp.concatenate([_bd(binv, m0, m1), _bd(kinv, m0, m1)], axis=0)
                m = _mm_nt(lhs, rhs)
                vbd = _bd(v, m0, m1)
                probs.append(dict(d=d, rows=rows, v=v, cum=cum, kkq=kkq, rq=rq, binv=binv, kinv=kinv,
                                  a_b=m[:CH, :LANES] * strict_ref[d],
                                  akv=_mm(m[:CH, LANES:] * strict_ref[d], vbd),
                                  a_rb=m[CH:, :LANES] * incl_ref[d],
                                  y0=_mm(m[CH:, LANES:] * incl_ref[d], vbd)))
        es = _inv_levels([p["a_b"] for p in probs], [p["d"] for p in probs], lvl_ref, m0, m1)
        for p, e in zip(probs, es):
            rhs = jnp.concatenate([p["kkq"], p["akv"]], axis=1)
            sol = rhs + _mm(e, _bd(rhs, m0w, m1w))
            last = CH - 1 if p["d"] == 0 else 0
            ctot = p["cum"][last:last + 1, :]
            etot = jnp.exp(ctot)
            p.update(wt=sol[:, :LANES], u0=-sol[:, LANES:], etot=etot,
                     dec=jnp.concatenate([p["binv"] * etot, p["kinv"] * etot], axis=0))
        st = list(carry)
        for p in probs:
            d = p["d"]
            ws = _mm_nt(jnp.concatenate([p["wt"], p["rq"]], axis=0), st[d])
            u = p["u0"] - ws[:CH]
            y = ws[CH:] + p["y0"] + _mm(p["a_rb"], _bd(u, m0, m1))
            upd = _mm_tn(jnp.concatenate([u, p["v"]], axis=0), p["dec"])
            st[d] = st[d] * p["etot"] + upd * bdm
            yref = yf_ref if d == 0 else yb_ref
            yref[p["rows"], :] = y
        return tuple(st)
    z = jnp.zeros((LANES, LANES), F32)
    lax.fori_loop(0, n_all // grp, group_body, (z, z))

    gnw, gnb = vec_ref[7:8, :], vec_ref[8:9, :]
    for (po, uo, n) in _row_tiles(tc, tl):
        y = yf_ref[uo:uo + n, :] + yb_ref[uo:uo + n, :]
        mean = _mm_sel_r(y, bdm) * (1.0 / HD)
        yc = y - mean
        var = _mm_sel_r(yc * yc, bdm) * (1.0 / HD)
        out = (yc * lax.rsqrt(var + RWKV_GN_EPS) * gnw + gnb + bvs_ref[uo:uo + n, :]) * gs_ref[uo:uo + n, :]
        if uo < tc:
            oc_ref[uo:uo + n, :] = out
        else:
            ol_ref[uo - tc:uo - tc + n, :] = out


def _rwkv_mixer(rw_c, rw_l, mu, w0, w_up, a0, a_up, g_up, k_k, k_a, r_k, gn_w, gn_b, consts):
    bsz, tc, _ = rw_c.shape
    tl = rw_l.shape[1]
    tall = tc + tl
    n_all = tall // CH
    W = RWKV_WIDTH
    nb = W // LANES
    wup = jnp.zeros((2, LANES, W), F32)
    aup = jnp.zeros((2, LANES, W), F32)
    for d in range(2):
        wup = wup.at[d, d * RWKV_RANK:(d + 1) * RWKV_RANK, :].set(w_up[d])
        aup = aup.at[d, d * RWKV_RANK:(d + 1) * RWKV_RANK, :].set(a_up[d])
    vec = jnp.zeros((16, W), F32)
    vec = vec.at[0].set(k_k).at[1].set(k_a).at[2].set(r_k.reshape(-1))
    vec = vec.at[3].set(w0[0]).at[4].set(w0[1]).at[5].set(a0[0]).at[6].set(a0[1])
    vec = vec.at[7].set(gn_w).at[8].set(gn_b)
    mu_rkv = mu[:3 * W].reshape(3, 1, W)
    mux = jnp.zeros((SUBLANES, LANES), F32).at[0:3, :].set(mu[3 * W:].reshape(3, LANES))
    cm = lambda *blk: pl.BlockSpec(blk, lambda b, p: (0,) * len(blk))

    def tok(t, col0, per_pair=True):
        if per_pair:
            return pl.BlockSpec((None, t, LANES), lambda b, p: (b, 0, col0 + p))
        return pl.BlockSpec((None, t, LANES), lambda b, p: (b, 0, col0))
    stream = lambda t: [tok(t, 0), tok(t, nb), tok(t, 2 * nb), tok(t, 3 * nb, False),
                        tok(t, 3 * nb + 1, False), tok(t, 3 * nb + 2, False)]
    in_specs = stream(tc) + stream(tl) + [
        pl.BlockSpec((None, 1, LANES), lambda b, p: (0, 0, p)),
        pl.BlockSpec((None, 1, LANES), lambda b, p: (1, 0, p)),
        pl.BlockSpec((None, 1, LANES), lambda b, p: (2, 0, p)),
        cm(SUBLANES, LANES),
        pl.BlockSpec((2, LANES, LANES), lambda b, p: (0, 0, p)),
        pl.BlockSpec((2, LANES, LANES), lambda b, p: (0, 0, p)),
        pl.BlockSpec((RWKV_G_RANK, LANES), lambda b, p: (0, p)),
        pl.BlockSpec((16, LANES), lambda b, p: (0, p)),
        cm(2, CH, LANES), cm(2, CH, LANES), cm(2, 6, CH, LANES), cm(2, 1, LANES), cm(LANES, LANES),
        cm(2, ROWT, ROWT)]
    scratch = [pltpu.VMEM((tall, LANES), F32)] * 3 + [
        pltpu.VMEM((2, 5, tall, LANES), F32),
        pltpu.VMEM((tall, LANES), F32), pltpu.VMEM((tall, LANES), F32)]
    return pl.pallas_call(
        _rwkv_kernel,
        grid=(bsz, N_PAIRS),
        in_specs=in_specs,
        out_specs=[pl.BlockSpec((None, tc, LANES), lambda b, p: (b, 0, p)),
                   pl.BlockSpec((None, tl, LANES), lambda b, p: (b, 0, p))],
        out_shape=[jax.ShapeDtypeStruct((bsz, tc, W), F32),
                   jax.ShapeDtypeStruct((bsz, tl, W), F32)],
        scratch_shapes=scratch,
        compiler_params=pltpu.CompilerParams(dimension_semantics=("arbitrary", "arbitrary"),
                                             vmem_limit_bytes=VMEM_LIMIT),
        name="rwkv_mixer",
    )(*([rw_c] * 6), *([rw_l] * 6), mu_rkv, mu_rkv, mu_rkv, mux, wup, aup, g_up, vec,
      consts["incl"], consts["strict"], consts["lvl"], consts["hm"], consts["bdm"], consts["tri_t"])


def _lru_kernel(xc_ref, gc_ref, xl_ref, gl_ref, cw_ref, vec_ref, wcat_ref, bcat_ref,
                oc_ref, ol_ref, xp_ref, xv_ref, ab_ref, hf_ref, hb_ref):
    tc, tl = xc_ref.shape[0], xl_ref.shape[0]
    tall = tc + tl
    rows = tl // GRID_W
    W = xc_ref.shape[1]
    z = jnp.zeros((GAP, W), F32)
    xp_ref[0:GAP, :] = z
    xp_ref[GAP:GAP + tc, :] = xc_ref[...]
    xp_ref[GAP + tc:2 * GAP + tc, :] = z
    base = 2 * GAP + tc
    for c in range(GRID_W):
        xp_ref[base + c * rows:base + (c + 1) * rows, :] = xl_ref[pl.ds(c, rows, stride=GRID_W), :]
    xp_ref[base + tl:base + tl + GAP, :] = z
    cb = vec_ref[0:1, :]
    for (po, uo, n) in _row_tiles(tc, tl):
        acc = cb + cw_ref[0:1, :] * xp_ref[po - 2:po - 2 + n, :]
        for tap in range(1, CONV_W):
            acc = acc + cw_ref[tap:tap + 1, :] * xp_ref[po - 2 + tap:po - 2 + tap + n, :]
        xv_ref[uo:uo + n, :] = acc
        gates = _mm(acc, wcat_ref[...]) + bcat_ref[...]
        for d in range(2):
            rg = _sigmoid(gates[:, (2 * d) * W:(2 * d + 1) * W])
            ig = _sigmoid(gates[:, (2 * d + 1) * W:(2 * d + 2) * W])
            log_a = -LRU_C * rg * _softplus(-vec_ref[1 + d:2 + d, :])
            a = jnp.exp(log_a)
            mult = jnp.sqrt(-jnp.tanh(log_a) * (1.0 + a * a))
            ab_ref[d, 0, uo:uo + n, :] = a
            ab_ref[d, 1, uo:uo + n, :] = mult * (ig * acc)

    sub = lax.broadcasted_iota(jnp.int32, (SUBLANES, W), 0)
    n_tiles_c, n_tiles = tc // SUBLANES, tall // SUBLANES

    def tile_scan(a, b, d):
        for sh in (1, 2, 4):
            if d == 0:
                ok = sub >= sh
                a_s = jnp.where(ok, pltpu.roll(a, sh, 0), 1.0)
                b_s = jnp.where(ok, pltpu.roll(b, sh, 0), 0.0)
            else:
                ok = sub < SUBLANES - sh
                a_s = jnp.where(ok, pltpu.roll(a, SUBLANES - sh, 0), 1.0)
                b_s = jnp.where(ok, pltpu.roll(b, SUBLANES - sh, 0), 0.0)
            b = b + a * b_s
            a = a * a_s
        return a, b

    def scan_body(s, carry):
        hf, hb = carry
        r0 = pl.multiple_of(s * SUBLANES, SUBLANES)
        a, b = tile_scan(ab_ref[0, 0, pl.ds(r0, SUBLANES), :], ab_ref[0, 1, pl.ds(r0, SUBLANES), :], 0)
        h = b + a * hf
        hf_ref[pl.ds(r0, SUBLANES), :] = h
        hf = jnp.broadcast_to(h[SUBLANES - 1:SUBLANES, :], (SUBLANES, W))
        tb = jnp.where(s < n_tiles_c, n_tiles_c - 1 - s, n_tiles + n_tiles_c - 1 - s)
        r1 = pl.multiple_of(tb * SUBLANES, SUBLANES)
        a, b = tile_scan(ab_ref[1, 0, pl.ds(r1, SUBLANES), :], ab_ref[1, 1, pl.ds(r1, SUBLANES), :], 1)
        h = b + a * hb
        hb_ref[pl.ds(r1, SUBLANES), :] = h
        hb = jnp.broadcast_to(h[0:1, :], (SUBLANES, W))
        return hf, hb
    z8 = jnp.zeros((SUBLANES, W), F32)
    lax.fori_loop(0, n_tiles, scan_body, (z8, z8))

    oc_ref[...] = (hf_ref[0:tc, :] + hb_ref[0:tc, :]) * _gelu_tanh(gc_ref[...])
    for c in range(GRID_W):
        h = hf_ref[tc + c * rows:tc + (c + 1) * rows, :] + hb_ref[tc + c * rows:tc + (c + 1) * rows, :]
        ol_ref[pl.ds(c, rows, stride=GRID_W), :] = h * _gelu_tanh(gl_ref[pl.ds(c, rows, stride=GRID_W), :])


def _lru_mixer(x_c, g_c, x_l, g_l, conv_w, conv_b, w_a, b_a, w_x, b_x, lam):
    bsz, tc, W = x_c.shape
    tl = x_l.shape[1]
    tall = tc + tl
    nh = W // LANES
    bph = LANES // LRU_BW
    wcat = jnp.zeros((nh, LANES, 4 * LANES), F32)
    bcat = jnp.zeros((nh, 1, 4 * LANES), F32)
    for hh in range(nh):
        for d in range(2):
            for gi, (wsrc, bsrc) in enumerate(((w_a, b_a), (w_x, b_x))):
                c0 = (2 * d + gi) * LANES
                bcat = bcat.at[hh, 0, c0:c0 + LANES].set(bsrc[d, hh * LANES:(hh + 1) * LANES])
                for n in range(bph):
                    wcat = wcat.at[hh, n * LRU_BW:(n + 1) * LRU_BW,
                                   c0 + n * LRU_BW:c0 + (n + 1) * LRU_BW].set(wsrc[d, hh * bph + n])
    vec = jnp.zeros((SUBLANES, W), F32).at[0].set(conv_b).at[1].set(lam[0]).at[2].set(lam[1])
    tok = lambda t: pl.BlockSpec((None, t, LANES), lambda b, h: (b, 0, h))
    return pl.pallas_call(
        _lru_kernel,
        grid=(bsz, nh),
        in_specs=[tok(tc), tok(tc), tok(tl), tok(tl),
                  pl.BlockSpec((CONV_W, LANES), lambda b, h: (0, h)),
                  pl.BlockSpec((SUBLANES, LANES), lambda b, h: (0, h)),
                  pl.BlockSpec((None, LANES, 4 * LANES), lambda b, h: (h, 0, 0)),
                  pl.BlockSpec((None, 1, 4 * LANES), lambda b, h: (h, 0, 0))],
        out_specs=[tok(tc), tok(tl)],
        out_shape=[jax.ShapeDtypeStruct((bsz, tc, W), F32), jax.ShapeDtypeStruct((bsz, tl, W), F32)],
        scratch_shapes=[pltpu.VMEM((tall + 3 * GAP, LANES), F32), pltpu.VMEM((tall, LANES), F32),
                        pltpu.VMEM((2, 2, tall, LANES), F32),
                        pltpu.VMEM((tall, LANES), F32), pltpu.VMEM((tall, LANES), F32)],
        compiler_params=pltpu.CompilerParams(dimension_semantics=("arbitrary", "arbitrary"),
                                             vmem_limit_bytes=VMEM_LIMIT),
        name="lru_mixer",
    )(x_c, g_c, x_l, g_l, conv_w, vec, wcat.astype(BF16), bcat)


def _finish_kernel(x_ref, gdn_ref, lru_ref, rwk_ref, m2_ref, m3_ref, m4_ref, m5_ref, nrm_ref,
                   wo_ref, up_ref, dn_ref, o_ref):
    x = x_ref[...]
    o = (jnp.dot(gdn_ref[...].astype(BF16), wo_ref[0:GDN_WIDTH, :], preferred_element_type=F32)
         + jnp.dot(lru_ref[...].astype(BF16), wo_ref[GDN_WIDTH:GDN_WIDTH + LRU_WIDTH, :],
                   preferred_element_type=F32)
         + jnp.dot(rwk_ref[...].astype(BF16), wo_ref[GDN_WIDTH + LRU_WIDTH:, :], preferred_element_type=F32))
    x = x + m2_ref[...] * _rms(o, nrm_ref[0:1, :])
    h = (_rms(x, nrm_ref[1:2, :]) * (1.0 + m4_ref[...]) + m3_ref[...]).astype(BF16)
    f = jnp.zeros_like(x)
    fc = 1024
    for j in range(D_FF // fc):
        a = jnp.maximum(jnp.dot(h, up_ref[:, j * fc:(j + 1) * fc], preferred_element_type=F32), 0.0)
        f = f + jnp.dot((a * a).astype(BF16), dn_ref[j * fc:(j + 1) * fc, :], preferred_element_type=F32)
    o_ref[...] = x + m5_ref[...] * _rms(f, nrm_ref[2:3, :])


def _finish(x2, gdn, lru, rwk, mod_rows, rows_per_mod, norms, wo_bf, up_bf, dn_bf):
    n = x2.shape[0]
    tm = 256
    tiles_per_mod = rows_per_mod // tm
    modspec = lambda k: pl.BlockSpec((None, 1, D_MODEL), lambda i: (6 * (i // tiles_per_mod) + k, 0, 0))
    cm = lambda *blk: pl.BlockSpec(blk, lambda i: (0,) * len(blk))
    return pl.pallas_call(
        _finish_kernel,
        grid=(n // tm,),
        in_specs=[pl.BlockSpec((tm, D_MODEL), lambda i: (i, 0)),
                  pl.BlockSpec((tm, GDN_WIDTH), lambda i: (i, 0)),
                  pl.BlockSpec((tm, LRU_WIDTH), lambda i: (i, 0)),
                  pl.BlockSpec((tm, RWKV_WIDTH), lambda i: (i, 0)),
                  modspec(2), modspec(3), modspec(4), modspec(5),
                  cm(SUBLANES, D_MODEL), cm(D_MODEL, D_MODEL), cm(D_MODEL, D_FF), cm(D_FF, D_MODEL)],
        out_specs=pl.BlockSpec((tm, D_MODEL), lambda i: (i, 0)),
        out_shape=jax.ShapeDtypeStruct((n, D_MODEL), F32),
        compiler_params=pltpu.CompilerParams(dimension_semantics=("arbitrary",),
                                             vmem_limit_bytes=VMEM_LIMIT),
        name="finish",
    )(x2, gdn, lru, rwk, mod_rows, mod_rows, mod_rows, mod_rows, norms, wo_bf, up_bf, dn_bf)


def _arrange_w_in(w):
    s = np.cumsum([0, 3 * GDN_WIDTH, GDN_WIDTH, 2 * GDN_HEADS, 2 * GDN_HEADS, LRU_WIDTH, LRU_WIDTH, RWKV_IN])
    pad = jnp.zeros((w.shape[0], BA_W - 4 * GDN_HEADS), w.dtype)
    return jnp.concatenate([w[:, s[0]:s[2]], w[:, s[2]:s[4]], pad, w[:, s[4]:]], axis=1)


def kernel(x, c, ctx, c_ctx, ada_w, ada_b, norm_mix_pre, norm_mix_post, norm_ffn_pre, norm_ffn_post, w_in, gdn_conv, gdn_a_log, gdn_dt_bias, gdn_norm, lru_conv, lru_conv_b, lru_wa, lru_ba, lru_wx, lru_bx, lru_lambda, rwkv_mu, rwkv_w0, rwkv_w_up, rwkv_a0, rwkv_a_up, rwkv_g_up, rwkv_k_k, rwkv_k_a, rwkv_r_k, rwkv_gn_w, rwkv_gn_b, w_out, ffn_up, ffn_down):
    bsz, tl, _ = x.shape
    tc = ctx.shape[1]
    depth = w_in.shape[0]
    consts = _device_consts()

    cvec = jnp.zeros((16, D_MODEL), F32).at[0:bsz].set(c).at[bsz].set(c_ctx)
    mods = _ada_mod(cvec, ada_w, ada_b).reshape(depth, 16, 6, D_MODEL)

    xl = x.reshape(bsz * tl, D_MODEL)
    xc = ctx.reshape(bsz * tc, D_MODEL)
    for i in range(depth):
        mod_l = mods[i, 0:bsz].reshape(bsz * 6, 1, D_MODEL)
        mod_c = mods[i, bsz:bsz + 1].reshape(6, 1, D_MODEL)
        w_bf = _arrange_w_in(w_in[i]).astype(BF16)
        pl_ = _inproj(xl, mod_l, tl, norm_mix_pre[i], w_bf)
        pc_ = _inproj(xc, mod_c, bsz * tc, norm_mix_pre[i], w_bf)
        r3 = lambda a, t: a.reshape(bsz, t, a.shape[-1])
        qkv_l, z_l, ba_l, lx_l, lg_l, rw_l = (r3(a, tl) for a in pl_)
        qkv_c, z_c, ba_c, lx_c, lg_c, rw_c = (r3(a, tc) for a in pc_)

        gdn_c, gdn_l = _gdn_mixer(qkv_c, z_c, ba_c, qkv_l, z_l, ba_l, gdn_conv[i], gdn_a_log[i],
                                  gdn_dt_bias[i], gdn_norm[i], consts)
        lru_c, lru_l = _lru_mixer(lx_c, lg_c, lx_l, lg_l, lru_conv[i], lru_conv_b[i], lru_wa[i], lru_ba[i],
                                  lru_wx[i], lru_bx[i], lru_lambda[i])
        rwk_c, rwk_l = _rwkv_mixer(rw_c, rw_l, rwkv_mu[i], rwkv_w0[i], rwkv_w_up[i], rwkv_a0[i],
                                   rwkv_a_up[i], rwkv_g_up[i], rwkv_k_k[i], rwkv_k_a[i], rwkv_r_k[i],
                                   rwkv_gn_w[i], rwkv_gn_b[i], consts)

        norms = jnp.zeros((SUBLANES, D_MODEL), F32).at[0].set(norm_mix_post[i]).at[1].set(
            norm_ffn_pre[i]).at[2].set(norm_ffn_post[i])
        wo_bf, up_bf, dn_bf = w_out[i].astype(BF16), ffn_up[i].astype(BF16), ffn_down[i].astype(BF16)
        f2 = lambda a: a.reshape(-1, a.shape[-1])
        xl = _finish(xl, f2(gdn_l), f2(lru_l), f2(rwk_l), mod_l, tl, norms, wo_bf, up_bf, dn_bf)
        if i < depth - 1:
            xc = _finish(xc, f2(gdn_c), f2(lru_c), f2(rwk_c), mod_c, bsz * tc, norms, wo_bf, up_bf, dn_bf)
    return xl.reshape(bsz, tl, D_MODEL)
```

```python
import functools

import numpy as np
import jax
import jax.numpy as jnp
from jax import lax
from jax.experimental import pallas as pl
from jax.experimental.pallas import tpu as pltpu

F32 = jnp.float32
BF16 = jnp.bfloat16

LANES = 128
SUBLANES = 8
VMEM_LIMIT = 56 * 1024 * 1024

D_MODEL = 1024
DEPTH = 2
GRID_W = 64
CONV_W = 4
EPS = 1e-6
D_FF = 4 * D_MODEL
HD = 64
GDN_WIDTH = 3 * D_MODEL // 8
GDN_HEADS = GDN_WIDTH // HD
LRU_WIDTH = D_MODEL // 4
LRU_BLOCKS = 4
LRU_BW = LRU_WIDTH // LRU_BLOCKS
LRU_C = 8.0
RWKV_WIDTH = D_MODEL - GDN_WIDTH - LRU_WIDTH
RWKV_HEADS = RWKV_WIDTH // HD
RWKV_RANK = 64
RWKV_G_RANK = 128
RWKV_GN_EPS = 6.4e-4
RWKV_IN = 3 * RWKV_WIDTH + 2 * RWKV_RANK + 2 * RWKV_RANK + RWKV_G_RANK
N_PAIRS = GDN_HEADS // 2
BA_W = LANES
P_OFF = np.cumsum([0, 3 * GDN_WIDTH, GDN_WIDTH, BA_W, LRU_WIDTH, LRU_WIDTH, RWKV_IN])
D_INP = int(P_OFF[-1])

CH = 64
ROWT = 256
GAP = SUBLANES


def _mm(a, b):
    return jnp.dot(a.astype(BF16), b.astype(BF16), preferred_element_type=F32)


def _mm_nt(a, b):
    return lax.dot_general(a.astype(BF16), b.astype(BF16), (((1,), (1,)), ((), ())),
                           preferred_element_type=F32)


def _mm_tn(a, b):
    return lax.dot_general(a.astype(BF16), b.astype(BF16), (((0,), (0,)), ((), ())),
                           preferred_element_type=F32)


def _split2(x):
    hi = x.astype(BF16)
    lo = (x - hi.astype(F32)).astype(BF16)
    return hi, lo


def _mm_sel_l(m01, x):
    mb = m01.astype(BF16)
    h, l = _split2(x)
    d = functools.partial(jnp.dot, preferred_element_type=F32)
    return d(mb, h) + d(mb, l)


def _mm_sel_r(x, m01):
    mb = m01.astype(BF16)
    h, l = _split2(x)
    d = functools.partial(jnp.dot, preferred_element_type=F32)
    return d(h, mb) + d(l, mb)


def _sigmoid(x):
    return 0.5 * jnp.tanh(0.5 * x) + 0.5


def _silu(x):
    return x * _sigmoid(x)


def _softplus(x):
    return jnp.maximum(x, 0.0) + jnp.log(1.0 + jnp.exp(-jnp.abs(x)))


def _gelu_tanh(x):
    return 0.5 * x * (1.0 + jnp.tanh(0.7978845608028654 * (x + 0.044715 * (x * x * x))))


def _bd(y, m0, m1):
    return jnp.concatenate([y * m0, y * m1], axis=0)


def _pair_consts():
    i = np.arange(CH)[:, None]
    j = (np.arange(LANES) % HD)[None, :]
    incl = np.stack([i >= j, i <= j]).astype(np.float32)
    strict = np.stack([i > j, i < j]).astype(np.float32)
    lvls = []
    for d in range(2):
        per = []
        for m in (1, 2, 4, 8, 16, 32):
            same = (i // (2 * m)) == (j // (2 * m))
            lo_i, lo_j = (i % (2 * m)) < m, (j % (2 * m)) < m
            off = same & (~lo_i) & lo_j if d == 0 else same & lo_i & (~lo_j)
            per.append(off)
        lvls.append(np.stack(per))
    lvl = np.stack(lvls).astype(np.float32)
    lane = np.arange(LANES)
    hm = np.stack([lane < HD, lane >= HD]).astype(np.float32)[:, None, :]
    bdm = ((np.arange(LANES)[:, None] // HD) == (lane[None, :] // HD)).astype(np.float32)
    r = np.arange(ROWT)[:, None]
    c = np.arange(ROWT)[None, :]
    same = (r // CH) == (c // CH)
    tri_t = np.stack([same & (r >= c), same & (r <= c)]).astype(np.float32)
    ones_t = same.astype(np.float32)
    return dict(incl=incl, strict=strict, lvl=lvl, hm=hm, bdm=bdm, tri_t=tri_t, ones_t=ones_t)


def _device_consts():
    out = {k: jnp.asarray(v) for k, v in _pair_consts().items()}
    for k in ("tri_t", "ones_t"):
        out[k] = out[k].astype(BF16)
    return out


def _inv_levels(mats, dirs, lvl_ref, m0, m1):
    es = [-(a * lvl_ref[d, 0]) for a, d in zip(mats, dirs)]
    for k in range(1, 6):
        offs = [a * lvl_ref[d, k] for a, d in zip(mats, dirs)]
        xs = [off + _mm(e, _bd(off, m0, m1)) for e, off in zip(es, offs)]
        es = [e - x - _mm(x, _bd(e, m0, m1)) for e, x in zip(es, xs)]
    return es


def _round_robin(gens):
    gens = list(gens)
    while gens:
        alive = []
        for g in gens:
            try:
                next(g)
                alive.append(g)
            except StopIteration:
                pass
        gens = alive


def _group_size(n_all):
    for g in (6, 4, 3, 2):
        if n_all % g == 0:
            return g
    return 1


def _chunk_order(s, n_ctx, n_all, d):
    if d == 0:
        return s
    return jnp.where(s < n_ctx, n_ctx - 1 - s, n_all + n_ctx - 1 - s)


def _fill_padded(dst_ref, src_c_ref, src_l_ref, tc, tl):
    w = dst_ref.shape[1]
    z = jnp.zeros((GAP, w), F32)
    dst_ref[0:GAP, :] = z
    dst_ref[GAP:GAP + tc, :] = src_c_ref[...]
    dst_ref[GAP + tc:2 * GAP + tc, :] = z
    dst_ref[2 * GAP + tc:2 * GAP + tc + tl, :] = src_l_ref[...]
    dst_ref[2 * GAP + tc + tl:3 * GAP + tc + tl, :] = z


def _row_tiles(tc, tl):
    out = []
    for base_p, base_u, n in ((GAP, 0, tc), (2 * GAP + tc, tc, tl)):
        for t0 in range(0, n, ROWT):
            out.append((base_p + t0, base_u + t0, min(ROWT, n - t0)))
    return out


def _ada_kernel(c_ref, w_ref, b_ref, o_ref):
    c = c_ref[...]
    o_ref[...] = _mm(_silu(c), w_ref[...]) + b_ref[...]


def _ada_mod(cvec, ada_w, ada_b):
    L = ada_w.shape[0]
    n = ada_w.shape[2]
    tn = 1536
    return pl.pallas_call(
        _ada_kernel,
        grid=(L, n // tn),
        in_specs=[pl.BlockSpec((16, D_MODEL), lambda l, j: (0, 0)),
                  pl.BlockSpec((None, D_MODEL, tn), lambda l, j: (l, 0, j)),
                  pl.BlockSpec((None, 1, tn), lambda l, j: (l, 0, j))],
        out_specs=pl.BlockSpec((None, 16, tn), lambda l, j: (l, 0, j)),
        out_shape=jax.ShapeDtypeStruct((L, 16, n), F32),
        compiler_params=pltpu.CompilerParams(dimension_semantics=("arbitrary", "arbitrary"),
                                             vmem_limit_bytes=VMEM_LIMIT),
        name="ada_mod",
    )(cvec, ada_w, ada_b.reshape(L, 1, n))


def _rms(x, g):
    return x * lax.rsqrt(jnp.mean(x * x, axis=-1, keepdims=True) + EPS) * g


def _inproj_kernel(x_ref, sh_ref, sc_ref, g_ref, w_ref, qkv_ref, z_ref, ba_ref, lx_ref, lg_ref, rw_ref):
    h = _rms(x_ref[...], g_ref[...]) * (1.0 + sc_ref[...]) + sh_ref[...]
    p = jnp.dot(h.astype(BF16), w_ref[...], preferred_element_type=F32)
    for ref, k in zip((qkv_ref, z_ref, ba_ref, lx_ref, lg_ref, rw_ref), range(6)):
        ref[...] = p[:, int(P_OFF[k]):int(P_OFF[k + 1])]


def _inproj(x2, mod_rows, rows_per_mod, g, w_bf):
    n = x2.shape[0]
    tm = 256
    tiles_per_mod = rows_per_mod // tm
    widths = [int(P_OFF[k + 1] - P_OFF[k]) for k in range(6)]
    return pl.pallas_call(
        _inproj_kernel,
        grid=(n // tm,),
        in_specs=[pl.BlockSpec((tm, D_MODEL), lambda i: (i, 0)),
                  pl.BlockSpec((None, 1, D_MODEL), lambda i: (6 * (i // tiles_per_mod), 0, 0)),
                  pl.BlockSpec((None, 1, D_MODEL), lambda i: (6 * (i // tiles_per_mod) + 1, 0, 0)),
                  pl.BlockSpec((1, D_MODEL), lambda i: (0, 0)),
                  pl.BlockSpec((D_MODEL, D_INP), lambda i: (0, 0))],
        out_specs=[pl.BlockSpec((tm, w), lambda i: (i, 0)) for w in widths],
        out_shape=[jax.ShapeDtypeStruct((n, w), F32) for w in widths],
        compiler_params=pltpu.CompilerParams(dimension_semantics=("arbitrary",),
                                             vmem_limit_bytes=VMEM_LIMIT),
        name="inproj",
    )(x2, mod_rows, mod_rows, g.reshape(1, D_MODEL), w_bf)


def _gdn_kernel(qc_ref, kc_ref, vc_ref, zc_ref, bac_ref, ql_ref, kl_ref, vl_ref, zl_ref, bal_ref,
                cwq_ref, cwk_ref, cwv_ref, aux_ref, e_ref,
                incl_ref, strict_ref, lvl_ref, hm_ref, bdm_ref, trit_ref, onest_ref,
                oc_ref, ol_ref,
                qp_ref, kp_ref, vp_ref, qs_ref, ks_ref, vs_ref, dir_ref, ls_ref, p1_ref, of_ref, ob_ref):
    tc, tl = qc_ref.shape[0], ql_ref.shape[0]
    tall = tc + tl
    n_ctx, n_all = tc // CH, tall // CH
    grp = _group_size(n_all)
    m0, m1 = hm_ref[0], hm_ref[1]
    m0w = jnp.concatenate([m0, m0], axis=1)
    m1w = jnp.concatenate([m1, m1], axis=1)
    bdm = bdm_ref[...]

    _fill_padded(qp_ref, qc_ref, ql_ref, tc, tl)
    _fill_padded(kp_ref, kc_ref, kl_ref, tc, tl)
    _fill_padded(vp_ref, vc_ref, vl_ref, tc, tl)
    lane = lax.broadcasted_iota(jnp.int32, (1, LANES), 1)
    alog, dtb = aux_ref[0:1, :], aux_ref[1:2, :]
    for (po, uo, n) in _row_tiles(tc, tl):
        def conv(src, cw):
            acc = cw[0:1, :] * src[po - 2:po - 2 + n, :]
            for tap in range(1, CONV_W):
                acc = acc + cw[tap:tap + 1, :] * src[po - 2 + tap:po - 2 + tap + n, :]
            return _silu(acc)
        q = conv(qp_ref, cwq_ref)
        k = conv(kp_ref, cwk_ref)
        v = conv(vp_ref, cwv_ref)
        q = q * lax.rsqrt(_mm(q * q, bdm) + 1e-6) * (HD ** -0.5)
        k = k * lax.rsqrt(_mm(k * k, bdm) + 1e-6)
        qs_ref[uo:uo + n, :] = q
        ks_ref[uo:uo + n, :] = k
        vs_ref[uo:uo + n, :] = v
        ba = bac_ref[uo:uo + n, :] if uo < tc else bal_ref[uo - tc:uo - tc + n, :]
        beta = _sigmoid(ba)
        gval = -jnp.exp(alog) * _softplus(ba + dtb)
        bgv = jnp.where(lane < 2 * GDN_HEADS, beta, gval)
        ex = _mm_sel_r(bgv, e_ref[...])
        incl_t = [jnp.concatenate([incl_ref[dd]] * (n // CH), axis=0) for dd in range(2)]
        for d in range(2):
            g = ex[:, (2 + d) * LANES:(3 + d) * LANES]
            gc = _mm_sel_l(trit_ref[d, 0:n, 0:n], g)
            gr = _mm_sel_l(onest_ref[0:n, 0:n], g * incl_t[1 - d])
            dir_ref[d, 0, uo:uo + n, :] = ex[:, d * LANES:(d + 1) * LANES]
            dir_ref[d, 1, uo:uo + n, :] = gc
            dir_ref[d, 2, uo:uo + n, :] = jnp.exp(jnp.minimum(gc - gr, 0.0)) * incl_t[d]

    n_grp = n_all // grp
    zp = jnp.zeros(p1_ref.shape[1:], F32)
    p1_ref[0] = zp
    zs = jnp.zeros(ls_ref.shape[1:], F32)
    ls_ref[1] = zs

    def chunk_rows(gi, i, d):
        g = jnp.clip(gi, 0, n_grp - 1)
        c = _chunk_order(g * grp + i, n_ctx, n_all, d)
        return pl.ds(pl.multiple_of(c * CH, CH), CH)

    pd = [(i, d) for i in range(grp) for d in range(2)]

    def stage_a(gi, slot):
        for i, d in pd:
            rows = chunk_rows(gi, i, d)
            q, k = qs_ref[rows, :], ks_ref[rows, :]
            beta, dec = dir_ref[d, 0, rows, :], dir_ref[d, 2, rows, :]
            la = _mm_nt(jnp.concatenate([k, q], axis=0), _bd(k, m0, m1))
            ls_ref[slot, i, d, 0:CH, :] = la[:CH] * dec * strict_ref[d] * beta
            ls_ref[slot, i, d, CH:2 * CH, :] = la[CH:] * dec
            yield

    def stage_b(gi, slot, validf):
        mats = [ls_ref[slot, i, d, 0:CH, :] for i, d in pd]
        es = [-(a * lvl_ref[d, 0]) for a, (_, d) in zip(mats, pd)]
        for lv in range(1, 6):
            offs = [a * lvl_ref[d, lv] for a, (_, d) in zip(mats, pd)]
            xs = [off + _mm(e, _bd(off, m0, m1)) for e, off in zip(es, offs)]
            yield
            es = [e - x - _mm(x, _bd(e, m0, m1)) for e, x in zip(es, xs)]
            yield
        sols, kds, egs, gls = [], [], [], []
        for e, (i, d) in zip(es, pd):
            rows = chunk_rows(gi, i, d)
            k, v = ks_ref[rows, :], vs_ref[rows, :]
            beta, gc = dir_ref[d, 0, rows, :], dir_ref[d, 1, rows, :]
            eg = jnp.exp(gc)
            rhs = jnp.concatenate([v * beta, k * beta * eg], axis=1)
            sols.append(rhs + _mm(e, _bd(rhs, m0w, m1w)))
            glast = gc[CH - 1:CH, :] if d == 0 else gc[0:1, :]
            kds.append(k * jnp.exp(glast - gc))
            egs.append(eg)
            gls.append(glast)
        yield
        for sol, kd, eg, glast, (i, d) in zip(sols, kds, egs, gls, pd):
            rows = chunk_rows(gi, i, d)
            attn = ls_ref[slot, i, d, CH:2 * CH, :]
            au = _mm(attn, _bd(sol, m0w, m1w))
            kn = _mm_tn(kd, sol)
            p1_ref[slot, i, d, 0:LANES, :] = kn[:, LANES:] * bdm
            p1_ref[slot, i, d, LANES:LANES + CH, :] = qs_ref[rows, :] * eg - au[:, LANES:]
            p1_ref[slot, i, d, LANES + CH:2 * LANES + CH, :] = kn[:, :LANES] * (bdm * validf)
            p1_ref[slot, i, d, 2 * LANES + CH:2 * LANES + 2 * CH, :] = au[:, :LANES]
            p1_ref[slot, i, d, 2 * LANES + 2 * CH:2 * LANES + 2 * CH + SUBLANES, :] = jnp.broadcast_to(
                jnp.exp(glast), (SUBLANES, LANES))
        yield

    def stage_c(gi, slot, st):
        for i, d in pd:
            rows = chunk_rows(gi, i, d)
            kwq = p1_ref[slot, i, d, 0:LANES + CH, :]
            nmat = p1_ref[slot, i, d, LANES + CH:2 * LANES + CH, :]
            omat = p1_ref[slot, i, d, 2 * LANES + CH:2 * LANES + 2 * CH, :]
            egl = p1_ref[slot, i, d, 2 * LANES + 2 * CH:2 * LANES + 2 * CH + 1, :]
            ks = _mm(kwq, st[d])
            oref = of_ref if d == 0 else ob_ref
            oref[rows, :] = ks[LANES:] + omat
            st[d] = st[d] * egl - ks[:LANES] + nmat
            yield

    def step(t, st):
        even = lax.rem(t, 2)
        st = list(st)
        _round_robin([stage_b(t - 1, 1 - even, (t >= 1).astype(F32)), stage_c(t - 2, even, st),
                      stage_a(t, even)])
        return tuple(st)
    z = jnp.zeros((LANES, LANES), F32)
    lax.fori_loop(0, n_grp + 2, step, (z, z))

    nw = aux_ref[2:3, :]
    for (po, uo, n) in _row_tiles(tc, tl):
        o = of_ref[uo:uo + n, :] + ob_ref[uo:uo + n, :]
        ms = _mm(o * o, bdm) * (1.0 / HD)
        if uo < tc:
            oc_ref[uo:uo + n, :] = o * lax.rsqrt(ms + EPS) * nw * _silu(zc_ref[uo:uo + n, :])
        else:
            lo = uo - tc
            ol_ref[lo:lo + n, :] = o * lax.rsqrt(ms + EPS) * nw * _silu(zl_ref[lo:lo + n, :])


def _gdn_expand_consts():
    e = np.zeros((N_PAIRS, BA_W, 4 * LANES), np.float32)
    for p in range(N_PAIRS):
        for blk in range(4):
            d, is_g = blk % 2, blk // 2
            for h in range(2):
                col = is_g * 2 * GDN_HEADS + d * GDN_HEADS + 2 * p + h
                e[p, col, blk * LANES + h * HD:blk * LANES + (h + 1) * HD] = 1.0
    return e


def _gdn_mixer(qkv_c, z_c, ba_c, qkv_l, z_l, ba_l, conv_w, a_log, dt_bias, norm_w, consts):
    bsz, tc, _ = qkv_c.shape
    tl = qkv_l.shape[1]
    tall = tc + tl
    n_all = tall // CH
    aux = jnp.zeros((SUBLANES, LANES), F32)
    aux = aux.at[0, 2 * GDN_HEADS:4 * GDN_HEADS].set(a_log.reshape(-1))
    aux = aux.at[1, 2 * GDN_HEADS:4 * GDN_HEADS].set(dt_bias.reshape(-1))
    aux = aux.at[2, :].set(jnp.tile(norm_w, 2))
    e = jnp.asarray(_gdn_expand_consts())
    cm = lambda *blk: pl.BlockSpec(blk, lambda b, p: (0,) * len(blk))

    def tok(t, col0):
        return pl.BlockSpec((None, t, LANES), lambda b, p: (b, 0, col0 + p))
    in_specs = [tok(tc, 0), tok(tc, N_PAIRS), tok(tc, 2 * N_PAIRS), tok(tc, 0),
                pl.BlockSpec((None, tc, BA_W), lambda b, p: (b, 0, 0)),
                tok(tl, 0), tok(tl, N_PAIRS), tok(tl, 2 * N_PAIRS), tok(tl, 0),
                pl.BlockSpec((None, tl, BA_W), lambda b, p: (b, 0, 0)),
                pl.BlockSpec((CONV_W, LANES), lambda b, p: (0, p)),
                pl.BlockSpec((CONV_W, LANES), lambda b, p: (0, N_PAIRS + p)),
                pl.BlockSpec((CONV_W, LANES), lambda b, p: (0, 2 * N_PAIRS + p)),
                cm(SUBLANES, LANES),
                pl.BlockSpec((None, BA_W, 4 * LANES), lambda b, p: (p, 0, 0)),
                cm(2, CH, LANES), cm(2, CH, LANES), cm(2, 6, CH, LANES), cm(2, 1, LANES), cm(LANES, LANES),
                cm(2, ROWT, ROWT), cm(ROWT, ROWT)]
    pad_rows = tall + 3 * GAP
    scratch = [pltpu.VMEM((pad_rows, LANES), F32)] * 3 + [pltpu.VMEM((tall, LANES), F32)] * 3 + [
        pltpu.VMEM((2, 3, tall, LANES), F32),
        pltpu.VMEM((2, _group_size(n_all), 2, 2 * CH, LANES), F32),
        pltpu.VMEM((2, _group_size(n_all), 2, 2 * LANES + 2 * CH + SUBLANES, LANES), F32),
        pltpu.VMEM((tall, LANES), F32), pltpu.VMEM((tall, LANES), F32)]
    return pl.pallas_call(
        _gdn_kernel,
        grid=(bsz, N_PAIRS),
        in_specs=in_specs,
        out_specs=[pl.BlockSpec((None, tc, LANES), lambda b, p: (b, 0, p)),
                   pl.BlockSpec((None, tl, LANES), lambda b, p: (b, 0, p))],
        out_shape=[jax.ShapeDtypeStruct((bsz, tc, GDN_WIDTH), F32),
                   jax.ShapeDtypeStruct((bsz, tl, GDN_WIDTH), F32)],
        scratch_shapes=scratch,
        compiler_params=pltpu.CompilerParams(dimension_semantics=("arbitrary", "arbitrary"),
                                             vmem_limit_bytes=VMEM_LIMIT),
        name="gdn_mixer",
    )(qkv_c, qkv_c, qkv_c, z_c, ba_c, qkv_l, qkv_l, qkv_l, z_l, ba_l,
      conv_w, conv_w, conv_w, aux, e,
      consts["incl"], consts["strict"], consts["lvl"], consts["hm"], consts["bdm"],
      consts["tri_t"], consts["ones_t"])


def _rwkv_kernel(rc_ref, kc_ref, vc_ref, wdc_ref, adc_ref, gdc_ref,
                 rl_ref, kl_ref, vl_ref, wdl_ref, adl_ref, gdl_ref,
                 mur_ref, muk_ref, muv_ref, mux_ref, wup_ref, aup_ref, gup_ref, vec_ref,
                 incl_ref, strict_ref, lvl_ref, hm_ref, bdm_ref, trit_ref,
                 oc_ref, ol_ref,
                 vs_ref, gs_ref, bvs_ref, dir_ref, ls_ref, p1_ref, yf_ref, yb_ref):
    tc, tl = rc_ref.shape[0], rl_ref.shape[0]
    tall = tc + tl
    n_ctx, n_all = tc // CH, tall // CH
    grp = _group_size(n_all)
    m0, m1 = hm_ref[0], hm_ref[1]
    m0w = jnp.concatenate([m0, m0], axis=1)
    m1w = jnp.concatenate([m1, m1], axis=1)
    bdm = bdm_ref[...]

    kkw, kaw, rkw = vec_ref[0:1, :], vec_ref[1:2, :], vec_ref[2:3, :]
    for (_, uo, n) in _row_tiles(tc, tl):
        ctx_tile = uo < tc
        lo = uo if ctx_tile else uo - tc
        t_len = tc if ctx_tile else tl
        rowi = lax.broadcasted_iota(jnp.int32, (n, LANES), 0)

        def lerp(src_c, src_l, mu):
            src = src_c if ctx_tile else src_l
            x = src[lo:lo + n, :]
            prev = src[lo - 1:lo, :] if lo > 0 else jnp.zeros((1, LANES), F32)
            nxt = src[lo + n:lo + n + 1, :] if lo + n < t_len else jnp.zeros((1, LANES), F32)
            xm = jnp.where(rowi == 0, prev, pltpu.roll(x, 1, 0))
            xp = jnp.where(rowi == n - 1, nxt, pltpu.roll(x, n - 1, 0))
            return x + mu * (0.5 * (xm + xp) - x)
        r = lerp(rc_ref, rl_ref, mur_ref[...])
        k = lerp(kc_ref, kl_ref, muk_ref[...])
        v = lerp(vc_ref, vl_ref, muv_ref[...])
        wd = lerp(wdc_ref, wdl_ref, mux_ref[0:1, :])
        ad = lerp(adc_ref, adl_ref, mux_ref[1:2, :])
        gd = lerp(gdc_ref, gdl_ref, mux_ref[2:3, :])
        tw = jnp.tanh(wd)
        kkv = k * kkw
        kk = kkv * lax.rsqrt(_mm(kkv * kkv, bdm) + 1e-6)
        ksum = jnp.zeros_like(k)
        for d in range(2):
            logw = -_softplus(-(vec_ref[3 + d:4 + d, :] + _mm(tw, wup_ref[d]))) - 0.5
            lw = -jnp.exp(logw)
            a = _sigmoid(vec_ref[5 + d:6 + d, :] + _mm(ad, aup_ref[d]))
            kdir = k * (1.0 + (a - 1.0) * kaw)
            ksum = ksum + kdir
            cum = _mm_sel_l(trit_ref[d, 0:n, 0:n], lw)
            einv = jnp.exp(-cum)
            dir_ref[d, 0, uo:uo + n, :] = cum
            dir_ref[d, 1, uo:uo + n, :] = kk * jnp.exp(cum - lw)
            dir_ref[d, 2, uo:uo + n, :] = r * jnp.exp(cum)
            dir_ref[d, 3, uo:uo + n, :] = kk * a * einv
            dir_ref[d, 4, uo:uo + n, :] = kdir * einv
        bonus = _mm_sel_r(r * ksum * rkw, bdm)
        vs_ref[uo:uo + n, :] = v
        gs_ref[uo:uo + n, :] = _mm(_sigmoid(gd), gup_ref[...])
        bvs_ref[uo:uo + n, :] = bonus * v

    n_grp = n_all // grp
    p1_ref[0] = jnp.zeros(p1_ref.shape[1:], F32)
    ls_ref[1] = jnp.zeros(ls_ref.shape[1:], F32)
    pd = [(i, d) for i in range(grp) for d in range(2)]

    def chunk_rows(gi, i, d):
        g = jnp.clip(gi, 0, n_grp - 1)
        c = _chunk_order(g * grp + i, n_ctx, n_all, d)
        return pl.ds(pl.multiple_of(c * CH, CH), CH)

    def stage_a(gi, slot):
        def finish(pi, pdir, m, vbd):
            ls_ref[slot, pi, pdir, 0:CH, :] = m[:CH, :LANES] * strict_ref[pdir]
            ls_ref[slot, pi, pdir, CH:2 * CH, :] = _mm(m[:CH, LANES:] * strict_ref[pdir], vbd)
            ls_ref[slot, pi, pdir, 2 * CH:3 * CH, :] = m[CH:, :LANES] * incl_ref[pdir]
            ls_ref[slot, pi, pdir, 3 * CH:4 * CH, :] = _mm(m[CH:, LANES:] * incl_ref[pdir], vbd)
        pending = None
        for i, d in pd:
            rows = chunk_rows(gi, i, d)
            kkq, rq, binv, kinv = (dir_ref[d, j, rows, :] for j in range(1, 5))
            lhs = jnp.concatenate([kkq, rq], axis=0)
            rhs = jnp.concatenate([_bd(binv, m0, m1), _bd(kinv, m0, m1)], axis=0)
            cur = (i, d, _mm_nt(lhs, rhs), _bd(vs_ref[rows, :], m0, m1))
            if pending is not None:
                finish(*pending)
            pending = cur
            yield
        finish(*pending)
        yield

    def stage_b(gi, slot, validf):
        mats = [ls_ref[slot, i, d, 0:CH, :] for i, d in pd]
        es = [-(a * lvl_ref[d, 0]) for a, (_, d) in zip(mats, pd)]
        for lv in range(1, 6):
            offs = [a * lvl_ref[d, lv] for a, (_, d) in zip(mats, pd)]
            xs = [off + _mm(e, _bd(off, m0, m1)) for e, off in zip(es, offs)]
            yield
            es = [e - x - _mm(x, _bd(e, m0, m1)) for e, x in zip(es, xs)]
            yield
        sols = []
        for e, (i, d) in zip(es, pd):
            rows = chunk_rows(gi, i, d)
            rhs = jnp.concatenate([dir_ref[d, 1, rows, :], ls_ref[slot, i, d, CH:2 * CH, :]], axis=1)
            sols.append(rhs + _mm(e, _bd(rhs, m0w, m1w)))
        yield
        for sol, (i, d) in zip(sols, pd):
            rows = chunk_rows(gi, i, d)
            cum = dir_ref[d, 0, rows, :]
            etot = jnp.exp(cum[CH - 1:CH, :] if d == 0 else cum[0:1, :])
            bdec, kdec = dir_ref[d, 3, rows, :] * etot, dir_ref[d, 4, rows, :] * etot
            ar = _mm(ls_ref[slot, i, d, 2 * CH:3 * CH, :], _bd(sol, m0w, m1w))
            pmat = _mm_tn(sol[:, :LANES], bdec)
            nmat = _mm_tn(jnp.concatenate([-sol[:, LANES:], vs_ref[rows, :]], axis=0),
                          jnp.concatenate([bdec, kdec], axis=0))
            p1_ref[slot, i, d, 0:LANES, :] = pmat * bdm
            p1_ref[slot, i, d, LANES:2 * LANES, :] = nmat * (bdm * validf)
            p1_ref[slot, i, d, 2 * LANES:2 * LANES + CH, :] = dir_ref[d, 2, rows, :] - ar[:, :LANES]
            p1_ref[slot, i, d, 2 * LANES + CH:2 * LANES + 2 * CH, :] = (
                ls_ref[slot, i, d, 3 * CH:4 * CH, :] - ar[:, LANES:])
            p1_ref[slot, i, d, 2 * LANES + 2 * CH:2 * LANES + 2 * CH + SUBLANES, :] = jnp.broadcast_to(
                etot, (SUBLANES, LANES))
        yield

    def stage_c(gi, slot, st):
        for i, d in pd:
            rows = chunk_rows(gi, i, d)
            pmat = p1_ref[slot, i, d, 0:LANES, :]
            nmat = p1_ref[slot, i, d, LANES:2 * LANES, :]
            rmat = p1_ref[slot, i, d, 2 * LANES:2 * LANES + CH, :]
            ymat = p1_ref[slot, i, d, 2 * LANES + CH:2 * LANES + 2 * CH, :]
            etot = p1_ref[slot, i, d, 2 * LANES + 2 * CH:2 * LANES + 2 * CH + 1, :]
            yref = yf_ref if d == 0 else yb_ref
            yref[rows, :] = _mm_nt(rmat, st[d]) + ymat
            st[d] = st[d] * etot - _mm(st[d], pmat) + nmat
            yield

    def step(t, st):
        even = lax.rem(t, 2)
        st = list(st)
        _round_robin([stage_b(t - 1, 1 - even, (t >= 1).astype(F32)), stage_c(t - 2, even, st),
                      stage_a(t, even)])
        return tuple(st)
    z = jnp.zeros((LANES, LANES), F32)
    lax.fori_loop(0, n_grp + 2, step, (z, z))

    gnw, gnb = vec_ref[7:8, :], vec_ref[8:9, :]
    for (po, uo, n) in _row_tiles(tc, tl):
        y = yf_ref[uo:uo + n, :] + yb_ref[uo:uo + n, :]
        mean = _mm_sel_r(y, bdm) * (1.0 / HD)
        yc = y - mean
        var = _mm(yc * yc, bdm) * (1.0 / HD)
        out = (yc * lax.rsqrt(var + RWKV_GN_EPS) * gnw + gnb + bvs_ref[uo:uo + n, :]) * gs_ref[uo:uo + n, :]
        if uo < tc:
            oc_ref[uo:uo + n, :] = out
        else:
            ol_ref[uo - tc:uo - tc + n, :] = out


def _rwkv_mixer(rw_c, rw_l, mu, w0, w_up, a0, a_up, g_up, k_k, k_a, r_k, gn_w, gn_b, consts):
    bsz, tc, _ = rw_c.shape
    tl = rw_l.shape[1]
    tall = tc + tl
    n_all = tall // CH
    W = RWKV_WIDTH
    nb = W // LANES
    wup = jnp.zeros((2, LANES, W), F32)
    aup = jnp.zeros((2, LANES, W), F32)
    for d in range(2):
        wup = wup.at[d, d * RWKV_RANK:(d + 1) * RWKV_RANK, :].set(w_up[d])
        aup = aup.at[d, d * RWKV_RANK:(d + 1) * RWKV_RANK, :].set(a_up[d])
    vec = jnp.zeros((16, W), F32)
    vec = vec.at[0].set(k_k).at[1].set(k_a).at[2].set(r_k.reshape(-1))
    vec = vec.at[3].set(w0[0]).at[4].set(w0[1]).at[5].set(a0[0]).at[6].set(a0[1])
    vec = vec.at[7].set(gn_w).at[8].set(gn_b)
    mu_rkv = mu[:3 * W].reshape(3, 1, W)
    mux = jnp.zeros((SUBLANES, LANES), F32).at[0:3, :].set(mu[3 * W:].reshape(3, LANES))
    cm = lambda *blk: pl.BlockSpec(blk, lambda b, p: (0,) * len(blk))

    def tok(t, col0, per_pair=True):
        if per_pair:
            return pl.BlockSpec((None, t, LANES), lambda b, p: (b, 0, col0 + p))
        return pl.BlockSpec((None, t, LANES), lambda b, p: (b, 0, col0))
    stream = lambda t: [tok(t, 0), tok(t, nb), tok(t, 2 * nb), tok(t, 3 * nb, False),
                        tok(t, 3 * nb + 1, False), tok(t, 3 * nb + 2, False)]
    in_specs = stream(tc) + stream(tl) + [
        pl.BlockSpec((None, 1, LANES), lambda b, p: (0, 0, p)),
        pl.BlockSpec((None, 1, LANES), lambda b, p: (1, 0, p)),
        pl.BlockSpec((None, 1, LANES), lambda b, p: (2, 0, p)),
        cm(SUBLANES, LANES),
        pl.BlockSpec((2, LANES, LANES), lambda b, p: (0, 0, p)),
        pl.BlockSpec((2, LANES, LANES), lambda b, p: (0, 0, p)),
        pl.BlockSpec((RWKV_G_RANK, LANES), lambda b, p: (0, p)),
        pl.BlockSpec((16, LANES), lambda b, p: (0, p)),
        cm(2, CH, LANES), cm(2, CH, LANES), cm(2, 6, CH, LANES), cm(2, 1, LANES), cm(LANES, LANES),
        cm(2, ROWT, ROWT)]
    scratch = [pltpu.VMEM((tall, LANES), F32)] * 3 + [
        pltpu.VMEM((2, 5, tall, LANES), F32),
        pltpu.VMEM((2, _group_size(n_all), 2, 4 * CH, LANES), F32),
        pltpu.VMEM((2, _group_size(n_all), 2, 2 * LANES + 2 * CH + SUBLANES, LANES), F32),
        pltpu.VMEM((tall, LANES), F32), pltpu.VMEM((tall, LANES), F32)]
    return pl.pallas_call(
        _rwkv_kernel,
        grid=(bsz, N_PAIRS),
        in_specs=in_specs,
        out_specs=[pl.BlockSpec((None, tc, LANES), lambda b, p: (b, 0, p)),
                   pl.BlockSpec((None, tl, LANES), lambda b, p: (b, 0, p))],
        out_shape=[jax.ShapeDtypeStruct((bsz, tc, W), F32),
                   jax.ShapeDtypeStruct((bsz, tl, W), F32)],
        scratch_shapes=scratch,
        compiler_params=pltpu.CompilerParams(dimension_semantics=("arbitrary", "arbitrary"),
                                             vmem_limit_bytes=VMEM_LIMIT),
        name="rwkv_mixer",
    )(*([rw_c] * 6), *([rw_l] * 6), mu_rkv, mu_rkv, mu_rkv, mux, wup, aup, g_up, vec,
      consts["incl"], consts["strict"], consts["lvl"], consts["hm"], consts["bdm"], consts["tri_t"])


def _lru_kernel(xc_ref, gc_ref, xl_ref, gl_ref, cw_ref, vec_ref, wcat_ref, bcat_ref,
                oc_ref, ol_ref, xp_ref, ab_ref, hf_ref, hb_ref):
    tc, tl = xc_ref.shape[0], xl_ref.shape[0]
    tall = tc + tl
    rows = tl // GRID_W
    W = xc_ref.shape[1]
    z = jnp.zeros((GAP, W), F32)
    xp_ref[0:GAP, :] = z
    xp_ref[GAP:GAP + tc, :] = xc_ref[...]
    xp_ref[GAP + tc:2 * GAP + tc, :] = z
    base = 2 * GAP + tc
    for c in range(GRID_W):
        xp_ref[base + c * rows:base + (c + 1) * rows, :] = xl_ref[pl.ds(c, rows, stride=GRID_W), :]
    xp_ref[base + tl:base + tl + GAP, :] = z
    cb = vec_ref[0:1, :]
    for (po, uo, n) in _row_tiles(tc, tl):
        acc = cb + cw_ref[0:1, :] * xp_ref[po - 2:po - 2 + n, :]
        for tap in range(1, CONV_W):
            acc = acc + cw_ref[tap:tap + 1, :] * xp_ref[po - 2 + tap:po - 2 + tap + n, :]
        gates = _mm(acc, wcat_ref[...]) + bcat_ref[...]
        for d in range(2):
            rg = _sigmoid(gates[:, (2 * d) * W:(2 * d + 1) * W])
            ig = _sigmoid(gates[:, (2 * d + 1) * W:(2 * d + 2) * W])
            log_a = -LRU_C * rg * _softplus(-vec_ref[1 + d:2 + d, :])
            a = jnp.exp(log_a)
            mult = jnp.sqrt(-jnp.tanh(log_a) * (1.0 + a * a))
            ab_ref[d, 0, uo:uo + n, :] = a
            ab_ref[d, 1, uo:uo + n, :] = mult * (ig * acc)

    sub = lax.broadcasted_iota(jnp.int32, (SUBLANES, W), 0)
    n_tiles_c, n_tiles = tc // SUBLANES, tall // SUBLANES

    def tile_scan(a, b, d):
        for sh in (1, 2, 4):
            if d == 0:
                ok = sub >= sh
                a_s = jnp.where(ok, pltpu.roll(a, sh, 0), 1.0)
                b_s = jnp.where(ok, pltpu.roll(b, sh, 0), 0.0)
            else:
                ok = sub < SUBLANES - sh
                a_s = jnp.where(ok, pltpu.roll(a, SUBLANES - sh, 0), 1.0)
                b_s = jnp.where(ok, pltpu.roll(b, SUBLANES - sh, 0), 0.0)
            b = b + a * b_s
            a = a * a_s
        return a, b

    def scan_body(s, carry):
        hf, hb = carry
        r0 = pl.multiple_of(s * SUBLANES, SUBLANES)
        a, b = tile_scan(ab_ref[0, 0, pl.ds(r0, SUBLANES), :], ab_ref[0, 1, pl.ds(r0, SUBLANES), :], 0)
        h = b + a * hf
        hf_ref[pl.ds(r0, SUBLANES), :] = h
        hf = jnp.broadcast_to(h[SUBLANES - 1:SUBLANES, :], (SUBLANES, W))
        tb = jnp.where(s < n_tiles_c, n_tiles_c - 1 - s, n_tiles + n_tiles_c - 1 - s)
        r1 = pl.multiple_of(tb * SUBLANES, SUBLANES)
        a, b = tile_scan(ab_ref[1, 0, pl.ds(r1, SUBLANES), :], ab_ref[1, 1, pl.ds(r1, SUBLANES), :], 1)
        h = b + a * hb
        hb_ref[pl.ds(r1, SUBLANES), :] = h
        hb = jnp.broadcast_to(h[0:1, :], (SUBLANES, W))
        return hf, hb
    z8 = jnp.zeros((SUBLANES, W), F32)
    lax.fori_loop(0, n_tiles, scan_body, (z8, z8))

    oc_ref[...] = (hf_ref[0:tc, :] + hb_ref[0:tc, :]) * _gelu_tanh(gc_ref[...])
    for c in range(GRID_W):
        h = hf_ref[tc + c * rows:tc + (c + 1) * rows, :] + hb_ref[tc + c * rows:tc + (c + 1) * rows, :]
        ol_ref[pl.ds(c, rows, stride=GRID_W), :] = h * _gelu_tanh(gl_ref[pl.ds(c, rows, stride=GRID_W), :])


def _lru_mixer(x_c, g_c, x_l, g_l, conv_w, conv_b, w_a, b_a, w_x, b_x, lam):
    bsz, tc, W = x_c.shape
    tl = x_l.shape[1]
    tall = tc + tl
    nh = W // LANES
    bph = LANES // LRU_BW
    wcat = jnp.zeros((nh, LANES, 4 * LANES), F32)
    bcat = jnp.zeros((nh, 1, 4 * LANES), F32)
    for hh in range(nh):
        for d in range(2):
            for gi, (wsrc, bsrc) in enumerate(((w_a, b_a), (w_x, b_x))):
                c0 = (2 * d + gi) * LANES
                bcat = bcat.at[hh, 0, c0:c0 + LANES].set(bsrc[d, hh * LANES:(hh + 1) * LANES])
                for n in range(bph):
                    wcat = wcat.at[hh, n * LRU_BW:(n + 1) * LRU_BW,
                                   c0 + n * LRU_BW:c0 + (n + 1) * LRU_BW].set(wsrc[d, hh * bph + n])
    vec = jnp.zeros((SUBLANES, W), F32).at[0].set(conv_b).at[1].set(lam[0]).at[2].set(lam[1])
    tok = lambda t: pl.BlockSpec((None, t, LANES), lambda b, h: (b, 0, h))
    return pl.pallas_call(
        _lru_kernel,
        grid=(bsz, nh),
        in_specs=[tok(tc), tok(tc), tok(tl), tok(tl),
                  pl.BlockSpec((CONV_W, LANES), lambda b, h: (0, h)),
                  pl.BlockSpec((SUBLANES, LANES), lambda b, h: (0, h)),
                  pl.BlockSpec((None, LANES, 4 * LANES), lambda b, h: (h, 0, 0)),
                  pl.BlockSpec((None, 1, 4 * LANES), lambda b, h: (h, 0, 0))],
        out_specs=[tok(tc), tok(tl)],
        out_shape=[jax.ShapeDtypeStruct((bsz, tc, W), F32), jax.ShapeDtypeStruct((bsz, tl, W), F32)],
        scratch_shapes=[pltpu.VMEM((tall + 3 * GAP, LANES), F32),
                        pltpu.VMEM((2, 2, tall, LANES), F32),
                        pltpu.VMEM((tall, LANES), F32), pltpu.VMEM((tall, LANES), F32)],
        compiler_params=pltpu.CompilerParams(dimension_semantics=("arbitrary", "arbitrary"),
                                             vmem_limit_bytes=VMEM_LIMIT),
        name="lru_mixer",
    )(x_c, g_c, x_l, g_l, conv_w, vec, wcat.astype(BF16), bcat)


def _finish_kernel(x_ref, gdn_ref, lru_ref, rwk_ref, m2_ref, m3_ref, m4_ref, m5_ref, nrm_ref,
                   wo_ref, up_ref, dn_ref, o_ref):
    x = x_ref[...]
    o = (jnp.dot(gdn_ref[...].astype(BF16), wo_ref[0:GDN_WIDTH, :], preferred_element_type=F32)
         + jnp.dot(lru_ref[...].astype(BF16), wo_ref[GDN_WIDTH:GDN_WIDTH + LRU_WIDTH, :],
                   preferred_element_type=F32)
         + jnp.dot(rwk_ref[...].astype(BF16), wo_ref[GDN_WIDTH + LRU_WIDTH:, :], preferred_element_type=F32))
    x = x + m2_ref[...] * _rms(o, nrm_ref[0:1, :])
    h = (_rms(x, nrm_ref[1:2, :]) * (1.0 + m4_ref[...]) + m3_ref[...]).astype(BF16)
    f = jnp.zeros_like(x)
    fc = 1024
    for j in range(D_FF // fc):
        a = jnp.maximum(jnp.dot(h, up_ref[:, j * fc:(j + 1) * fc], preferred_element_type=F32), 0.0)
        f = f + jnp.dot((a * a).astype(BF16), dn_ref[j * fc:(j + 1) * fc, :], preferred_element_type=F32)
    o_ref[...] = x + m5_ref[...] * _rms(f, nrm_ref[2:3, :])


def _finish(x2, gdn, lru, rwk, mod_rows, rows_per_mod, norms, wo_bf, up_bf, dn_bf):
    n = x2.shape[0]
    tm = 256
    tiles_per_mod = rows_per_mod // tm
    modspec = lambda k: pl.BlockSpec((None, 1, D_MODEL), lambda i: (6 * (i // tiles_per_mod) + k, 0, 0))
    cm = lambda *blk: pl.BlockSpec(blk, lambda i: (0,) * len(blk))
    return pl.pallas_call(
        _finish_kernel,
        grid=(n // tm,),
        in_specs=[pl.BlockSpec((tm, D_MODEL), lambda i: (i, 0)),
                  pl.BlockSpec((tm, GDN_WIDTH), lambda i: (i, 0)),
                  pl.BlockSpec((tm, LRU_WIDTH), lambda i: (i, 0)),
                  pl.BlockSpec((tm, RWKV_WIDTH), lambda i: (i, 0)),
                  modspec(2), modspec(3), modspec(4), modspec(5),
                  cm(SUBLANES, D_MODEL), cm(D_MODEL, D_MODEL), cm(D_MODEL, D_FF), cm(D_FF, D_MODEL)],
        out_specs=pl.BlockSpec((tm, D_MODEL), lambda i: (i, 0)),
        out_shape=jax.ShapeDtypeStruct((n, D_MODEL), F32),
        compiler_params=pltpu.CompilerParams(dimension_semantics=("arbitrary",),
                                             vmem_limit_bytes=VMEM_LIMIT),
        name="finish",
    )(x2, gdn, lru, rwk, mod_rows, mod_rows, mod_rows, mod_rows, norms, wo_bf, up_bf, dn_bf)


def _arrange_w_in(w):
    s = np.cumsum([0, 3 * GDN_WIDTH, GDN_WIDTH, 2 * GDN_HEADS, 2 * GDN_HEADS, LRU_WIDTH, LRU_WIDTH, RWKV_IN])
    pad = jnp.zeros((w.shape[0], BA_W - 4 * GDN_HEADS), w.dtype)
    return jnp.concatenate([w[:, s[0]:s[2]], w[:, s[2]:s[4]], pad, w[:, s[4]:]], axis=1)


def kernel(x, c, ctx, c_ctx, ada_w, ada_b, norm_mix_pre, norm_mix_post, norm_ffn_pre, norm_ffn_post, w_in, gdn_conv, gdn_a_log, gdn_dt_bias, gdn_norm, lru_conv, lru_conv_b, lru_wa, lru_ba, lru_wx, lru_bx, lru_lambda, rwkv_mu, rwkv_w0, rwkv_w_up, rwkv_a0, rwkv_a_up, rwkv_g_up, rwkv_k_k, rwkv_k_a, rwkv_r_k, rwkv_gn_w, rwkv_gn_b, w_out, ffn_up, ffn_down):
    bsz, tl, _ = x.shape
    tc = ctx.shape[1]
    depth = w_in.shape[0]
    consts = _device_consts()

    cvec = jnp.zeros((16, D_MODEL), F32).at[0:bsz].set(c).at[bsz].set(c_ctx)
    mods = _ada_mod(cvec, ada_w, ada_b).reshape(depth, 16, 6, D_MODEL)

    xl = x.reshape(bsz * tl, D_MODEL)
    xc = ctx.reshape(bsz * tc, D_MODEL)
    for i in range(depth):
        mod_l = mods[i, 0:bsz].reshape(bsz * 6, 1, D_MODEL)
        mod_c = mods[i, bsz:bsz + 1].reshape(6, 1, D_MODEL)
        w_bf = _arrange_w_in(w_in[i]).astype(BF16)
        pl_ = _inproj(xl, mod_l, tl, norm_mix_pre[i], w_bf)
        pc_ = _inproj(xc, mod_c, bsz * tc, norm_mix_pre[i], w_bf)
        r3 = lambda a, t: a.reshape(bsz, t, a.shape[-1])
        qkv_l, z_l, ba_l, lx_l, lg_l, rw_l = (r3(a, tl) for a in pl_)
        qkv_c, z_c, ba_c, lx_c, lg_c, rw_c = (r3(a, tc) for a in pc_)

        gdn_c, gdn_l = _gdn_mixer(qkv_c, z_c, ba_c, qkv_l, z_l, ba_l, gdn_conv[i], gdn_a_log[i],
                                  gdn_dt_bias[i], gdn_norm[i], consts)
        lru_c, lru_l = _lru_mixer(lx_c, lg_c, lx_l, lg_l, lru_conv[i], lru_conv_b[i], lru_wa[i], lru_ba[i],
                                  lru_wx[i], lru_bx[i], lru_lambda[i])
        rwk_c, rwk_l = _rwkv_mixer(rw_c, rw_l, rwkv_mu[i], rwkv_w0[i], rwkv_w_up[i], rwkv_a0[i],
                                   rwkv_a_up[i], rwkv_g_up[i], rwkv_k_k[i], rwkv_k_a[i], rwkv_r_k[i],
                                   rwkv_gn_w[i], rwkv_gn_b[i], consts)

        norms = jnp.zeros((SUBLANES, D_MODEL), F32).at[0].set(norm_mix_post[i]).at[1].set(
            norm_ffn_pre[i]).at[2].set(norm_ffn_post[i])
        wo_bf, up_bf, dn_bf = w_out[i].astype(BF16), ffn_up[i].astype(BF16), ffn_down[i].astype(BF16)
        f2 = lambda a: a.reshape(-1, a.shape[-1])
        xl = _finish(xl, f2(gdn_l), f2(lru_l), f2(rwk_l), mod_l, tl, norms, wo_bf, up_bf, dn_bf)
        if i < depth - 1:
            xc = _finish(xc, f2(gdn_c), f2(lru_c), f2(rwk_c), mod_c, bsz * tc, norms, wo_bf, up_bf, dn_bf)
    return xl.reshape(bsz, tl, D_MODEL)
```

```python
import functools

import numpy as np
import jax
import jax.numpy as jnp
from jax import lax
from jax.experimental import pallas as pl
from jax.experimental.pallas import tpu as pltpu

F32 = jnp.float32
BF16 = jnp.bfloat16

LANES = 128
SUBLANES = 8
VMEM_LIMIT = 56 * 1024 * 1024

D_MODEL = 1024
DEPTH = 2
GRID_W = 64
CONV_W = 4
EPS = 1e-6
D_FF = 4 * D_MODEL
HD = 64
GDN_WIDTH = 3 * D_MODEL // 8
GDN_HEADS = GDN_WIDTH // HD
LRU_WIDTH = D_MODEL // 4
LRU_BLOCKS = 4
LRU_BW = LRU_WIDTH // LRU_BLOCKS
LRU_C = 8.0
RWKV_WIDTH = D_MODEL - GDN_WIDTH - LRU_WIDTH
RWKV_HEADS = RWKV_WIDTH // HD
RWKV_RANK = 64
RWKV_G_RANK = 128
RWKV_GN_EPS = 6.4e-4
RWKV_IN = 3 * RWKV_WIDTH + 2 * RWKV_RANK + 2 * RWKV_RANK + RWKV_G_RANK
N_PAIRS = GDN_HEADS // 2
BA_W = LANES
P_OFF = np.cumsum([0, 3 * GDN_WIDTH, GDN_WIDTH, BA_W, LRU_WIDTH, LRU_WIDTH, RWKV_IN])
D_INP = int(P_OFF[-1])

CH = 64
ROWT = 256
DENSE_TM = 512
GAP = SUBLANES


def _mm(a, b):
    return jnp.dot(a.astype(BF16), b.astype(BF16), preferred_element_type=F32)


def _mm_nt(a, b):
    return lax.dot_general(a.astype(BF16), b.astype(BF16), (((1,), (1,)), ((), ())),
                           preferred_element_type=F32)


def _mm_tn(a, b):
    return lax.dot_general(a.astype(BF16), b.astype(BF16), (((0,), (0,)), ((), ())),
                           preferred_element_type=F32)


def _split2(x):
    hi = x.astype(BF16)
    lo = (x - hi.astype(F32)).astype(BF16)
    return hi, lo


def _mm_sel_l(m01, x):
    mb = m01.astype(BF16)
    h, l = _split2(x)
    d = functools.partial(jnp.dot, preferred_element_type=F32)
    return d(mb, h) + d(mb, l)


def _mm_sel_r(x, m01):
    mb = m01.astype(BF16)
    h, l = _split2(x)
    d = functools.partial(jnp.dot, preferred_element_type=F32)
    return d(h, mb) + d(l, mb)


def _sigmoid(x):
    return 0.5 * jnp.tanh(0.5 * x) + 0.5


def _silu(x):
    return x * _sigmoid(x)


def _softplus(x):
    return jnp.maximum(x, 0.0) + jnp.log(1.0 + jnp.exp(-jnp.abs(x)))


def _gelu_tanh(x):
    return 0.5 * x * (1.0 + jnp.tanh(0.7978845608028654 * (x + 0.044715 * (x * x * x))))


def _bd(y, m0, m1):
    yb = y.astype(BF16)
    return jnp.concatenate([yb * m0, yb * m1], axis=0)


def _pair_consts():
    i = np.arange(CH)[:, None]
    j = (np.arange(LANES) % HD)[None, :]
    incl = np.stack([i >= j, i <= j]).astype(np.float32)
    strict = np.stack([i > j, i < j]).astype(np.float32)
    lvls = []
    for d in range(2):
        per = []
        for m in (1, 2, 4, 8, 16, 32):
            same = (i // (2 * m)) == (j // (2 * m))
            lo_i, lo_j = (i % (2 * m)) < m, (j % (2 * m)) < m
            off = same & (~lo_i) & lo_j if d == 0 else same & lo_i & (~lo_j)
            per.append(off)
        lvls.append(np.stack(per))
    lvl = np.stack(lvls).astype(np.float32)
    lane = np.arange(LANES)
    hm = np.stack([lane < HD, lane >= HD]).astype(np.float32)[:, None, :]
    bdm = ((np.arange(LANES)[:, None] // HD) == (lane[None, :] // HD)).astype(np.float32)
    r = np.arange(ROWT)[:, None]
    c = np.arange(ROWT)[None, :]
    same = (r // CH) == (c // CH)
    tri_t = np.stack([same & (r >= c), same & (r <= c)]).astype(np.float32)
    ones_t = same.astype(np.float32)
    return dict(incl=incl, strict=strict, lvl=lvl, hm=hm, bdm=bdm, tri_t=tri_t, ones_t=ones_t)


def _device_consts():
    out = {k: jnp.asarray(v) for k, v in _pair_consts().items()}
    for k in ("tri_t", "ones_t"):
        out[k] = out[k].astype(BF16)
    return out


def _inv_levels(mats, dirs, lvl_ref, m0, m1, out):
    es = [-(a * lvl_ref[d, 0]) for a, d in zip(mats, dirs)]
    for k in range(1, 6):
        offs = [a * lvl_ref[d, k] for a, d in zip(mats, dirs)]
        xs = [off + _mm(e, _bd(off, m0, m1)) for e, off in zip(es, offs)]
        yield
        es = [e - x - _mm(x, _bd(e, m0, m1)) for e, x in zip(es, xs)]
        yield
    out.extend(es)


def _round_robin(gens):
    gens = list(gens)
    while gens:
        alive = []
        for g in gens:
            try:
                next(g)
                alive.append(g)
            except StopIteration:
                pass
        gens = alive


def _group_size(n_all):
    for g in (6, 4, 3, 2):
        if n_all % g == 0:
            return g
    return 1


def _chunk_order(s, n_ctx, n_all, d):
    if d == 0:
        return s
    return jnp.where(s < n_ctx, n_ctx - 1 - s, n_all + n_ctx - 1 - s)


def _fill_padded(dst_ref, src_c_ref, src_l_ref, tc, tl):
    w = dst_ref.shape[1]
    z = jnp.zeros((GAP, w), F32)
    dst_ref[0:GAP, :] = z
    dst_ref[GAP:GAP + tc, :] = src_c_ref[...]
    dst_ref[GAP + tc:2 * GAP + tc, :] = z
    dst_ref[2 * GAP + tc:2 * GAP + tc + tl, :] = src_l_ref[...]
    dst_ref[2 * GAP + tc + tl:3 * GAP + tc + tl, :] = z


def _row_tiles(tc, tl):
    out = []
    for base_p, base_u, n in ((GAP, 0, tc), (2 * GAP + tc, tc, tl)):
        for t0 in range(0, n, ROWT):
            out.append((base_p + t0, base_u + t0, min(ROWT, n - t0)))
    return out


def _ada_kernel(c_ref, w_ref, b_ref, o_ref):
    c = c_ref[...]
    o_ref[...] = _mm(_silu(c), w_ref[...]) + b_ref[...]


def _ada_mod(cvec, ada_w, ada_b):
    L = ada_w.shape[0]
    n = ada_w.shape[2]
    tn = 1536
    return pl.pallas_call(
        _ada_kernel,
        grid=(L, n // tn),
        in_specs=[pl.BlockSpec((16, D_MODEL), lambda l, j: (0, 0)),
                  pl.BlockSpec((None, D_MODEL, tn), lambda l, j: (l, 0, j)),
                  pl.BlockSpec((None, 1, tn), lambda l, j: (l, 0, j))],
        out_specs=pl.BlockSpec((None, 16, tn), lambda l, j: (l, 0, j)),
        out_shape=jax.ShapeDtypeStruct((L, 16, n), F32),
        compiler_params=pltpu.CompilerParams(dimension_semantics=("arbitrary", "arbitrary"),
                                             vmem_limit_bytes=VMEM_LIMIT),
        name="ada_mod",
    )(cvec, ada_w, ada_b.reshape(L, 1, n))


def _rms(x, g):
    return x * lax.rsqrt(jnp.mean(x * x, axis=-1, keepdims=True) + EPS) * g


def _inproj_kernel(x_ref, sh_ref, sc_ref, g_ref, w_ref, qkv_ref, z_ref, ba_ref, lx_ref, lg_ref, rw_ref):
    h = _rms(x_ref[...], g_ref[...]) * (1.0 + sc_ref[...]) + sh_ref[...]
    p = jnp.dot(h.astype(BF16), w_ref[...], preferred_element_type=F32)
    for ref, k in zip((qkv_ref, z_ref, ba_ref, lx_ref, lg_ref, rw_ref), range(6)):
        ref[...] = p[:, int(P_OFF[k]):int(P_OFF[k + 1])]


def _inproj(x2, mod_rows, rows_per_mod, g, w_bf):
    n = x2.shape[0]
    tm = DENSE_TM
    tiles_per_mod = rows_per_mod // tm
    widths = [int(P_OFF[k + 1] - P_OFF[k]) for k in range(6)]
    return pl.pallas_call(
        _inproj_kernel,
        grid=(n // tm,),
        in_specs=[pl.BlockSpec((tm, D_MODEL), lambda i: (i, 0)),
                  pl.BlockSpec((None, 1, D_MODEL), lambda i: (6 * (i // tiles_per_mod), 0, 0)),
                  pl.BlockSpec((None, 1, D_MODEL), lambda i: (6 * (i // tiles_per_mod) + 1, 0, 0)),
                  pl.BlockSpec((1, D_MODEL), lambda i: (0, 0)),
                  pl.BlockSpec((D_MODEL, D_INP), lambda i: (0, 0), pipeline_mode=pl.Buffered(1))],
        out_specs=[pl.BlockSpec((tm, w), lambda i: (i, 0)) for w in widths],
        out_shape=[jax.ShapeDtypeStruct((n, w), F32) for w in widths],
        compiler_params=pltpu.CompilerParams(dimension_semantics=("arbitrary",),
                                             vmem_limit_bytes=VMEM_LIMIT),
        name="inproj",
    )(x2, mod_rows, mod_rows, g.reshape(1, D_MODEL), w_bf)


def _gdn_kernel(qc_ref, kc_ref, vc_ref, zc_ref, bac_ref, ql_ref, kl_ref, vl_ref, zl_ref, bal_ref,
                cwq_ref, cwk_ref, cwv_ref, aux_ref, e_ref,
                incl_ref, strict_ref, lvl_ref, hm_ref, bdm_ref, trit_ref, onest_ref,
                oc_ref, ol_ref,
                qp_ref, kp_ref, vp_ref, qs_ref, ks_ref, vs_ref, dir_ref, ls_ref, p1_ref, of_ref, ob_ref):
    tc, tl = qc_ref.shape[0], ql_ref.shape[0]
    tall = tc + tl
    n_ctx, n_all = tc // CH, tall // CH
    grp = _group_size(n_all)
    m0, m1 = hm_ref[0].astype(BF16), hm_ref[1].astype(BF16)
    m0w = jnp.concatenate([m0, m0], axis=1)
    m1w = jnp.concatenate([m1, m1], axis=1)
    bdm = bdm_ref[...]

    _fill_padded(qp_ref, qc_ref, ql_ref, tc, tl)
    _fill_padded(kp_ref, kc_ref, kl_ref, tc, tl)
    _fill_padded(vp_ref, vc_ref, vl_ref, tc, tl)
    lane = lax.broadcasted_iota(jnp.int32, (1, LANES), 1)
    alog, dtb = aux_ref[0:1, :], aux_ref[1:2, :]
    for (po, uo, n) in _row_tiles(tc, tl):
        def conv(src, cw):
            acc = cw[0:1, :] * src[po - 2:po - 2 + n, :]
            for tap in range(1, CONV_W):
                acc = acc + cw[tap:tap + 1, :] * src[po - 2 + tap:po - 2 + tap + n, :]
            return _silu(acc)
        q = conv(qp_ref, cwq_ref)
        k = conv(kp_ref, cwk_ref)
        v = conv(vp_ref, cwv_ref)
        q = q * lax.rsqrt(_mm(q * q, bdm) + 1e-6) * (HD ** -0.5)
        k = k * lax.rsqrt(_mm(k * k, bdm) + 1e-6)
        qs_ref[uo:uo + n, :] = q
        ks_ref[uo:uo + n, :] = k
        vs_ref[uo:uo + n, :] = v
        ba = bac_ref[uo:uo + n, :] if uo < tc else bal_ref[uo - tc:uo - tc + n, :]
        beta = _sigmoid(ba)
        gval = -jnp.exp(alog) * _softplus(ba + dtb)
        bgv = jnp.where(lane < 2 * GDN_HEADS, beta, gval)
        ex = _mm_sel_r(bgv, e_ref[...])
        incl_t = [jnp.concatenate([incl_ref[dd]] * (n // CH), axis=0) for dd in range(2)]
        for d in range(2):
            g = ex[:, (2 + d) * LANES:(3 + d) * LANES]
            gc = _mm_sel_l(trit_ref[d, 0:n, 0:n], g)
            gr = _mm_sel_l(onest_ref[0:n, 0:n], g * incl_t[1 - d])
            dir_ref[d, 0, uo:uo + n, :] = ex[:, d * LANES:(d + 1) * LANES]
            dir_ref[d, 1, uo:uo + n, :] = gc
            dir_ref[d, 2, uo:uo + n, :] = jnp.exp(jnp.minimum(gc - gr, 0.0)) * incl_t[d]

    n_grp = n_all // grp
    p1_ref[0] = jnp.zeros(p1_ref.shape[1:], F32)
    ls_ref[1] = jnp.zeros(ls_ref.shape[1:], F32)

    def chunk_rows(gi, i, d):
        g = jnp.clip(gi, 0, n_grp - 1)
        c = _chunk_order(g * grp + i, n_ctx, n_all, d)
        return pl.ds(pl.multiple_of(c * CH, CH), CH)

    pd = [(i, d) for i in range(grp) for d in range(2)]

    def stage_a(gi, slot):
        for i, d in pd:
            rows = chunk_rows(gi, i, d)
            q, k = qs_ref[rows, :], ks_ref[rows, :]
            beta, dec = dir_ref[d, 0, rows, :], dir_ref[d, 2, rows, :]
            la = _mm_nt(jnp.concatenate([k, q], axis=0), _bd(k, m0, m1))
            ls_ref[slot, i, d, 0:CH, :] = la[:CH] * dec * strict_ref[d] * beta
            ls_ref[slot, i, d, CH:2 * CH, :] = la[CH:] * dec
            yield

    def stage_b(gi, slot, p_slot, validf):
        mats = [ls_ref[slot, i, d, 0:CH, :] for i, d in pd]
        es = []
        yield from _inv_levels(mats, [d for _, d in pd], lvl_ref, m0, m1, es)
        sols, kds, egs, gls = [], [], [], []
        for e, (i, d) in zip(es, pd):
            rows = chunk_rows(gi, i, d)
            k, v = ks_ref[rows, :], vs_ref[rows, :]
            beta, gc = dir_ref[d, 0, rows, :], dir_ref[d, 1, rows, :]
            eg = jnp.exp(gc)
            rhs = jnp.concatenate([v * beta, k * beta * eg], axis=1)
            sols.append(rhs + _mm(e, _bd(rhs, m0w, m1w)))
            glast = gc[CH - 1:CH, :] if d == 0 else gc[0:1, :]
            kds.append(k * jnp.exp(glast - gc))
            egs.append(eg)
            gls.append(glast)
        yield
        for sol, kd, eg, glast, (i, d) in zip(sols, kds, egs, gls, pd):
            rows = chunk_rows(gi, i, d)
            attn = ls_ref[slot, i, d, CH:2 * CH, :]
            au = _mm(attn, _bd(sol, m0w, m1w))
            kn = _mm_tn(kd, sol)
            p1_ref[p_slot, i, d, 0:LANES, :] = kn[:, LANES:] * bdm
            p1_ref[p_slot, i, d, LANES:LANES + CH, :] = qs_ref[rows, :] * eg - au[:, LANES:]
            p1_ref[p_slot, i, d, LANES + CH:2 * LANES + CH, :] = kn[:, :LANES] * (bdm * validf)
            p1_ref[p_slot, i, d, 2 * LANES + CH:2 * LANES + 2 * CH, :] = au[:, :LANES]
            p1_ref[p_slot, i, d, 2 * LANES + 2 * CH:2 * LANES + 2 * CH + SUBLANES, :] = jnp.broadcast_to(
                jnp.exp(glast), (SUBLANES, LANES))
        yield

    def stage_c(gi, slot, st):
        for i, d in pd:
            rows = chunk_rows(gi, i, d)
            kwq = p1_ref[slot, i, d, 0:LANES + CH, :]
            nmat = p1_ref[slot, i, d, LANES + CH:2 * LANES + CH, :]
            omat = p1_ref[slot, i, d, 2 * LANES + CH:2 * LANES + 2 * CH, :]
            egl = p1_ref[slot, i, d, 2 * LANES + 2 * CH:2 * LANES + 2 * CH + 1, :]
            ks = _mm(kwq, st[d])
            oref = of_ref if d == 0 else ob_ref
            oref[rows, :] = ks[LANES:] + omat
            st[d] = st[d] * egl - ks[:LANES] + nmat
            yield

    def step(t, st):
        par = lax.rem(t, 2)
        st = list(st)
        _round_robin([stage_b(t - 1, 1 - par, 1 - par, jnp.where(t >= 1, 1.0, 0.0).astype(F32)),
                      stage_c(t - 2, par, st),
                      stage_a(t, par)])
        return tuple(st)
    z = jnp.zeros((LANES, LANES), F32)
    lax.fori_loop(0, n_grp + 2, step, (z, z))

    nw = aux_ref[2:3, :]
    for (po, uo, n) in _row_tiles(tc, tl):
        o = of_ref[uo:uo + n, :] + ob_ref[uo:uo + n, :]
        ms = _mm(o * o, bdm) * (1.0 / HD)
        if uo < tc:
            oc_ref[uo:uo + n, :] = o * lax.rsqrt(ms + EPS) * nw * _silu(zc_ref[uo:uo + n, :])
        else:
            lo = uo - tc
            ol_ref[lo:lo + n, :] = o * lax.rsqrt(ms + EPS) * nw * _silu(zl_ref[lo:lo + n, :])


def _gdn_expand_consts():
    e = np.zeros((N_PAIRS, BA_W, 4 * LANES), np.float32)
    for p in range(N_PAIRS):
        for blk in range(4):
            d, is_g = blk % 2, blk // 2
            for h in range(2):
                col = is_g * 2 * GDN_HEADS + d * GDN_HEADS + 2 * p + h
                e[p, col, blk * LANES + h * HD:blk * LANES + (h + 1) * HD] = 1.0
    return e


def _gdn_mixer(qkv_c, z_c, ba_c, qkv_l, z_l, ba_l, conv_w, a_log, dt_bias, norm_w, consts):
    bsz, tc, _ = qkv_c.shape
    tl = qkv_l.shape[1]
    tall = tc + tl
    n_all = tall // CH
    aux = jnp.zeros((SUBLANES, LANES), F32)
    aux = aux.at[0, 2 * GDN_HEADS:4 * GDN_HEADS].set(a_log.reshape(-1))
    aux = aux.at[1, 2 * GDN_HEADS:4 * GDN_HEADS].set(dt_bias.reshape(-1))
    aux = aux.at[2, :].set(jnp.tile(norm_w, 2))
    e = jnp.asarray(_gdn_expand_consts())
    cm = lambda *blk: pl.BlockSpec(blk, lambda b, p: (0,) * len(blk))

    def tok(t, col0):
        return pl.BlockSpec((None, t, LANES), lambda b, p: (b, 0, col0 + p))
    in_specs = [tok(tc, 0), tok(tc, N_PAIRS), tok(tc, 2 * N_PAIRS), tok(tc, 0),
                pl.BlockSpec((None, tc, BA_W), lambda b, p: (b, 0, 0)),
                tok(tl, 0), tok(tl, N_PAIRS), tok(tl, 2 * N_PAIRS), tok(tl, 0),
                pl.BlockSpec((None, tl, BA_W), lambda b, p: (b, 0, 0)),
                pl.BlockSpec((CONV_W, LANES), lambda b, p: (0, p)),
                pl.BlockSpec((CONV_W, LANES), lambda b, p: (0, N_PAIRS + p)),
                pl.BlockSpec((CONV_W, LANES), lambda b, p: (0, 2 * N_PAIRS + p)),
                cm(SUBLANES, LANES),
                pl.BlockSpec((None, BA_W, 4 * LANES), lambda b, p: (p, 0, 0)),
                cm(2, CH, LANES), cm(2, CH, LANES), cm(2, 6, CH, LANES), cm(2, 1, LANES), cm(LANES, LANES),
                cm(2, ROWT, ROWT), cm(ROWT, ROWT)]
    pad_rows = tall + 3 * GAP
    scratch = [pltpu.VMEM((pad_rows, LANES), F32)] * 3 + [pltpu.VMEM((tall, LANES), F32)] * 3 + [
        pltpu.VMEM((2, 3, tall, LANES), F32),
        pltpu.VMEM((2, _group_size(n_all), 2, 2 * CH, LANES), F32),
        pltpu.VMEM((2, _group_size(n_all), 2, 2 * LANES + 2 * CH + SUBLANES, LANES), F32),
        pltpu.VMEM((tall, LANES), F32), pltpu.VMEM((tall, LANES), F32)]
    return pl.pallas_call(
        _gdn_kernel,
        grid=(bsz, N_PAIRS),
        in_specs=in_specs,
        out_specs=[pl.BlockSpec((None, tc, LANES), lambda b, p: (b, 0, p)),
                   pl.BlockSpec((None, tl, LANES), lambda b, p: (b, 0, p))],
        out_shape=[jax.ShapeDtypeStruct((bsz, tc, GDN_WIDTH), F32),
                   jax.ShapeDtypeStruct((bsz, tl, GDN_WIDTH), F32)],
        scratch_shapes=scratch,
        compiler_params=pltpu.CompilerParams(dimension_semantics=("arbitrary", "arbitrary"),
                                             vmem_limit_bytes=VMEM_LIMIT),
        name="gdn_mixer",
    )(qkv_c, qkv_c, qkv_c, z_c, ba_c, qkv_l, qkv_l, qkv_l, z_l, ba_l,
      conv_w, conv_w, conv_w, aux, e,
      consts["incl"], consts["strict"], consts["lvl"], consts["hm"], consts["bdm"],
      consts["tri_t"], consts["ones_t"])


def _rwkv_kernel(rc_ref, kc_ref, vc_ref, wdc_ref, adc_ref, gdc_ref,
                 rl_ref, kl_ref, vl_ref, wdl_ref, adl_ref, gdl_ref,
                 mur_ref, muk_ref, muv_ref, mux_ref, wup_ref, aup_ref, gup_ref, vec_ref,
                 incl_ref, strict_ref, lvl_ref, hm_ref, bdm_ref, trit_ref,
                 oc_ref, ol_ref,
                 vs_ref, gs_ref, bvs_ref, dir_ref, ls_ref, p1_ref, yf_ref, yb_ref):
    tc, tl = rc_ref.shape[0], rl_ref.shape[0]
    tall = tc + tl
    n_ctx, n_all = tc // CH, tall // CH
    grp = _group_size(n_all)
    m0, m1 = hm_ref[0].astype(BF16), hm_ref[1].astype(BF16)
    m0w = jnp.concatenate([m0, m0], axis=1)
    m1w = jnp.concatenate([m1, m1], axis=1)
    bdm = bdm_ref[...]

    kkw, kaw, rkw = vec_ref[0:1, :], vec_ref[1:2, :], vec_ref[2:3, :]
    for (_, uo, n) in _row_tiles(tc, tl):
        ctx_tile = uo < tc
        lo = uo if ctx_tile else uo - tc
        t_len = tc if ctx_tile else tl
        rowi = lax.broadcasted_iota(jnp.int32, (n, LANES), 0)

        def lerp(src_c, src_l, mu):
            src = src_c if ctx_tile else src_l
            x = src[lo:lo + n, :]
            prev = src[lo - 1:lo, :] if lo > 0 else jnp.zeros((1, LANES), F32)
            nxt = src[lo + n:lo + n + 1, :] if lo + n < t_len else jnp.zeros((1, LANES), F32)
            xm = jnp.where(rowi == 0, prev, pltpu.roll(x, 1, 0))
            xp = jnp.where(rowi == n - 1, nxt, pltpu.roll(x, n - 1, 0))
            return x + mu * (0.5 * (xm + xp) - x)
        r = lerp(rc_ref, rl_ref, mur_ref[...])
        k = lerp(kc_ref, kl_ref, muk_ref[...])
        v = lerp(vc_ref, vl_ref, muv_ref[...])
        wd = lerp(wdc_ref, wdl_ref, mux_ref[0:1, :])
        ad = lerp(adc_ref, adl_ref, mux_ref[1:2, :])
        gd = lerp(gdc_ref, gdl_ref, mux_ref[2:3, :])
        tw = jnp.tanh(wd)
        kkv = k * kkw
        kk = kkv * lax.rsqrt(_mm(kkv * kkv, bdm) + 1e-6)
        ksum = jnp.zeros_like(k)
        for d in range(2):
            lw = -0.6065306597126334 * _sigmoid(vec_ref[3 + d:4 + d, :] + _mm(tw, wup_ref[d]))
            a = _sigmoid(vec_ref[5 + d:6 + d, :] + _mm(ad, aup_ref[d]))
            kdir = k * (1.0 + (a - 1.0) * kaw)
            ksum = ksum + kdir
            cum = _mm_sel_l(trit_ref[d, 0:n, 0:n], lw)
            einv = jnp.exp(-cum)
            dir_ref[d, 0, uo:uo + n, :] = cum
            dir_ref[d, 1, uo:uo + n, :] = kk * jnp.exp(cum - lw)
            dir_ref[d, 2, uo:uo + n, :] = r * jnp.exp(cum)
            dir_ref[d, 3, uo:uo + n, :] = kk * a * einv
            dir_ref[d, 4, uo:uo + n, :] = kdir * einv
        bonus = _mm_sel_r(r * ksum * rkw, bdm)
        vs_ref[uo:uo + n, :] = v
        gs_ref[uo:uo + n, :] = _mm(_sigmoid(gd), gup_ref[...])
        bvs_ref[uo:uo + n, :] = bonus * v

    n_grp = n_all // grp
    p1_ref[0] = jnp.zeros(p1_ref.shape[1:], F32)
    ls_ref[1] = jnp.zeros(ls_ref.shape[1:], F32)
    pd = [(i, d) for i in range(grp) for d in range(2)]

    def chunk_rows(gi, i, d):
        g = jnp.clip(gi, 0, n_grp - 1)
        c = _chunk_order(g * grp + i, n_ctx, n_all, d)
        return pl.ds(pl.multiple_of(c * CH, CH), CH)

    def stage_a(gi, slot):
        def finish(pi, pdir, m, vbd):
            ls_ref[slot, pi, pdir, 0:CH, :] = m[:CH, :LANES] * strict_ref[pdir]
            ls_ref[slot, pi, pdir, CH:2 * CH, :] = _mm(m[:CH, LANES:] * strict_ref[pdir], vbd)
            ls_ref[slot, pi, pdir, 2 * CH:3 * CH, :] = m[CH:, :LANES] * incl_ref[pdir]
            ls_ref[slot, pi, pdir, 3 * CH:4 * CH, :] = _mm(m[CH:, LANES:] * incl_ref[pdir], vbd)
        pending = None
        for i, d in pd:
            rows = chunk_rows(gi, i, d)
            kkq, rq, binv, kinv = (dir_ref[d, j, rows, :] for j in range(1, 5))
            lhs = jnp.concatenate([kkq, rq], axis=0)
            rhs = jnp.concatenate([_bd(binv, m0, m1), _bd(kinv, m0, m1)], axis=0)
            cur = (i, d, _mm_nt(lhs, rhs), _bd(vs_ref[rows, :], m0, m1))
            if pending is not None:
                finish(*pending)
            pending = cur
            yield
        finish(*pending)
        yield

    def stage_b(gi, slot, p_slot, validf):
        mats = [ls_ref[slot, i, d, 0:CH, :] for i, d in pd]
        es = []
        yield from _inv_levels(mats, [d for _, d in pd], lvl_ref, m0, m1, es)
        sols = []
        for e, (i, d) in zip(es, pd):
            rows = chunk_rows(gi, i, d)
            rhs = jnp.concatenate([dir_ref[d, 1, rows, :], ls_ref[slot, i, d, CH:2 * CH, :]], axis=1)
            sols.append(rhs + _mm(e, _bd(rhs, m0w, m1w)))
        yield
        for sol, (i, d) in zip(sols, pd):
            rows = chunk_rows(gi, i, d)
            cum = dir_ref[d, 0, rows, :]
            etot = jnp.exp(cum[CH - 1:CH, :] if d == 0 else cum[0:1, :])
            bdec, kdec = dir_ref[d, 3, rows, :] * etot, dir_ref[d, 4, rows, :] * etot
            ar = _mm(ls_ref[slot, i, d, 2 * CH:3 * CH, :], _bd(sol, m0w, m1w))
            pmat = _mm_tn(sol[:, :LANES], bdec)
            nmat = _mm_tn(jnp.concatenate([-sol[:, LANES:], vs_ref[rows, :]], axis=0),
                          jnp.concatenate([bdec, kdec], axis=0))
            p1_ref[p_slot, i, d, 0:LANES, :] = pmat * bdm
            p1_ref[p_slot, i, d, LANES:2 * LANES, :] = nmat * (bdm * validf)
            p1_ref[p_slot, i, d, 2 * LANES:2 * LANES + CH, :] = dir_ref[d, 2, rows, :] - ar[:, :LANES]
            p1_ref[p_slot, i, d, 2 * LANES + CH:2 * LANES + 2 * CH, :] = (
                ls_ref[slot, i, d, 3 * CH:4 * CH, :] - ar[:, LANES:])
            p1_ref[p_slot, i, d, 2 * LANES + 2 * CH:2 * LANES + 2 * CH + SUBLANES, :] = jnp.broadcast_to(
                etot, (SUBLANES, LANES))
        yield

    def stage_c(gi, slot, st):
        for i, d in pd:
            rows = chunk_rows(gi, i, d)
            pmat = p1_ref[slot, i, d, 0:LANES, :]
            nmat = p1_ref[slot, i, d, LANES:2 * LANES, :]
            rmat = p1_ref[slot, i, d, 2 * LANES:2 * LANES + CH, :]
            ymat = p1_ref[slot, i, d, 2 * LANES + CH:2 * LANES + 2 * CH, :]
            etot = p1_ref[slot, i, d, 2 * LANES + 2 * CH:2 * LANES + 2 * CH + 1, :]
            yref = yf_ref if d == 0 else yb_ref
            yref[rows, :] = _mm_nt(rmat, st[d]) + ymat
            st[d] = st[d] * etot - _mm(st[d], pmat) + nmat
            yield

    def step(t, st):
        par = lax.rem(t, 2)
        st = list(st)
        _round_robin([stage_b(t - 1, 1 - par, 1 - par, jnp.where(t >= 1, 1.0, 0.0).astype(F32)),
                      stage_c(t - 2, par, st),
                      stage_a(t, par)])
        return tuple(st)
    z = jnp.zeros((LANES, LANES), F32)
    lax.fori_loop(0, n_grp + 2, step, (z, z))

    gnw, gnb = vec_ref[7:8, :], vec_ref[8:9, :]
    for (po, uo, n) in _row_tiles(tc, tl):
        y = yf_ref[uo:uo + n, :] + yb_ref[uo:uo + n, :]
        mean = _mm_sel_r(y, bdm) * (1.0 / HD)
        yc = y - mean
        var = _mm(yc * yc, bdm) * (1.0 / HD)
        out = (yc * lax.rsqrt(var + RWKV_GN_EPS) * gnw + gnb + bvs_ref[uo:uo + n, :]) * gs_ref[uo:uo + n, :]
        if uo < tc:
            oc_ref[uo:uo + n, :] = out
        else:
            ol_ref[uo - tc:uo - tc + n, :] = out


def _rwkv_mixer(rw_c, rw_l, mu, w0, w_up, a0, a_up, g_up, k_k, k_a, r_k, gn_w, gn_b, consts):
    bsz, tc, _ = rw_c.shape
    tl = rw_l.shape[1]
    tall = tc + tl
    n_all = tall // CH
    W = RWKV_WIDTH
    nb = W // LANES
    wup = jnp.zeros((2, LANES, W), F32)
    aup = jnp.zeros((2, LANES, W), F32)
    for d in range(2):
        wup = wup.at[d, d * RWKV_RANK:(d + 1) * RWKV_RANK, :].set(w_up[d])
        aup = aup.at[d, d * RWKV_RANK:(d + 1) * RWKV_RANK, :].set(a_up[d])
    vec = jnp.zeros((16, W), F32)
    vec = vec.at[0].set(k_k).at[1].set(k_a).at[2].set(r_k.reshape(-1))
    vec = vec.at[3].set(w0[0]).at[4].set(w0[1]).at[5].set(a0[0]).at[6].set(a0[1])
    vec = vec.at[7].set(gn_w).at[8].set(gn_b)
    mu_rkv = mu[:3 * W].reshape(3, 1, W)
    mux = jnp.zeros((SUBLANES, LANES), F32).at[0:3, :].set(mu[3 * W:].reshape(3, LANES))
    cm = lambda *blk: pl.BlockSpec(blk, lambda b, p: (0,) * len(blk))

    def tok(t, col0, per_pair=True):
        if per_pair:
            return pl.BlockSpec((None, t, LANES), lambda b, p: (b, 0, col0 + p))
        return pl.BlockSpec((None, t, LANES), lambda b, p: (b, 0, col0))
    stream = lambda t: [tok(t, 0), tok(t, nb), tok(t, 2 * nb), tok(t, 3 * nb, False),
                        tok(t, 3 * nb + 1, False), tok(t, 3 * nb + 2, False)]
    in_specs = stream(tc) + stream(tl) + [
        pl.BlockSpec((None, 1, LANES), lambda b, p: (0, 0, p)),
        pl.BlockSpec((None, 1, LANES), lambda b, p: (1, 0, p)),
        pl.BlockSpec((None, 1, LANES), lambda b, p: (2, 0, p)),
        cm(SUBLANES, LANES),
        pl.BlockSpec((2, LANES, LANES), lambda b, p: (0, 0, p)),
        pl.BlockSpec((2, LANES, LANES), lambda b, p: (0, 0, p)),
        pl.BlockSpec((RWKV_G_RANK, LANES), lambda b, p: (0, p)),
        pl.BlockSpec((16, LANES), lambda b, p: (0, p)),
        cm(2, CH, LANES), cm(2, CH, LANES), cm(2, 6, CH, LANES), cm(2, 1, LANES), cm(LANES, LANES),
        cm(2, ROWT, ROWT)]
    scratch = [pltpu.VMEM((tall, LANES), F32)] * 3 + [
        pltpu.VMEM((2, 5, tall, LANES), F32),
        pltpu.VMEM((2, _group_size(n_all), 2, 4 * CH, LANES), F32),
        pltpu.VMEM((2, _group_size(n_all), 2, 2 * LANES + 2 * CH + SUBLANES, LANES), F32),
        pltpu.VMEM((tall, LANES), F32), pltpu.VMEM((tall, LANES), F32)]
    return pl.pallas_call(
        _rwkv_kernel,
        grid=(bsz, N_PAIRS),
        in_specs=in_specs,
        out_specs=[pl.BlockSpec((None, tc, LANES), lambda b, p: (b, 0, p)),
                   pl.BlockSpec((None, tl, LANES), lambda b, p: (b, 0, p))],
        out_shape=[jax.ShapeDtypeStruct((bsz, tc, W), F32),
                   jax.ShapeDtypeStruct((bsz, tl, W), F32)],
        scratch_shapes=scratch,
        compiler_params=pltpu.CompilerParams(dimension_semantics=("arbitrary", "arbitrary"),
                                             vmem_limit_bytes=VMEM_LIMIT),
        name="rwkv_mixer",
    )(*([rw_c] * 6), *([rw_l] * 6), mu_rkv, mu_rkv, mu_rkv, mux, wup, aup, g_up, vec,
      consts["incl"], consts["strict"], consts["lvl"], consts["hm"], consts["bdm"], consts["tri_t"])


def _lru_kernel(xc_ref, gc_ref, xl_ref, gl_ref, cw_ref, vec_ref, wcat_ref, bcat_ref,
                oc_ref, ol_ref, xp_ref, ab_ref, hf_ref, hb_ref):
    tc, tl = xc_ref.shape[0], xl_ref.shape[0]
    tall = tc + tl
    rows = tl // GRID_W
    W = xc_ref.shape[1]
    z = jnp.zeros((GAP, W), F32)
    xp_ref[0:GAP, :] = z
    xp_ref[GAP:GAP + tc, :] = xc_ref[...]
    xp_ref[GAP + tc:2 * GAP + tc, :] = z
    base = 2 * GAP + tc
    for c in range(GRID_W):
        xp_ref[base + c * rows:base + (c + 1) * rows, :] = xl_ref[pl.ds(c, rows, stride=GRID_W), :]
    xp_ref[base + tl:base + tl + GAP, :] = z
    cb = vec_ref[0:1, :]
    for (po, uo, n) in _row_tiles(tc, tl):
        acc = cb + cw_ref[0:1, :] * xp_ref[po - 2:po - 2 + n, :]
        for tap in range(1, CONV_W):
            acc = acc + cw_ref[tap:tap + 1, :] * xp_ref[po - 2 + tap:po - 2 + tap + n, :]
        gates = _mm(acc, wcat_ref[...]) + bcat_ref[...]
        for d in range(2):
            rg = _sigmoid(gates[:, (2 * d) * W:(2 * d + 1) * W])
            ig = _sigmoid(gates[:, (2 * d + 1) * W:(2 * d + 2) * W])
            log_a = -LRU_C * rg * _softplus(-vec_ref[1 + d:2 + d, :])
            a = jnp.exp(log_a)
            mult = jnp.sqrt(-jnp.tanh(log_a) * (1.0 + a * a))
            ab_ref[d, 0, uo:uo + n, :] = a
            ab_ref[d, 1, uo:uo + n, :] = mult * (ig * acc)

    sub = lax.broadcasted_iota(jnp.int32, (SUBLANES, W), 0)
    n_tiles_c, n_tiles = tc // SUBLANES, tall // SUBLANES

    def tile_scan(a, b, d):
        for sh in (1, 2, 4):
            if d == 0:
                ok = sub >= sh
                a_s = jnp.where(ok, pltpu.roll(a, sh, 0), 1.0)
                b_s = jnp.where(ok, pltpu.roll(b, sh, 0), 0.0)
            else:
                ok = sub < SUBLANES - sh
                a_s = jnp.where(ok, pltpu.roll(a, SUBLANES - sh, 0), 1.0)
                b_s = jnp.where(ok, pltpu.roll(b, SUBLANES - sh, 0), 0.0)
            b = b + a * b_s
            a = a * a_s
        return a, b

    def scan_body(s, carry):
        hf, hb = carry
        r0 = pl.multiple_of(s * SUBLANES, SUBLANES)
        a, b = tile_scan(ab_ref[0, 0, pl.ds(r0, SUBLANES), :], ab_ref[0, 1, pl.ds(r0, SUBLANES), :], 0)
        h = b + a * hf
        hf_ref[pl.ds(r0, SUBLANES), :] = h
        hf = jnp.broadcast_to(h[SUBLANES - 1:SUBLANES, :], (SUBLANES, W))
        tb = jnp.where(s < n_tiles_c, n_tiles_c - 1 - s, n_tiles + n_tiles_c - 1 - s)
        r1 = pl.multiple_of(tb * SUBLANES, SUBLANES)
        a, b = tile_scan(ab_ref[1, 0, pl.ds(r1, SUBLANES), :], ab_ref[1, 1, pl.ds(r1, SUBLANES), :], 1)
        h = b + a * hb
        hb_ref[pl.ds(r1, SUBLANES), :] = h
        hb = jnp.broadcast_to(h[0:1, :], (SUBLANES, W))
        return hf, hb
    z8 = jnp.zeros((SUBLANES, W), F32)
    lax.fori_loop(0, n_tiles, scan_body, (z8, z8))

    oc_ref[...] = (hf_ref[0:tc, :] + hb_ref[0:tc, :]) * _gelu_tanh(gc_ref[...])
    for c in range(GRID_W):
        h = hf_ref[tc + c * rows:tc + (c + 1) * rows, :] + hb_ref[tc + c * rows:tc + (c + 1) * rows, :]
        ol_ref[pl.ds(c, rows, stride=GRID_W), :] = h * _gelu_tanh(gl_ref[pl.ds(c, rows, stride=GRID_W), :])


def _lru_mixer(x_c, g_c, x_l, g_l, conv_w, conv_b, w_a, b_a, w_x, b_x, lam):
    bsz, tc, W = x_c.shape
    tl = x_l.shape[1]
    tall = tc + tl
    nh = W // LANES
    bph = LANES // LRU_BW
    wcat = jnp.zeros((nh, LANES, 4 * LANES), F32)
    bcat = jnp.zeros((nh, 1, 4 * LANES), F32)
    for hh in range(nh):
        for d in range(2):
            for gi, (wsrc, bsrc) in enumerate(((w_a, b_a), (w_x, b_x))):
                c0 = (2 * d + gi) * LANES
                bcat = bcat.at[hh, 0, c0:c0 + LANES].set(bsrc[d, hh * LANES:(hh + 1) * LANES])
                for n in range(bph):
                    wcat = wcat.at[hh, n * LRU_BW:(n + 1) * LRU_BW,
                                   c0 + n * LRU_BW:c0 + (n + 1) * LRU_BW].set(wsrc[d, hh * bph + n])
    vec = jnp.zeros((SUBLANES, W), F32).at[0].set(conv_b).at[1].set(lam[0]).at[2].set(lam[1])
    tok = lambda t: pl.BlockSpec((None, t, LANES), lambda b, h: (b, 0, h))
    return pl.pallas_call(
        _lru_kernel,
        grid=(bsz, nh),
        in_specs=[tok(tc), tok(tc), tok(tl), tok(tl),
                  pl.BlockSpec((CONV_W, LANES), lambda b, h: (0, h)),
                  pl.BlockSpec((SUBLANES, LANES), lambda b, h: (0, h)),
                  pl.BlockSpec((None, LANES, 4 * LANES), lambda b, h: (h, 0, 0)),
                  pl.BlockSpec((None, 1, 4 * LANES), lambda b, h: (h, 0, 0))],
        out_specs=[tok(tc), tok(tl)],
        out_shape=[jax.ShapeDtypeStruct((bsz, tc, W), F32), jax.ShapeDtypeStruct((bsz, tl, W), F32)],
        scratch_shapes=[pltpu.VMEM((tall + 3 * GAP, LANES), F32),
                        pltpu.VMEM((2, 2, tall, LANES), F32),
                        pltpu.VMEM((tall, LANES), F32), pltpu.VMEM((tall, LANES), F32)],
        compiler_params=pltpu.CompilerParams(dimension_semantics=("arbitrary", "arbitrary"),
                                             vmem_limit_bytes=VMEM_LIMIT),
        name="lru_mixer",
    )(x_c, g_c, x_l, g_l, conv_w, vec, wcat.astype(BF16), bcat)


def _finish_kernel(x_ref, gdn_ref, lru_ref, rwk_ref, m2_ref, m3_ref, m4_ref, m5_ref, nrm_ref,
                   wo_ref, up_ref, dn_ref, o_ref):
    x = x_ref[...]
    o = (jnp.dot(gdn_ref[...].astype(BF16), wo_ref[0:GDN_WIDTH, :], preferred_element_type=F32)
         + jnp.dot(lru_ref[...].astype(BF16), wo_ref[GDN_WIDTH:GDN_WIDTH + LRU_WIDTH, :],
                   preferred_element_type=F32)
         + jnp.dot(rwk_ref[...].astype(BF16), wo_ref[GDN_WIDTH + LRU_WIDTH:, :], preferred_element_type=F32))
    x = x + m2_ref[...] * _rms(o, nrm_ref[0:1, :])
    h = (_rms(x, nrm_ref[1:2, :]) * (1.0 + m4_ref[...]) + m3_ref[...]).astype(BF16)
    f = jnp.zeros_like(x)
    fc = 1024
    for j in range(D_FF // fc):
        a = jnp.maximum(jnp.dot(h, up_ref[:, j * fc:(j + 1) * fc], preferred_element_type=F32), 0.0)
        f = f + jnp.dot((a * a).astype(BF16), dn_ref[j * fc:(j + 1) * fc, :], preferred_element_type=F32)
    o_ref[...] = x + m5_ref[...] * _rms(f, nrm_ref[2:3, :])


def _finish(x2, gdn, lru, rwk, mod_rows, rows_per_mod, norms, wo_bf, up_bf, dn_bf):
    n = x2.shape[0]
    tm = DENSE_TM
    tiles_per_mod = rows_per_mod // tm
    modspec = lambda k: pl.BlockSpec((None, 1, D_MODEL), lambda i: (6 * (i // tiles_per_mod) + k, 0, 0))
    cm = lambda *blk: pl.BlockSpec(blk, lambda i: (0,) * len(blk), pipeline_mode=pl.Buffered(1))
    return pl.pallas_call(
        _finish_kernel,
        grid=(n // tm,),
        in_specs=[pl.BlockSpec((tm, D_MODEL), lambda i: (i, 0)),
                  pl.BlockSpec((tm, GDN_WIDTH), lambda i: (i, 0)),
                  pl.BlockSpec((tm, LRU_WIDTH), lambda i: (i, 0)),
                  pl.BlockSpec((tm, RWKV_WIDTH), lambda i: (i, 0)),
                  modspec(2), modspec(3), modspec(4), modspec(5),
                  cm(SUBLANES, D_MODEL), cm(D_MODEL, D_MODEL), cm(D_MODEL, D_FF), cm(D_FF, D_MODEL)],
        out_specs=pl.BlockSpec((tm, D_MODEL), lambda i: (i, 0)),
        out_shape=jax.ShapeDtypeStruct((n, D_MODEL), F32),
        compiler_params=pltpu.CompilerParams(dimension_semantics=("arbitrary",),
                                             vmem_limit_bytes=VMEM_LIMIT),
        name="finish",
    )(x2, gdn, lru, rwk, mod_rows, mod_rows, mod_rows, mod_rows, norms, wo_bf, up_bf, dn_bf)


def _arrange_w_in(w):
    s = np.cumsum([0, 3 * GDN_WIDTH, GDN_WIDTH, 2 * GDN_HEADS, 2 * GDN_HEADS, LRU_WIDTH, LRU_WIDTH, RWKV_IN])
    pad = jnp.zeros((w.shape[0], BA_W - 4 * GDN_HEADS), w.dtype)
    return jnp.concatenate([w[:, s[0]:s[2]], w[:, s[2]:s[4]], pad, w[:, s[4]:]], axis=1)


def kernel(x, c, ctx, c_ctx, ada_w, ada_b, norm_mix_pre, norm_mix_post, norm_ffn_pre, norm_ffn_post, w_in, gdn_conv, gdn_a_log, gdn_dt_bias, gdn_norm, lru_conv, lru_conv_b, lru_wa, lru_ba, lru_wx, lru_bx, lru_lambda, rwkv_mu, rwkv_w0, rwkv_w_up, rwkv_a0, rwkv_a_up, rwkv_g_up, rwkv_k_k, rwkv_k_a, rwkv_r_k, rwkv_gn_w, rwkv_gn_b, w_out, ffn_up, ffn_down):
    bsz, tl, _ = x.shape
    tc = ctx.shape[1]
    depth = w_in.shape[0]
    consts = _device_consts()

    cvec = jnp.zeros((16, D_MODEL), F32).at[0:bsz].set(c).at[bsz].set(c_ctx)
    mods = _ada_mod(cvec, ada_w, ada_b).reshape(depth, 16, 6, D_MODEL)

    xl = x.reshape(bsz * tl, D_MODEL)
    xc = ctx.reshape(bsz * tc, D_MODEL)
    for i in range(depth):
        mod_l = mods[i, 0:bsz].reshape(bsz * 6, 1, D_MODEL)
        mod_c = mods[i, bsz:bsz + 1].reshape(6, 1, D_MODEL)
        w_bf = _arrange_w_in(w_in[i]).astype(BF16)
        pl_ = _inproj(xl, mod_l, tl, norm_mix_pre[i], w_bf)
        pc_ = _inproj(xc, mod_c, bsz * tc, norm_mix_pre[i], w_bf)
        r3 = lambda a, t: a.reshape(bsz, t, a.shape[-1])
        qkv_l, z_l, ba_l, lx_l, lg_l, rw_l = (r3(a, tl) for a in pl_)
        qkv_c, z_c, ba_c, lx_c, lg_c, rw_c = (r3(a, tc) for a in pc_)

        gdn_c, gdn_l = _gdn_mixer(qkv_c, z_c, ba_c, qkv_l, z_l, ba_l, gdn_conv[i], gdn_a_log[i],
                                  gdn_dt_bias[i], gdn_norm[i], consts)
        lru_c, lru_l = _lru_mixer(lx_c, lg_c, lx_l, lg_l, lru_conv[i], lru_conv_b[i], lru_wa[i], lru_ba[i],
                                  lru_wx[i], lru_bx[i], lru_lambda[i])
        rwk_c, rwk_l = _rwkv_mixer(rw_c, rw_l, rwkv_mu[i], rwkv_w0[i], rwkv_w_up[i], rwkv_a0[i],
                                   rwkv_a_up[i], rwkv_g_up[i], rwkv_k_k[i], rwkv_k_a[i], rwkv_r_k[i],
                                   rwkv_gn_w[i], rwkv_gn_b[i], consts)

        norms = jnp.zeros((SUBLANES, D_MODEL), F32).at[0].set(norm_mix_post[i]).at[1].set(
            norm_ffn_pre[i]).at[2].set(norm_ffn_post[i])
        wo_bf, up_bf, dn_bf = w_out[i].astype(BF16), ffn_up[i].astype(BF16), ffn_down[i].astype(BF16)
        f2 = lambda a: a.reshape(-1, a.shape[-1])
        xl = _finish(xl, f2(gdn_l), f2(lru_l), f2(rwk_l), mod_l, tl, norms, wo_bf, up_bf, dn_bf)
        if i < depth - 1:
            xc = _finish(xc, f2(gdn_c), f2(lru_c), f2(rwk_c), mod_c, bsz * tc, norms, wo_bf, up_bf, dn_bf)
    return xl.reshape(bsz, tl, D_MODEL)
```

```python
import functools

import numpy as np
import jax
import jax.numpy as jnp
from jax import lax
from jax.experimental import pallas as pl
from jax.experimental.pallas import tpu as pltpu

F32 = jnp.float32
BF16 = jnp.bfloat16

LANES = 128
SUBLANES = 8
VMEM_LIMIT = 56 * 1024 * 1024

D_MODEL = 1024
DEPTH = 2
GRID_W = 64
CONV_W = 4
EPS = 1e-6
D_FF = 4 * D_MODEL
HD = 64
GDN_WIDTH = 3 * D_MODEL // 8
GDN_HEADS = GDN_WIDTH // HD
LRU_WIDTH = D_MODEL // 4
LRU_BLOCKS = 4
LRU_BW = LRU_WIDTH // LRU_BLOCKS
LRU_C = 8.0
RWKV_WIDTH = D_MODEL - GDN_WIDTH - LRU_WIDTH
RWKV_HEADS = RWKV_WIDTH // HD
RWKV_RANK = 64
RWKV_G_RANK = 128
RWKV_GN_EPS = 6.4e-4
RWKV_IN = 3 * RWKV_WIDTH + 2 * RWKV_RANK + 2 * RWKV_RANK + RWKV_G_RANK
N_PAIRS = GDN_HEADS // 2
BA_W = LANES
P_OFF = np.cumsum([0, 3 * GDN_WIDTH, GDN_WIDTH, BA_W, LRU_WIDTH, LRU_WIDTH, RWKV_IN])
D_INP = int(P_OFF[-1])

CH = 64
ROWT = 256
DENSE_TM = 512
GAP = SUBLANES


def _mm(a, b):
    return jnp.dot(a.astype(BF16), b.astype(BF16), preferred_element_type=F32)


def _mm_nt(a, b):
    return lax.dot_general(a.astype(BF16), b.astype(BF16), (((1,), (1,)), ((), ())),
                           preferred_element_type=F32)


def _mm_tn(a, b):
    return lax.dot_general(a.astype(BF16), b.astype(BF16), (((0,), (0,)), ((), ())),
                           preferred_element_type=F32)


def _split2(x):
    hi = x.astype(BF16)
    lo = (x - hi.astype(F32)).astype(BF16)
    return hi, lo


def _mm_sel_l(m01, x):
    mb = m01.astype(BF16)
    h, l = _split2(x)
    d = functools.partial(jnp.dot, preferred_element_type=F32)
    return d(mb, h) + d(mb, l)


def _mm_sel_r(x, m01):
    mb = m01.astype(BF16)
    h, l = _split2(x)
    d = functools.partial(jnp.dot, preferred_element_type=F32)
    return d(h, mb) + d(l, mb)


def _sigmoid(x):
    return 0.5 * jnp.tanh(0.5 * x) + 0.5


def _silu(x):
    return x * _sigmoid(x)


def _softplus(x):
    return jnp.maximum(x, 0.0) + jnp.log(1.0 + jnp.exp(-jnp.abs(x)))


def _gelu_tanh(x):
    return 0.5 * x * (1.0 + jnp.tanh(0.7978845608028654 * (x + 0.044715 * (x * x * x))))


def _bd(y, m0, m1):
    yb = y.astype(BF16)
    return jnp.concatenate([yb * m0, yb * m1], axis=0)


def _pair_consts():
    i = np.arange(CH)[:, None]
    j = (np.arange(LANES) % HD)[None, :]
    incl = np.stack([i >= j, i <= j]).astype(np.float32)
    strict = np.stack([i > j, i < j]).astype(np.float32)
    lvls = []
    for d in range(2):
        per = []
        for m in (1, 2, 4, 8, 16, 32):
            same = (i // (2 * m)) == (j // (2 * m))
            lo_i, lo_j = (i % (2 * m)) < m, (j % (2 * m)) < m
            off = same & (~lo_i) & lo_j if d == 0 else same & lo_i & (~lo_j)
            per.append(off)
        lvls.append(np.stack(per))
    lvl = np.stack(lvls).astype(np.float32)
    lane = np.arange(LANES)
    hm = np.stack([lane < HD, lane >= HD]).astype(np.float32)[:, None, :]
    bdm = ((np.arange(LANES)[:, None] // HD) == (lane[None, :] // HD)).astype(np.float32)
    r = np.arange(ROWT)[:, None]
    c = np.arange(ROWT)[None, :]
    same = (r // CH) == (c // CH)
    tri_t = np.stack([same & (r >= c), same & (r <= c)]).astype(np.float32)
    ones_t = same.astype(np.float32)
    return dict(incl=incl, strict=strict, lvl=lvl, hm=hm, bdm=bdm, tri_t=tri_t, ones_t=ones_t)


def _device_consts():
    out = {k: jnp.asarray(v) for k, v in _pair_consts().items()}
    for k in ("tri_t", "ones_t"):
        out[k] = out[k].astype(BF16)
    return out


def _inv_levels(mats, dirs, lvl_ref, m0, m1, out):
    es = [-(a * lvl_ref[d, 0]) for a, d in zip(mats, dirs)]
    for k in range(1, 6):
        offs = [a * lvl_ref[d, k] for a, d in zip(mats, dirs)]
        xs = [off + p for off, p in zip(offs, _mm_pairs(es, [_bd(off, m0, m1) for off in offs]))]
        yield
        es = [e - x - p for e, x, p in zip(es, xs, _mm_pairs(xs, [_bd(e, m0, m1) for e in es]))]
        yield
    out.extend(es)


def _mm_pairs(lhs, rhs):
    out = []
    for j in range(0, len(lhs) - 1, 2):
        r = jnp.dot(jnp.concatenate([lhs[j], lhs[j + 1]], axis=0).astype(BF16),
                    jnp.concatenate([rhs[j], rhs[j + 1]], axis=1), preferred_element_type=F32)
        out += [r[:CH, :LANES], r[CH:, LANES:]]
    if len(lhs) % 2:
        out.append(_mm(lhs[-1], rhs[-1]))
    return out


def _rows(vectors, n_rows):
    m = jnp.stack([v.astype(F32) for v in vectors])
    return jnp.pad(m, ((0, n_rows - m.shape[0]), (0, 0)))


def _round_robin(gens):
    gens = list(gens)
    while gens:
        alive = []
        for g in gens:
            try:
                next(g)
                alive.append(g)
            except StopIteration:
                pass
        gens = alive


def _group_size(n_all):
    for g in (6, 4, 3, 2):
        if n_all % g == 0:
            return g
    return 1


def _chunk_order(s, n_ctx, n_all, d):
    if d == 0:
        return s
    if isinstance(s, int):
        return n_ctx - 1 - s if s < n_ctx else n_all + n_ctx - 1 - s
    return jnp.where(s < n_ctx, n_ctx - 1 - s, n_all + n_ctx - 1 - s)


def _chunk_rows(gi, grp, i, d, n_ctx, n_all):
    c = _chunk_order(gi * grp + i, n_ctx, n_all, d)
    return pl.ds(c * CH if isinstance(c, int) else pl.multiple_of(c * CH, CH), CH)


def _run_pipeline(n_grp, stage_a, stage_b, stage_c, st, b_before_a=True):
    def step(t, st, do_a, do_b, do_c):
        par = t % 2 if isinstance(t, int) else lax.rem(t, 2)
        st = list(st)
        gens = []
        if do_c:
            gens.append(stage_c(t - 2, par, st))
        ab = ([stage_b(t - 1, 1 - par)] if do_b else []) + ([stage_a(t, par)] if do_a else [])
        _round_robin(gens + (ab if b_before_a else ab[::-1]))
        return tuple(st)
    assert n_grp >= 2
    st = step(0, st, True, False, False)
    st = step(1, st, True, True, False)
    st = lax.fori_loop(2, n_grp, lambda t, s: step(t, s, True, True, True), st)
    st = step(n_grp, st, False, True, True)
    return step(n_grp + 1, st, False, False, True)


def _fill_padded(dst_ref, src_c_ref, src_l_ref, tc, tl):
    w = dst_ref.shape[1]
    z = jnp.zeros((GAP, w), F32)
    dst_ref[0:GAP, :] = z
    dst_ref[GAP:GAP + tc, :] = src_c_ref[...]
    dst_ref[GAP + tc:2 * GAP + tc, :] = z
    dst_ref[2 * GAP + tc:2 * GAP + tc + tl, :] = src_l_ref[...]
    dst_ref[2 * GAP + tc + tl:3 * GAP + tc + tl, :] = z


def _row_tiles(tc, tl):
    out = []
    for base_p, base_u, n in ((GAP, 0, tc), (2 * GAP + tc, tc, tl)):
        for t0 in range(0, n, ROWT):
            out.append((base_p + t0, base_u + t0, min(ROWT, n - t0)))
    return out


def _ada_kernel(c_ref, w_ref, b_ref, o_ref):
    c = c_ref[...]
    o_ref[...] = _mm(_silu(c), w_ref[...]) + b_ref[...]


def _ada_mod(cvec, ada_w, ada_b):
    L = ada_w.shape[0]
    n = ada_w.shape[2]
    tn = 1536
    return pl.pallas_call(
        _ada_kernel,
        grid=(L, n // tn),
        in_specs=[pl.BlockSpec((16, D_MODEL), lambda l, j: (0, 0)),
                  pl.BlockSpec((None, D_MODEL, tn), lambda l, j: (l, 0, j)),
                  pl.BlockSpec((None, 1, tn), lambda l, j: (l, 0, j))],
        out_specs=pl.BlockSpec((None, 16, tn), lambda l, j: (l, 0, j)),
        out_shape=jax.ShapeDtypeStruct((L, 16, n), F32),
        compiler_params=pltpu.CompilerParams(dimension_semantics=("arbitrary", "arbitrary"),
                                             vmem_limit_bytes=VMEM_LIMIT),
        name="ada_mod",
    )(cvec, ada_w, ada_b.reshape(L, 1, n))


def _rms(x, g):
    return x * lax.rsqrt(jnp.mean(x * x, axis=-1, keepdims=True) + EPS) * g


def _inproj_kernel(x_ref, sh_ref, sc_ref, g_ref, w_ref, qkv_ref, z_ref, ba_ref, lx_ref, lg_ref, rw_ref):
    h = _rms(x_ref[...], g_ref[...]) * (1.0 + sc_ref[...]) + sh_ref[...]
    p = jnp.dot(h.astype(BF16), w_ref[...], preferred_element_type=F32)
    for ref, k in zip((qkv_ref, z_ref, ba_ref, lx_ref, lg_ref, rw_ref), range(6)):
        ref[...] = p[:, int(P_OFF[k]):int(P_OFF[k + 1])]


def _inproj(x2, mod_rows, rows_per_mod, g_all, w_all, layer):
    n = x2.shape[0]
    tm = DENSE_TM
    tiles_per_mod = rows_per_mod // tm
    widths = [int(P_OFF[k + 1] - P_OFF[k]) for k in range(6)]
    return pl.pallas_call(
        _inproj_kernel,
        grid=(n // tm,),
        in_specs=[pl.BlockSpec((tm, D_MODEL), lambda i: (i, 0)),
                  pl.BlockSpec((None, 1, D_MODEL), lambda i: (6 * (i // tiles_per_mod), 0, 0)),
                  pl.BlockSpec((None, 1, D_MODEL), lambda i: (6 * (i // tiles_per_mod) + 1, 0, 0)),
                  pl.BlockSpec((None, 1, D_MODEL), lambda i: (layer, 0, 0)),
                  pl.BlockSpec((None, D_MODEL, D_INP), lambda i: (layer, 0, 0), pipeline_mode=pl.Buffered(1))],
        out_specs=[pl.BlockSpec((tm, w), lambda i: (i, 0)) for w in widths],
        out_shape=[jax.ShapeDtypeStruct((n, w), F32) for w in widths],
        compiler_params=pltpu.CompilerParams(dimension_semantics=("arbitrary",),
                                             vmem_limit_bytes=VMEM_LIMIT),
        name="inproj",
    )(x2, mod_rows, mod_rows, g_all, w_all)


def _gdn_kernel(qc_ref, kc_ref, vc_ref, zc_ref, bac_ref, ql_ref, kl_ref, vl_ref, zl_ref, bal_ref,
                cwq_ref, cwk_ref, cwv_ref, aux_ref, e_ref,
                incl_ref, strict_ref, lvl_ref, hm_ref, bdm_ref, trit_ref, onest_ref,
                oc_ref, ol_ref,
                qp_ref, kp_ref, vp_ref, qs_ref, ks_ref, vs_ref, dir_ref, ls_ref, p1_ref, of_ref, ob_ref):
    tc, tl = qc_ref.shape[0], ql_ref.shape[0]
    tall = tc + tl
    n_ctx, n_all = tc // CH, tall // CH
    grp = _group_size(n_all)
    m0, m1 = hm_ref[0].astype(BF16), hm_ref[1].astype(BF16)
    m0w = jnp.concatenate([m0, m0], axis=1)
    m1w = jnp.concatenate([m1, m1], axis=1)
    bdm = bdm_ref[...]

    _fill_padded(qp_ref, qc_ref, ql_ref, tc, tl)
    _fill_padded(kp_ref, kc_ref, kl_ref, tc, tl)
    _fill_padded(vp_ref, vc_ref, vl_ref, tc, tl)
    lane = lax.broadcasted_iota(jnp.int32, (1, LANES), 1)
    alog, dtb = aux_ref[0:1, :], aux_ref[1:2, :]
    for (po, uo, n) in _row_tiles(tc, tl):
        def conv(src, cw):
            acc = cw[0:1, :] * src[po - 2:po - 2 + n, :]
            for tap in range(1, CONV_W):
                acc = acc + cw[tap:tap + 1, :] * src[po - 2 + tap:po - 2 + tap + n, :]
            return _silu(acc)
        q = conv(qp_ref, cwq_ref)
        k = conv(kp_ref, cwk_ref)
        v = conv(vp_ref, cwv_ref)
        q = q * lax.rsqrt(_mm(q * q, bdm) + 1e-6) * (HD ** -0.5)
        k = k * lax.rsqrt(_mm(k * k, bdm) + 1e-6)
        qs_ref[uo:uo + n, :] = q
        ks_ref[uo:uo + n, :] = k
        vs_ref[uo:uo + n, :] = v
        ba = bac_ref[uo:uo + n, :] if uo < tc else bal_ref[uo - tc:uo - tc + n, :]
        beta = _sigmoid(ba)
        gval = -jnp.exp(alog) * _softplus(ba + dtb)
        bgv = jnp.where(lane < 2 * GDN_HEADS, beta, gval)
        ex = _mm_sel_r(bgv, e_ref[...])
        incl_t = [jnp.concatenate([incl_ref[dd]] * (n // CH), axis=0) for dd in range(2)]
        for d in range(2):
            g = ex[:, (2 + d) * LANES:(3 + d) * LANES]
            gc = _mm_sel_l(trit_ref[d, 0:n, 0:n], g)
            gr = _mm_sel_l(onest_ref[0:n, 0:n], g * incl_t[1 - d])
            dir_ref[d, 0, uo:uo + n, :] = ex[:, d * LANES:(d + 1) * LANES]
            dir_ref[d, 1, uo:uo + n, :] = gc
            dir_ref[d, 2, uo:uo + n, :] = jnp.exp(jnp.minimum(gc - gr, 0.0)) * incl_t[d]

    n_grp = n_all // grp
    chunk_rows = lambda gi, i, d: _chunk_rows(gi, grp, i, d, n_ctx, n_all)
    pd = [(i, d) for i in range(grp) for d in range(2)]

    def stage_a(gi, slot):
        for i, d in pd:
            rows = chunk_rows(gi, i, d)
            q, k = qs_ref[rows, :], ks_ref[rows, :]
            beta, dec = dir_ref[d, 0, rows, :], dir_ref[d, 2, rows, :]
            la = _mm_nt(jnp.concatenate([k, q], axis=0), _bd(k, m0, m1))
            ls_ref[slot, i, d, 0:CH, :] = la[:CH] * dec * strict_ref[d] * beta
            ls_ref[slot, i, d, CH:2 * CH, :] = la[CH:] * dec
            yield

    def stage_b(gi, slot):
        mats = [ls_ref[slot, i, d, 0:CH, :] for i, d in pd]
        es = []
        yield from _inv_levels(mats, [d for _, d in pd], lvl_ref, m0, m1, es)
        sols, kds, egs, gls = [], [], [], []
        for e, (i, d) in zip(es, pd):
            rows = chunk_rows(gi, i, d)
            k, v = ks_ref[rows, :], vs_ref[rows, :]
            beta, gc = dir_ref[d, 0, rows, :], dir_ref[d, 1, rows, :]
            eg = jnp.exp(gc)
            rhs = jnp.concatenate([v * beta, k * beta * eg], axis=1)
            sols.append(rhs + _mm(e, _bd(rhs, m0w, m1w)))
            glast = gc[CH - 1:CH, :] if d == 0 else gc[0:1, :]
            kds.append(k * jnp.exp(glast - gc))
            egs.append(eg)
            gls.append(glast)
        yield
        for sol, kd, eg, glast, (i, d) in zip(sols, kds, egs, gls, pd):
            rows = chunk_rows(gi, i, d)
            attn = ls_ref[slot, i, d, CH:2 * CH, :]
            au = _mm(attn, _bd(sol, m0w, m1w))
            kn = _mm_tn(kd, sol)
            p1_ref[slot, i, d, 0:LANES, :] = kn[:, LANES:] * bdm
            p1_ref[slot, i, d, LANES:LANES + CH, :] = qs_ref[rows, :] * eg - au[:, LANES:]
            p1_ref[slot, i, d, LANES + CH:2 * LANES + CH, :] = kn[:, :LANES] * bdm
            p1_ref[slot, i, d, 2 * LANES + CH:2 * LANES + 2 * CH, :] = au[:, :LANES]
            p1_ref[slot, i, d, 2 * LANES + 2 * CH:2 * LANES + 2 * CH + SUBLANES, :] = jnp.broadcast_to(
                jnp.exp(glast), (SUBLANES, LANES))
        yield

    def stage_c(gi, slot, st):
        for i, d in pd:
            rows = chunk_rows(gi, i, d)
            kwq = p1_ref[slot, i, d, 0:LANES + CH, :]
            nmat = p1_ref[slot, i, d, LANES + CH:2 * LANES + CH, :]
            omat = p1_ref[slot, i, d, 2 * LANES + CH:2 * LANES + 2 * CH, :]
            egl = p1_ref[slot, i, d, 2 * LANES + 2 * CH:2 * LANES + 2 * CH + 1, :]
            ks = _mm(kwq, st[d])
            oref = of_ref if d == 0 else ob_ref
            oref[rows, :] = ks[LANES:] + omat
            st[d] = st[d] * egl - ks[:LANES] + nmat
            yield

    z = jnp.zeros((LANES, LANES), F32)
    _run_pipeline(n_grp, stage_a, stage_b, stage_c, (z, z))

    nw = aux_ref[2:3, :]
    for (po, uo, n) in _row_tiles(tc, tl):
        o = of_ref[uo:uo + n, :] + ob_ref[uo:uo + n, :]
        ms = _mm(o * o, bdm) * (1.0 / HD)
        if uo < tc:
            oc_ref[uo:uo + n, :] = o * lax.rsqrt(ms + EPS) * nw * _silu(zc_ref[uo:uo + n, :])
        else:
            lo = uo - tc
            ol_ref[lo:lo + n, :] = o * lax.rsqrt(ms + EPS) * nw * _silu(zl_ref[lo:lo + n, :])


def _gdn_expand_consts():
    e = np.zeros((N_PAIRS, BA_W, 4 * LANES), np.float32)
    for p in range(N_PAIRS):
        for blk in range(4):
            d, is_g = blk % 2, blk // 2
            for h in range(2):
                col = is_g * 2 * GDN_HEADS + d * GDN_HEADS + 2 * p + h
                e[p, col, blk * LANES + h * HD:blk * LANES + (h + 1) * HD] = 1.0
    return e


def _gdn_mixer(qkv_c, z_c, ba_c, qkv_l, z_l, ba_l, conv_w, a_log, dt_bias, norm_w, consts):
    bsz, tc, _ = qkv_c.shape
    tl = qkv_l.shape[1]
    tall = tc + tl
    n_all = tall // CH
    lane_pad = lambda a: jnp.pad(a.reshape(-1), (2 * GDN_HEADS, LANES - 4 * GDN_HEADS))
    aux = _rows([lane_pad(a_log), lane_pad(dt_bias), jnp.tile(norm_w, 2)], SUBLANES)
    e = jnp.asarray(_gdn_expand_consts())
    cm = lambda *blk: pl.BlockSpec(blk, lambda b, p: (0,) * len(blk))

    def tok(t, col0):
        return pl.BlockSpec((None, t, LANES), lambda b, p: (b, 0, col0 + p))
    in_specs = [tok(tc, 0), tok(tc, N_PAIRS), tok(tc, 2 * N_PAIRS), tok(tc, 0),
                pl.BlockSpec((None, tc, BA_W), lambda b, p: (b, 0, 0)),
                tok(tl, 0), tok(tl, N_PAIRS), tok(tl, 2 * N_PAIRS), tok(tl, 0),
                pl.BlockSpec((None, tl, BA_W), lambda b, p: (b, 0, 0)),
                pl.BlockSpec((CONV_W, LANES), lambda b, p: (0, p)),
                pl.BlockSpec((CONV_W, LANES), lambda b, p: (0, N_PAIRS + p)),
                pl.BlockSpec((CONV_W, LANES), lambda b, p: (0, 2 * N_PAIRS + p)),
                cm(SUBLANES, LANES),
                pl.BlockSpec((None, BA_W, 4 * LANES), lambda b, p: (p, 0, 0)),
                cm(2, CH, LANES), cm(2, CH, LANES), cm(2, 6, CH, LANES), cm(2, 1, LANES), cm(LANES, LANES),
                cm(2, ROWT, ROWT), cm(ROWT, ROWT)]
    pad_rows = tall + 3 * GAP
    scratch = [pltpu.VMEM((pad_rows, LANES), F32)] * 3 + [pltpu.VMEM((tall, LANES), F32)] * 3 + [
        pltpu.VMEM((2, 3, tall, LANES), F32),
        pltpu.VMEM((2, _group_size(n_all), 2, 2 * CH, LANES), F32),
        pltpu.VMEM((2, _group_size(n_all), 2, 2 * LANES + 2 * CH + SUBLANES, LANES), F32),
        pltpu.VMEM((tall, LANES), F32), pltpu.VMEM((tall, LANES), F32)]
    return pl.pallas_call(
        _gdn_kernel,
        grid=(bsz, N_PAIRS),
        in_specs=in_specs,
        out_specs=[pl.BlockSpec((None, tc, LANES), lambda b, p: (b, 0, p)),
                   pl.BlockSpec((None, tl, LANES), lambda b, p: (b, 0, p))],
        out_shape=[jax.ShapeDtypeStruct((bsz, tc, GDN_WIDTH), F32),
                   jax.ShapeDtypeStruct((bsz, tl, GDN_WIDTH), F32)],
        scratch_shapes=scratch,
        compiler_params=pltpu.CompilerParams(dimension_semantics=("arbitrary", "arbitrary"),
                                             vmem_limit_bytes=VMEM_LIMIT),
        name="gdn_mixer",
    )(qkv_c, qkv_c, qkv_c, z_c, ba_c, qkv_l, qkv_l, qkv_l, z_l, ba_l,
      conv_w, conv_w, conv_w, aux, e,
      consts["incl"], consts["strict"], consts["lvl"], consts["hm"], consts["bdm"],
      consts["tri_t"], consts["ones_t"])


def _rwkv_kernel(rc_ref, kc_ref, vc_ref, wdc_ref, adc_ref, gdc_ref,
                 rl_ref, kl_ref, vl_ref, wdl_ref, adl_ref, gdl_ref,
                 mur_ref, muk_ref, muv_ref, mux_ref, wup_ref, aup_ref, gup_ref, vec_ref,
                 incl_ref, strict_ref, lvl_ref, hm_ref, bdm_ref, trit_ref,
                 oc_ref, ol_ref,
                 vs_ref, gs_ref, bvs_ref, dir_ref, ls_ref, p1_ref, yf_ref, yb_ref):
    tc, tl = rc_ref.shape[0], rl_ref.shape[0]
    tall = tc + tl
    n_ctx, n_all = tc // CH, tall // CH
    grp = _group_size(n_all)
    m0, m1 = hm_ref[0].astype(BF16), hm_ref[1].astype(BF16)
    m0w = jnp.concatenate([m0, m0], axis=1)
    m1w = jnp.concatenate([m1, m1], axis=1)
    bdm = bdm_ref[...]

    kkw, kaw, rkw = vec_ref[0:1, :], vec_ref[1:2, :], vec_ref[2:3, :]
    for (_, uo, n) in _row_tiles(tc, tl):
        ctx_tile = uo < tc
        lo = uo if ctx_tile else uo - tc
        t_len = tc if ctx_tile else tl
        rowi = lax.broadcasted_iota(jnp.int32, (n, LANES), 0)

        def lerp(src_c, src_l, mu):
            src = src_c if ctx_tile else src_l
            x = src[lo:lo + n, :]
            prev = src[lo - 1:lo, :] if lo > 0 else jnp.zeros((1, LANES), F32)
            nxt = src[lo + n:lo + n + 1, :] if lo + n < t_len else jnp.zeros((1, LANES), F32)
            xm = jnp.where(rowi == 0, prev, pltpu.roll(x, 1, 0))
            xp = jnp.where(rowi == n - 1, nxt, pltpu.roll(x, n - 1, 0))
            return x + mu * (0.5 * (xm + xp) - x)
        r = lerp(rc_ref, rl_ref, mur_ref[...])
        k = lerp(kc_ref, kl_ref, muk_ref[...])
        v = lerp(vc_ref, vl_ref, muv_ref[...])
        wd = lerp(wdc_ref, wdl_ref, mux_ref[0:1, :])
        ad = lerp(adc_ref, adl_ref, mux_ref[1:2, :])
        gd = lerp(gdc_ref, gdl_ref, mux_ref[2:3, :])
        tw = jnp.tanh(wd)
        kkv = k * kkw
        kk = kkv * lax.rsqrt(_mm(kkv * kkv, bdm) + 1e-6)
        ksum = jnp.zeros_like(k)
        for d in range(2):
            lw = -0.6065306597126334 * _sigmoid(vec_ref[3 + d:4 + d, :] + _mm(tw, wup_ref[d]))
            a = _sigmoid(vec_ref[5 + d:6 + d, :] + _mm(ad, aup_ref[d]))
            kdir = k * (1.0 + (a - 1.0) * kaw)
            ksum = ksum + kdir
            cum = _mm_sel_l(trit_ref[d, 0:n, 0:n], lw)
            einv = jnp.exp(-cum)
            dir_ref[d, 0, uo:uo + n, :] = cum
            dir_ref[d, 1, uo:uo + n, :] = kk * jnp.exp(cum - lw)
            dir_ref[d, 2, uo:uo + n, :] = r * jnp.exp(cum)
            dir_ref[d, 3, uo:uo + n, :] = kk * a * einv
            dir_ref[d, 4, uo:uo + n, :] = kdir * einv
        bonus = _mm_sel_r(r * ksum * rkw, bdm)
        vs_ref[uo:uo + n, :] = v
        gs_ref[uo:uo + n, :] = _mm(_sigmoid(gd), gup_ref[...])
        bvs_ref[uo:uo + n, :] = bonus * v

    n_grp = n_all // grp
    pd = [(i, d) for i in range(grp) for d in range(2)]
    chunk_rows = lambda gi, i, d: _chunk_rows(gi, grp, i, d, n_ctx, n_all)

    def stage_a(gi, slot):
        def finish(pi, pdir, m, vbd):
            ls_ref[slot, pi, pdir, 0:CH, :] = m[:CH, :LANES] * strict_ref[pdir]
            ls_ref[slot, pi, pdir, CH:2 * CH, :] = _mm(m[:CH, LANES:] * strict_ref[pdir], vbd)
            ls_ref[slot, pi, pdir, 2 * CH:3 * CH, :] = m[CH:, :LANES] * incl_ref[pdir]
            ls_ref[slot, pi, pdir, 3 * CH:4 * CH, :] = _mm(m[CH:, LANES:] * incl_ref[pdir], vbd)
        pending = None
        for i, d in pd:
            rows = chunk_rows(gi, i, d)
            kkq, rq, binv, kinv = (dir_ref[d, j, rows, :] for j in range(1, 5))
            lhs = jnp.concatenate([kkq, rq], axis=0)
            rhs = jnp.concatenate([_bd(binv, m0, m1), _bd(kinv, m0, m1)], axis=0)
            cur = (i, d, _mm_nt(lhs, rhs), _bd(vs_ref[rows, :], m0, m1))
            if pending is not None:
                finish(*pending)
            pending = cur
            yield
        finish(*pending)
        yield

    def stage_b(gi, slot):
        mats = [ls_ref[slot, i, d, 0:CH, :] for i, d in pd]
        es = []
        yield from _inv_levels(mats, [d for _, d in pd], lvl_ref, m0, m1, es)
        sols = []
        for e, (i, d) in zip(es, pd):
            rows = chunk_rows(gi, i, d)
            rhs = jnp.concatenate([dir_ref[d, 1, rows, :], ls_ref[slot, i, d, CH:2 * CH, :]], axis=1)
            sols.append(rhs + _mm(e, _bd(rhs, m0w, m1w)))
        yield
        for sol, (i, d) in zip(sols, pd):
            rows = chunk_rows(gi, i, d)
            cum = dir_ref[d, 0, rows, :]
            etot = jnp.exp(cum[CH - 1:CH, :] if d == 0 else cum[0:1, :])
            bdec, kdec = dir_ref[d, 3, rows, :] * etot, dir_ref[d, 4, rows, :] * etot
            ar = _mm(ls_ref[slot, i, d, 2 * CH:3 * CH, :], _bd(sol, m0w, m1w))
            pmat = _mm_tn(sol[:, :LANES], bdec)
            nmat = _mm_tn(jnp.concatenate([-sol[:, LANES:], vs_ref[rows, :]], axis=0),
                          jnp.concatenate([bdec, kdec], axis=0))
            p1_ref[slot, i, d, 0:LANES, :] = pmat * bdm
            p1_ref[slot, i, d, LANES:2 * LANES, :] = nmat * bdm
            p1_ref[slot, i, d, 2 * LANES:2 * LANES + CH, :] = dir_ref[d, 2, rows, :] - ar[:, :LANES]
            p1_ref[slot, i, d, 2 * LANES + CH:2 * LANES + 2 * CH, :] = (
                ls_ref[slot, i, d, 3 * CH:4 * CH, :] - ar[:, LANES:])
            p1_ref[slot, i, d, 2 * LANES + 2 * CH:2 * LANES + 2 * CH + SUBLANES, :] = jnp.broadcast_to(
                etot, (SUBLANES, LANES))
        yield

    def stage_c(gi, slot, st):
        for i, d in pd:
            rows = chunk_rows(gi, i, d)
            pmat = p1_ref[slot, i, d, 0:LANES, :]
            nmat = p1_ref[slot, i, d, LANES:2 * LANES, :]
            rmat = p1_ref[slot, i, d, 2 * LANES:2 * LANES + CH, :]
            ymat = p1_ref[slot, i, d, 2 * LANES + CH:2 * LANES + 2 * CH, :]
            etot = p1_ref[slot, i, d, 2 * LANES + 2 * CH:2 * LANES + 2 * CH + 1, :]
            yref = yf_ref if d == 0 else yb_ref
            yref[rows, :] = _mm_nt(rmat, st[d]) + ymat
            st[d] = st[d] * etot - _mm(st[d], pmat) + nmat
            yield

    z = jnp.zeros((LANES, LANES), F32)
    _run_pipeline(n_grp, stage_a, stage_b, stage_c, (z, z), b_before_a=False)

    gnw, gnb = vec_ref[7:8, :], vec_ref[8:9, :]
    for (po, uo, n) in _row_tiles(tc, tl):
        y = yf_ref[uo:uo + n, :] + yb_ref[uo:uo + n, :]
        mean = _mm_sel_r(y, bdm) * (1.0 / HD)
        yc = y - mean
        var = _mm(yc * yc, bdm) * (1.0 / HD)
        out = (yc * lax.rsqrt(var + RWKV_GN_EPS) * gnw + gnb + bvs_ref[uo:uo + n, :]) * gs_ref[uo:uo + n, :]
        if uo < tc:
            oc_ref[uo:uo + n, :] = out
        else:
            ol_ref[uo - tc:uo - tc + n, :] = out


def _rwkv_mixer(rw_c, rw_l, mu, w0, w_up, a0, a_up, g_up, k_k, k_a, r_k, gn_w, gn_b, consts):
    bsz, tc, _ = rw_c.shape
    tl = rw_l.shape[1]
    tall = tc + tl
    n_all = tall // CH
    W = RWKV_WIDTH
    nb = W // LANES
    zr = jnp.zeros((RWKV_RANK, W), F32)
    pad_dir = lambda u: jnp.stack([jnp.concatenate([u[0], zr], axis=0), jnp.concatenate([zr, u[1]], axis=0)])
    wup, aup = pad_dir(w_up), pad_dir(a_up)
    vec = _rows([k_k, k_a, r_k.reshape(-1), w0[0], w0[1], a0[0], a0[1], gn_w, gn_b], 16)
    mu_rkv = mu[:3 * W].reshape(3, 1, W)
    mux = jnp.pad(mu[3 * W:].reshape(3, LANES), ((0, SUBLANES - 3), (0, 0)))
    cm = lambda *blk: pl.BlockSpec(blk, lambda b, p: (0,) * len(blk))

    def tok(t, col0, per_pair=True):
        if per_pair:
            return pl.BlockSpec((None, t, LANES), lambda b, p: (b, 0, col0 + p))
        return pl.BlockSpec((None, t, LANES), lambda b, p: (b, 0, col0))
    stream = lambda t: [tok(t, 0), tok(t, nb), tok(t, 2 * nb), tok(t, 3 * nb, False),
                        tok(t, 3 * nb + 1, False), tok(t, 3 * nb + 2, False)]
    in_specs = stream(tc) + stream(tl) + [
        pl.BlockSpec((None, 1, LANES), lambda b, p: (0, 0, p)),
        pl.BlockSpec((None, 1, LANES), lambda b, p: (1, 0, p)),
        pl.BlockSpec((None, 1, LANES), lambda b, p: (2, 0, p)),
        cm(SUBLANES, LANES),
        pl.BlockSpec((2, LANES, LANES), lambda b, p: (0, 0, p)),
        pl.BlockSpec((2, LANES, LANES), lambda b, p: (0, 0, p)),
        pl.BlockSpec((RWKV_G_RANK, LANES), lambda b, p: (0, p)),
        pl.BlockSpec((16, LANES), lambda b, p: (0, p)),
        cm(2, CH, LANES), cm(2, CH, LANES), cm(2, 6, CH, LANES), cm(2, 1, LANES), cm(LANES, LANES),
        cm(2, ROWT, ROWT)]
    scratch = [pltpu.VMEM((tall, LANES), F32)] * 3 + [
        pltpu.VMEM((2, 5, tall, LANES), F32),
        pltpu.VMEM((2, _group_size(n_all), 2, 4 * CH, LANES), F32),
        pltpu.VMEM((2, _group_size(n_all), 2, 2 * LANES + 2 * CH + SUBLANES, LANES), F32),
        pltpu.VMEM((tall, LANES), F32), pltpu.VMEM((tall, LANES), F32)]
    return pl.pallas_call(
        _rwkv_kernel,
        grid=(bsz, N_PAIRS),
        in_specs=in_specs,
        out_specs=[pl.BlockSpec((None, tc, LANES), lambda b, p: (b, 0, p)),
                   pl.BlockSpec((None, tl, LANES), lambda b, p: (b, 0, p))],
        out_shape=[jax.ShapeDtypeStruct((bsz, tc, W), F32),
                   jax.ShapeDtypeStruct((bsz, tl, W), F32)],
        scratch_shapes=scratch,
        compiler_params=pltpu.CompilerParams(dimension_semantics=("arbitrary", "arbitrary"),
                                             vmem_limit_bytes=VMEM_LIMIT),
        name="rwkv_mixer",
    )(*([rw_c] * 6), *([rw_l] * 6), mu_rkv, mu_rkv, mu_rkv, mux, wup, aup, g_up, vec,
      consts["incl"], consts["strict"], consts["lvl"], consts["hm"], consts["bdm"], consts["tri_t"])


def _lru_kernel(xc_ref, gc_ref, xl_ref, gl_ref, cw_ref, vec_ref, wcat_ref, bcat_ref,
                oc_ref, ol_ref, xp_ref, ab_ref, hf_ref, hb_ref):
    tc, tl = xc_ref.shape[0], xl_ref.shape[0]
    tall = tc + tl
    rows = tl // GRID_W
    W = xc_ref.shape[1]
    z = jnp.zeros((GAP, W), F32)
    xp_ref[0:GAP, :] = z
    xp_ref[GAP:GAP + tc, :] = xc_ref[...]
    xp_ref[GAP + tc:2 * GAP + tc, :] = z
    base = 2 * GAP + tc
    for c in range(GRID_W):
        xp_ref[base + c * rows:base + (c + 1) * rows, :] = xl_ref[pl.ds(c, rows, stride=GRID_W), :]
    xp_ref[base + tl:base + tl + GAP, :] = z
    cb = vec_ref[0:1, :]
    for (po, uo, n) in _row_tiles(tc, tl):
        acc = cb + cw_ref[0:1, :] * xp_ref[po - 2:po - 2 + n, :]
        for tap in range(1, CONV_W):
            acc = acc + cw_ref[tap:tap + 1, :] * xp_ref[po - 2 + tap:po - 2 + tap + n, :]
        gates = _mm(acc, wcat_ref[...]) + bcat_ref[...]
        for d in range(2):
            rg = _sigmoid(gates[:, (2 * d) * W:(2 * d + 1) * W])
            ig = _sigmoid(gates[:, (2 * d + 1) * W:(2 * d + 2) * W])
            log_a = -LRU_C * rg * _softplus(-vec_ref[1 + d:2 + d, :])
            a = jnp.exp(log_a)
            mult = jnp.sqrt(-jnp.tanh(log_a) * (1.0 + a * a))
            ab_ref[d, 0, uo:uo + n, :] = a
            ab_ref[d, 1, uo:uo + n, :] = mult * (ig * acc)

    sub = lax.broadcasted_iota(jnp.int32, (SUBLANES, W), 0)
    n_tiles_c, n_tiles = tc // SUBLANES, tall // SUBLANES

    def tile_scan(a, b, d):
        for sh in (1, 2, 4):
            if d == 0:
                ok = sub >= sh
                a_s = jnp.where(ok, pltpu.roll(a, sh, 0), 1.0)
                b_s = jnp.where(ok, pltpu.roll(b, sh, 0), 0.0)
            else:
                ok = sub < SUBLANES - sh
                a_s = jnp.where(ok, pltpu.roll(a, SUBLANES - sh, 0), 1.0)
                b_s = jnp.where(ok, pltpu.roll(b, SUBLANES - sh, 0), 0.0)
            b = b + a * b_s
            a = a * a_s
        return a, b

    def scan_body(s, carry):
        hf, hb = carry
        r0 = pl.multiple_of(s * SUBLANES, SUBLANES)
        a, b = tile_scan(ab_ref[0, 0, pl.ds(r0, SUBLANES), :], ab_ref[0, 1, pl.ds(r0, SUBLANES), :], 0)
        h = b + a * hf
        hf_ref[pl.ds(r0, SUBLANES), :] = h
        hf = jnp.broadcast_to(h[SUBLANES - 1:SUBLANES, :], (SUBLANES, W))
        tb = jnp.where(s < n_tiles_c, n_tiles_c - 1 - s, n_tiles + n_tiles_c - 1 - s)
        r1 = pl.multiple_of(tb * SUBLANES, SUBLANES)
        a, b = tile_scan(ab_ref[1, 0, pl.ds(r1, SUBLANES), :], ab_ref[1, 1, pl.ds(r1, SUBLANES), :], 1)
        h = b + a * hb
        hb_ref[pl.ds(r1, SUBLANES), :] = h
        hb = jnp.broadcast_to(h[0:1, :], (SUBLANES, W))
        return hf, hb
    z8 = jnp.zeros((SUBLANES, W), F32)
    lax.fori_loop(0, n_tiles, scan_body, (z8, z8))

    oc_ref[...] = (hf_ref[0:tc, :] + hb_ref[0:tc, :]) * _gelu_tanh(gc_ref[...])
    for c in range(GRID_W):
        h = hf_ref[tc + c * rows:tc + (c + 1) * rows, :] + hb_ref[tc + c * rows:tc + (c + 1) * rows, :]
        ol_ref[pl.ds(c, rows, stride=GRID_W), :] = h * _gelu_tanh(gl_ref[pl.ds(c, rows, stride=GRID_W), :])


def _lru_mixer(x_c, g_c, x_l, g_l, conv_w, conv_b, w_a, b_a, w_x, b_x, lam):
    bsz, tc, W = x_c.shape
    tl = x_l.shape[1]
    tall = tc + tl
    nh = W // LANES
    bph = LANES // LRU_BW
    w6 = jnp.stack([w_a, w_x], axis=1).reshape(2, 2, nh, bph, LRU_BW, LRU_BW)
    wcat = jnp.einsum("dghnrc,nm->hnrdgmc", w6, jnp.eye(bph, dtype=F32)).reshape(nh, LANES, 4 * LANES)
    bcat = jnp.stack([b_a, b_x], axis=1).reshape(2, 2, nh, LANES).transpose(2, 0, 1, 3).reshape(nh, 1, 4 * LANES)
    vec = _rows([conv_b, lam[0], lam[1]], SUBLANES)
    tok = lambda t: pl.BlockSpec((None, t, LANES), lambda b, h: (b, 0, h))
    return pl.pallas_call(
        _lru_kernel,
        grid=(bsz, nh),
        in_specs=[tok(tc), tok(tc), tok(tl), tok(tl),
                  pl.BlockSpec((CONV_W, LANES), lambda b, h: (0, h)),
                  pl.BlockSpec((SUBLANES, LANES), lambda b, h: (0, h)),
                  pl.BlockSpec((None, LANES, 4 * LANES), lambda b, h: (h, 0, 0)),
                  pl.BlockSpec((None, 1, 4 * LANES), lambda b, h: (h, 0, 0))],
        out_specs=[tok(tc), tok(tl)],
        out_shape=[jax.ShapeDtypeStruct((bsz, tc, W), F32), jax.ShapeDtypeStruct((bsz, tl, W), F32)],
        scratch_shapes=[pltpu.VMEM((tall + 3 * GAP, LANES), F32),
                        pltpu.VMEM((2, 2, tall, LANES), F32),
                        pltpu.VMEM((tall, LANES), F32), pltpu.VMEM((tall, LANES), F32)],
        compiler_params=pltpu.CompilerParams(dimension_semantics=("arbitrary", "arbitrary"),
                                             vmem_limit_bytes=VMEM_LIMIT),
        name="lru_mixer",
    )(x_c, g_c, x_l, g_l, conv_w, vec, wcat.astype(BF16), bcat)


def _finish_kernel(x_ref, gdn_ref, lru_ref, rwk_ref, m2_ref, m3_ref, m4_ref, m5_ref, nrm_ref,
                   wo_ref, up_ref, dn_ref, o_ref):
    x = x_ref[...]
    o = (jnp.dot(gdn_ref[...].astype(BF16), wo_ref[0:GDN_WIDTH, :], preferred_element_type=F32)
         + jnp.dot(lru_ref[...].astype(BF16), wo_ref[GDN_WIDTH:GDN_WIDTH + LRU_WIDTH, :],
                   preferred_element_type=F32)
         + jnp.dot(rwk_ref[...].astype(BF16), wo_ref[GDN_WIDTH + LRU_WIDTH:, :], preferred_element_type=F32))
    x = x + m2_ref[...] * _rms(o, nrm_ref[0:1, :])
    h = (_rms(x, nrm_ref[1:2, :]) * (1.0 + m4_ref[...]) + m3_ref[...]).astype(BF16)
    f = jnp.zeros_like(x)
    fc = 1024
    for j in range(D_FF // fc):
        a = jnp.maximum(jnp.dot(h, up_ref[:, j * fc:(j + 1) * fc], preferred_element_type=F32), 0.0)
        f = f + jnp.dot((a * a).astype(BF16), dn_ref[j * fc:(j + 1) * fc, :], preferred_element_type=F32)
    o_ref[...] = x + m5_ref[...] * _rms(f, nrm_ref[2:3, :])


def _finish(x2, gdn, lru, rwk, mod_rows, rows_per_mod, norms, wo_bf, up_bf, dn_bf, layer):
    n = x2.shape[0]
    tm = DENSE_TM
    tiles_per_mod = rows_per_mod // tm
    modspec = lambda k: pl.BlockSpec((None, 1, D_MODEL), lambda i: (6 * (i // tiles_per_mod) + k, 0, 0))
    cm = lambda *blk: pl.BlockSpec((None,) + blk, lambda i: (layer,) + (0,) * len(blk),
                                   pipeline_mode=pl.Buffered(1))
    return pl.pallas_call(
        _finish_kernel,
        grid=(n // tm,),
        in_specs=[pl.BlockSpec((tm, D_MODEL), lambda i: (i, 0)),
                  pl.BlockSpec((tm, GDN_WIDTH), lambda i: (i, 0)),
                  pl.BlockSpec((tm, LRU_WIDTH), lambda i: (i, 0)),
                  pl.BlockSpec((tm, RWKV_WIDTH), lambda i: (i, 0)),
                  modspec(2), modspec(3), modspec(4), modspec(5),
                  cm(SUBLANES, D_MODEL), cm(D_MODEL, D_MODEL), cm(D_MODEL, D_FF), cm(D_FF, D_MODEL)],
        out_specs=pl.BlockSpec((tm, D_MODEL), lambda i: (i, 0)),
        out_shape=jax.ShapeDtypeStruct((n, D_MODEL), F32),
        compiler_params=pltpu.CompilerParams(dimension_semantics=("arbitrary",),
                                             vmem_limit_bytes=VMEM_LIMIT),
        name="finish",
    )(x2, gdn, lru, rwk, mod_rows, mod_rows, mod_rows, mod_rows, norms, wo_bf, up_bf, dn_bf)


def _arrange_w_in(w):
    s = np.cumsum([0, 3 * GDN_WIDTH, GDN_WIDTH, 2 * GDN_HEADS, 2 * GDN_HEADS, LRU_WIDTH, LRU_WIDTH, RWKV_IN])
    pad = jnp.zeros(w.shape[:-1] + (BA_W - 4 * GDN_HEADS,), w.dtype)
    return jnp.concatenate([w[..., s[0]:s[4]], pad, w[..., s[4]:]], axis=-1)


def kernel(x, c, ctx, c_ctx, ada_w, ada_b, norm_mix_pre, norm_mix_post, norm_ffn_pre, norm_ffn_post, w_in, gdn_conv, gdn_a_log, gdn_dt_bias, gdn_norm, lru_conv, lru_conv_b, lru_wa, lru_ba, lru_wx, lru_bx, lru_lambda, rwkv_mu, rwkv_w0, rwkv_w_up, rwkv_a0, rwkv_a_up, rwkv_g_up, rwkv_k_k, rwkv_k_a, rwkv_r_k, rwkv_gn_w, rwkv_gn_b, w_out, ffn_up, ffn_down):
    bsz, tl, _ = x.shape
    tc = ctx.shape[1]
    depth = w_in.shape[0]
    consts = _device_consts()

    cvec = jnp.pad(jnp.concatenate([c, c_ctx[None, :]], axis=0), ((0, 16 - bsz - 1), (0, 0)))
    mods = _ada_mod(cvec, ada_w, ada_b).reshape(depth, 16, 6, D_MODEL)

    w_in_bf = _arrange_w_in(w_in).astype(BF16)
    wo_bf, up_bf, dn_bf = w_out.astype(BF16), ffn_up.astype(BF16), ffn_down.astype(BF16)
    g_pre = norm_mix_pre.reshape(depth, 1, D_MODEL)
    norms = jnp.pad(jnp.stack([norm_mix_post, norm_ffn_pre, norm_ffn_post], axis=1),
                    ((0, 0), (0, SUBLANES - 3), (0, 0)))

    xl = x.reshape(bsz * tl, D_MODEL)
    xc = ctx.reshape(bsz * tc, D_MODEL)
    for i in range(depth):
        mod_l = mods[i, 0:bsz].reshape(bsz * 6, 1, D_MODEL)
        mod_c = mods[i, bsz:bsz + 1].reshape(6, 1, D_MODEL)
        pl_ = _inproj(xl, mod_l, tl, g_pre, w_in_bf, i)
        pc_ = _inproj(xc, mod_c, bsz * tc, g_pre, w_in_bf, i)
        r3 = lambda a, t: a.reshape(bsz, t, a.shape[-1])
        qkv_l, z_l, ba_l, lx_l, lg_l, rw_l = (r3(a, tl) for a in pl_)
        qkv_c, z_c, ba_c, lx_c, lg_c, rw_c = (r3(a, tc) for a in pc_)

        gdn_c, gdn_l = _gdn_mixer(qkv_c, z_c, ba_c, qkv_l, z_l, ba_l, gdn_conv[i], gdn_a_log[i],
                                  gdn_dt_bias[i], gdn_norm[i], consts)
        lru_c, lru_l = _lru_mixer(lx_c, lg_c, lx_l, lg_l, lru_conv[i], lru_conv_b[i], lru_wa[i], lru_ba[i],
                                  lru_wx[i], lru_bx[i], lru_lambda[i])
        rwk_c, rwk_l = _rwkv_mixer(rw_c, rw_l, rwkv_mu[i], rwkv_w0[i], rwkv_w_up[i], rwkv_a0[i],
                                   rwkv_a_up[i], rwkv_g_up[i], rwkv_k_k[i], rwkv_k_a[i], rwkv_r_k[i],
                                   rwkv_gn_w[i], rwkv_gn_b[i], consts)

        f2 = lambda a: a.reshape(-1, a.shape[-1])
        xl = _finish(xl, f2(gdn_l), f2(lru_l), f2(rwk_l), mod_l, tl, norms, wo_bf, up_bf, dn_bf, i)
        if i < depth - 1:
            xc = _finish(xc, f2(gdn_c), f2(lru_c), f2(rwk_c), mod_c, bsz * tc, norms, wo_bf, up_bf, dn_bf, i)
    return xl.reshape(bsz, tl, D_MODEL)
```

```python
import functools

import numpy as np
import jax
import jax.numpy as jnp
from jax import lax
from jax.experimental import pallas as pl
from jax.experimental.pallas import tpu as pltpu

F32 = jnp.float32
BF16 = jnp.bfloat16

LANES = 128
SUBLANES = 8
VMEM_LIMIT = 56 * 1024 * 1024

D_MODEL = 1024
DEPTH = 2
GRID_W = 64
CONV_W = 4
EPS = 1e-6
D_FF = 4 * D_MODEL
HD = 64
GDN_WIDTH = 3 * D_MODEL // 8
GDN_HEADS = GDN_WIDTH // HD
LRU_WIDTH = D_MODEL // 4
LRU_BLOCKS = 4
LRU_BW = LRU_WIDTH // LRU_BLOCKS
LRU_C = 8.0
RWKV_WIDTH = D_MODEL - GDN_WIDTH - LRU_WIDTH
RWKV_HEADS = RWKV_WIDTH // HD
RWKV_RANK = 64
RWKV_G_RANK = 128
RWKV_GN_EPS = 6.4e-4
RWKV_IN = 3 * RWKV_WIDTH + 2 * RWKV_RANK + 2 * RWKV_RANK + RWKV_G_RANK
N_PAIRS = GDN_HEADS // 2
BA_W = LANES
P_OFF = np.cumsum([0, 3 * GDN_WIDTH, GDN_WIDTH, BA_W, LRU_WIDTH, LRU_WIDTH, RWKV_IN])
D_INP = int(P_OFF[-1])

CH = 64
ROWT = 256
DENSE_TM = 512
GAP = SUBLANES


def _mm(a, b):
    return jnp.dot(a.astype(BF16), b.astype(BF16), preferred_element_type=F32)


def _mm_nt(a, b):
    return lax.dot_general(a.astype(BF16), b.astype(BF16), (((1,), (1,)), ((), ())),
                           preferred_element_type=F32)


def _mm_tn(a, b):
    return lax.dot_general(a.astype(BF16), b.astype(BF16), (((0,), (0,)), ((), ())),
                           preferred_element_type=F32)


def _split2(x):
    hi = x.astype(BF16)
    lo = (x - hi.astype(F32)).astype(BF16)
    return hi, lo


def _mm_sel_l(m01, x):
    mb = m01.astype(BF16)
    h, l = _split2(x)
    d = functools.partial(jnp.dot, preferred_element_type=F32)
    return d(mb, h) + d(mb, l)


def _mm_sel_r(x, m01):
    mb = m01.astype(BF16)
    h, l = _split2(x)
    d = functools.partial(jnp.dot, preferred_element_type=F32)
    return d(h, mb) + d(l, mb)


def _sigmoid(x):
    return 0.5 * jnp.tanh(0.5 * x) + 0.5


def _silu_of_twice(h):
    return h + h * jnp.tanh(h)


def _silu(x):
    return _silu_of_twice(0.5 * x)


def _softplus(x):
    return jnp.maximum(x, 0.0) + jnp.log(1.0 + jnp.exp(-jnp.abs(x)))


def _gelu_tanh(x):
    return 0.5 * x * (1.0 + jnp.tanh(0.7978845608028654 * (x + 0.044715 * (x * x * x))))


def _bd(y, m0, m1):
    yb = y.astype(BF16)
    return jnp.concatenate([yb * m0, yb * m1], axis=0)


def _pair_consts():
    i = np.arange(CH)[:, None]
    j = (np.arange(LANES) % HD)[None, :]
    incl = np.stack([i >= j, i <= j]).astype(np.float32)
    strict = np.stack([i > j, i < j]).astype(np.float32)
    lvls = []
    for d in range(2):
        per = []
        for m in (1, 2, 4, 8, 16, 32):
            same = (i // (2 * m)) == (j // (2 * m))
            lo_i, lo_j = (i % (2 * m)) < m, (j % (2 * m)) < m
            off = same & (~lo_i) & lo_j if d == 0 else same & lo_i & (~lo_j)
            per.append(off)
        lvls.append(np.stack(per))
    lvl = np.stack(lvls).astype(np.float32)
    lane = np.arange(LANES)
    hm = np.stack([lane < HD, lane >= HD]).astype(np.float32)[:, None, :]
    bdm = ((np.arange(LANES)[:, None] // HD) == (lane[None, :] // HD)).astype(np.float32)
    r = np.arange(ROWT)[:, None]
    c = np.arange(ROWT)[None, :]
    same = (r // CH) == (c // CH)
    tri_t = np.stack([same & (r >= c), same & (r <= c)]).astype(np.float32)
    ones_t = same.astype(np.float32)
    return dict(incl=incl, strict=strict, lvl=lvl, hm=hm, bdm=bdm, tri_t=tri_t, ones_t=ones_t)


def _device_consts():
    out = {k: jnp.asarray(v) for k, v in _pair_consts().items()}
    for k in ("tri_t", "ones_t"):
        out[k] = out[k].astype(BF16)
    return out


def _inv_levels(mats, dirs, lvl_ref, m0, m1, out):
    es = [-(a * lvl_ref[d, 0]) for a, d in zip(mats, dirs)]
    for k in range(1, 6):
        offs = [a * lvl_ref[d, k] for a, d in zip(mats, dirs)]
        xs = [off + p for off, p in zip(offs, _mm_pairs(es, [_bd(off, m0, m1) for off in offs]))]
        yield
        es = [e - x - p for e, x, p in zip(es, xs, _mm_pairs(xs, [_bd(e, m0, m1) for e in es]))]
        yield
    out.extend(es)


def _mm_pairs(lhs, rhs):
    out = []
    for j in range(0, len(lhs) - 1, 2):
        r = jnp.dot(jnp.concatenate([lhs[j], lhs[j + 1]], axis=0).astype(BF16),
                    jnp.concatenate([rhs[j], rhs[j + 1]], axis=1), preferred_element_type=F32)
        out += [r[:CH, :LANES], r[CH:, LANES:]]
    if len(lhs) % 2:
        out.append(_mm(lhs[-1], rhs[-1]))
    return out


def _rows(vectors, n_rows):
    m = jnp.stack([v.astype(F32) for v in vectors])
    return jnp.pad(m, ((0, n_rows - m.shape[0]), (0, 0)))


def _round_robin(gens):
    gens = list(gens)
    while gens:
        alive = []
        for g in gens:
            try:
                next(g)
                alive.append(g)
            except StopIteration:
                pass
        gens = alive


def _group_size(n_all):
    for g in (6, 4, 3, 2):
        if n_all % g == 0:
            return g
    return 1


def _chunk_order(s, n_ctx, n_all, d):
    if d == 0:
        return s
    if isinstance(s, int):
        return n_ctx - 1 - s if s < n_ctx else n_all + n_ctx - 1 - s
    return jnp.where(s < n_ctx, n_ctx - 1 - s, n_all + n_ctx - 1 - s)


def _chunk_rows(gi, grp, i, d, n_ctx, n_all):
    c = _chunk_order(gi * grp + i, n_ctx, n_all, d)
    return pl.ds(c * CH if isinstance(c, int) else pl.multiple_of(c * CH, CH), CH)


def _run_pipeline(n_grp, stage_a, stage_b, stage_c, st, b_before_a=True):
    def step(t, st, do_a, do_b, do_c):
        par = t % 2 if isinstance(t, int) else lax.rem(t, 2)
        st = list(st)
        gens = []
        if do_c:
            gens.append(stage_c(t - 2, par, st))
        ab = ([stage_b(t - 1, 1 - par)] if do_b else []) + ([stage_a(t, par)] if do_a else [])
        _round_robin(gens + (ab if b_before_a else ab[::-1]))
        return tuple(st)
    assert n_grp >= 2
    st = step(0, st, True, False, False)
    st = step(1, st, True, True, False)
    st = lax.fori_loop(2, n_grp, lambda t, s: step(t, s, True, True, True), st)
    st = step(n_grp, st, False, True, True)
    return step(n_grp + 1, st, False, False, True)


def _fill_padded(dst_ref, src_c_ref, src_l_ref, tc, tl):
    w = dst_ref.shape[1]
    z = jnp.zeros((GAP, w), F32)
    dst_ref[0:GAP, :] = z
    dst_ref[GAP:GAP + tc, :] = src_c_ref[...]
    dst_ref[GAP + tc:2 * GAP + tc, :] = z
    dst_ref[2 * GAP + tc:2 * GAP + tc + tl, :] = src_l_ref[...]
    dst_ref[2 * GAP + tc + tl:3 * GAP + tc + tl, :] = z


def _row_tiles(tc, tl):
    out = []
    for base_p, base_u, n in ((GAP, 0, tc), (2 * GAP + tc, tc, tl)):
        for t0 in range(0, n, ROWT):
            out.append((base_p + t0, base_u + t0, min(ROWT, n - t0)))
    return out


def _ada_kernel(c_ref, w_ref, b_ref, o_ref):
    c = c_ref[...]
    o_ref[...] = _mm(_silu(c), w_ref[...]) + b_ref[...]


def _ada_mod(cvec, ada_w, ada_b):
    L = ada_w.shape[0]
    n = ada_w.shape[2]
    tn = 1536
    return pl.pallas_call(
        _ada_kernel,
        grid=(L, n // tn),
        in_specs=[pl.BlockSpec((16, D_MODEL), lambda l, j: (0, 0)),
                  pl.BlockSpec((None, D_MODEL, tn), lambda l, j: (l, 0, j)),
                  pl.BlockSpec((None, 1, tn), lambda l, j: (l, 0, j))],
        out_specs=pl.BlockSpec((None, 16, tn), lambda l, j: (l, 0, j)),
        out_shape=jax.ShapeDtypeStruct((L, 16, n), F32),
        compiler_params=pltpu.CompilerParams(dimension_semantics=("arbitrary", "arbitrary"),
                                             vmem_limit_bytes=VMEM_LIMIT),
        name="ada_mod",
    )(cvec, ada_w, ada_b.reshape(L, 1, n))


def _rms(x, g):
    return x * lax.rsqrt(jnp.mean(x * x, axis=-1, keepdims=True) + EPS) * g


def _inproj_kernel(x_ref, sh_ref, sc_ref, g_ref, w_ref, qkv_ref, z_ref, ba_ref, lx_ref, lg_ref, rw_ref):
    h = _rms(x_ref[...], g_ref[...]) * (1.0 + sc_ref[...]) + sh_ref[...]
    p = jnp.dot(h.astype(BF16), w_ref[...], preferred_element_type=F32)
    for ref, k in zip((qkv_ref, z_ref, ba_ref, lx_ref, lg_ref, rw_ref), range(6)):
        ref[...] = p[:, int(P_OFF[k]):int(P_OFF[k + 1])]


def _inproj(x2, mod_rows, rows_per_mod, g_all, w_all, layer):
    n = x2.shape[0]
    tm = DENSE_TM
    tiles_per_mod = rows_per_mod // tm
    widths = [int(P_OFF[k + 1] - P_OFF[k]) for k in range(6)]
    return pl.pallas_call(
        _inproj_kernel,
        grid=(n // tm,),
        in_specs=[pl.BlockSpec((tm, D_MODEL), lambda i: (i, 0)),
                  pl.BlockSpec((None, 1, D_MODEL), lambda i: (6 * (i // tiles_per_mod), 0, 0)),
                  pl.BlockSpec((None, 1, D_MODEL), lambda i: (6 * (i // tiles_per_mod) + 1, 0, 0)),
                  pl.BlockSpec((None, 1, D_MODEL), lambda i: (layer, 0, 0)),
                  pl.BlockSpec((None, D_MODEL, D_INP), lambda i: (layer, 0, 0), pipeline_mode=pl.Buffered(1))],
        out_specs=[pl.BlockSpec((tm, w), lambda i: (i, 0)) for w in widths],
        out_shape=[jax.ShapeDtypeStruct((n, w), F32) for w in widths],
        compiler_params=pltpu.CompilerParams(dimension_semantics=("arbitrary",),
                                             vmem_limit_bytes=VMEM_LIMIT),
        name="inproj",
    )(x2, mod_rows, mod_rows, g_all, w_all)


def _gdn_kernel(qc_ref, kc_ref, vc_ref, zc_ref, bac_ref, ql_ref, kl_ref, vl_ref, zl_ref, bal_ref,
                cwq_ref, cwk_ref, cwv_ref, aux_ref, e_ref,
                incl_ref, strict_ref, lvl_ref, hm_ref, bdm_ref, trit_ref, onest_ref,
                oc_ref, ol_ref,
                qp_ref, kp_ref, vp_ref, qs_ref, ks_ref, vs_ref, dir_ref, ls_ref, p1_ref, of_ref, ob_ref):
    tc, tl = qc_ref.shape[0], ql_ref.shape[0]
    tall = tc + tl
    n_ctx, n_all = tc // CH, tall // CH
    grp = _group_size(n_all)
    m0, m1 = hm_ref[0].astype(BF16), hm_ref[1].astype(BF16)
    m0w = jnp.concatenate([m0, m0], axis=1)
    m1w = jnp.concatenate([m1, m1], axis=1)
    bdm = bdm_ref[...]

    _fill_padded(qp_ref, qc_ref, ql_ref, tc, tl)
    _fill_padded(kp_ref, kc_ref, kl_ref, tc, tl)
    _fill_padded(vp_ref, vc_ref, vl_ref, tc, tl)
    lane = lax.broadcasted_iota(jnp.int32, (1, LANES), 1)
    alog, dtb = aux_ref[0:1, :], aux_ref[1:2, :]
    for (po, uo, n) in _row_tiles(tc, tl):
        def conv(src, cw_ref):
            cw = 0.5 * cw_ref[...]
            acc = cw[0:1, :] * src[po - 2:po - 2 + n, :]
            for tap in range(1, CONV_W):
                acc = acc + cw[tap:tap + 1, :] * src[po - 2 + tap:po - 2 + tap + n, :]
            return _silu_of_twice(acc)
        q = conv(qp_ref, cwq_ref)
        k = conv(kp_ref, cwk_ref)
        v = conv(vp_ref, cwv_ref)
        q = q * lax.rsqrt(_mm(q * q, bdm) + 1e-6) * (HD ** -0.5)
        k = k * lax.rsqrt(_mm(k * k, bdm) + 1e-6)
        qs_ref[uo:uo + n, :] = q
        ks_ref[uo:uo + n, :] = k
        vs_ref[uo:uo + n, :] = v
        ba = bac_ref[uo:uo + n, :] if uo < tc else bal_ref[uo - tc:uo - tc + n, :]
        beta = _sigmoid(ba)
        gval = -jnp.exp(alog) * _softplus(ba + dtb)
        bgv = jnp.where(lane < 2 * GDN_HEADS, beta, gval)
        ex = _mm_sel_r(bgv, e_ref[...])
        incl_t = [jnp.concatenate([incl_ref[dd]] * (n // CH), axis=0) for dd in range(2)]
        for d in range(2):
            g = ex[:, (2 + d) * LANES:(3 + d) * LANES]
            gc = _mm_sel_l(trit_ref[d, 0:n, 0:n], g)
            gt = gc.T
            gr = jnp.concatenate(
                [jnp.broadcast_to(jnp.concatenate([gt[0:1, c * CH:(c + 1) * CH], gt[HD:HD + 1, c * CH:(c + 1) * CH]],
                                                  axis=1), (CH, LANES)) for c in range(n // CH)], axis=0)
            dir_ref[d, 0, uo:uo + n, :] = ex[:, d * LANES:(d + 1) * LANES]
            dir_ref[d, 1, uo:uo + n, :] = gc
            dir_ref[d, 2, uo:uo + n, :] = jnp.exp(jnp.minimum(gc - gr, 0.0)) * incl_t[d]

    n_grp = n_all // grp
    chunk_rows = lambda gi, i, d: _chunk_rows(gi, grp, i, d, n_ctx, n_all)
    pd = [(i, d) for i in range(grp) for d in range(2)]

    def stage_a(gi, slot):
        for i, d in pd:
            rows = chunk_rows(gi, i, d)
            q, k = qs_ref[rows, :], ks_ref[rows, :]
            beta, dec = dir_ref[d, 0, rows, :], dir_ref[d, 2, rows, :]
            la = _mm_nt(jnp.concatenate([k, q], axis=0), _bd(k, m0, m1))
            ls_ref[slot, i, d, 0:CH, :] = la[:CH] * dec * strict_ref[d] * beta
            ls_ref[slot, i, d, CH:2 * CH, :] = la[CH:] * dec
            yield

    def stage_b(gi, slot):
        mats = [ls_ref[slot, i, d, 0:CH, :] for i, d in pd]
        es = []
        yield from _inv_levels(mats, [d for _, d in pd], lvl_ref, m0, m1, es)
        sols, kds, egs, gls = [], [], [], []
        for e, (i, d) in zip(es, pd):
            rows = chunk_rows(gi, i, d)
            k, v = ks_ref[rows, :], vs_ref[rows, :]
            beta, gc = dir_ref[d, 0, rows, :], dir_ref[d, 1, rows, :]
            eg = jnp.exp(gc)
            rhs = jnp.concatenate([v * beta, k * beta * eg], axis=1)
            sols.append(rhs + _mm(e, _bd(rhs, m0w, m1w)))
            glast = gc[CH - 1:CH, :] if d == 0 else gc[0:1, :]
            kds.append(k * jnp.exp(glast - gc))
            egs.append(eg)
            gls.append(glast)
        yield
        for sol, kd, eg, glast, (i, d) in zip(sols, kds, egs, gls, pd):
            rows = chunk_rows(gi, i, d)
            attn = ls_ref[slot, i, d, CH:2 * CH, :]
            au = _mm(attn, _bd(sol, m0w, m1w))
            kn = _mm_tn(kd, sol)
            p1_ref[slot, i, d, 0:LANES, :] = kn[:, LANES:] * bdm
            p1_ref[slot, i, d, LANES:LANES + CH, :] = qs_ref[rows, :] * eg - au[:, LANES:]
            p1_ref[slot, i, d, LANES + CH:2 * LANES + CH, :] = kn[:, :LANES] * bdm
            p1_ref[slot, i, d, 2 * LANES + CH:2 * LANES + 2 * CH, :] = au[:, :LANES]
            p1_ref[slot, i, d, 2 * LANES + 2 * CH:2 * LANES + 2 * CH + SUBLANES, :] = jnp.broadcast_to(
                jnp.exp(glast), (SUBLANES, LANES))
        yield

    def stage_c(gi, slot, st):
        for i, d in pd:
            rows = chunk_rows(gi, i, d)
            kwq = p1_ref[slot, i, d, 0:LANES + CH, :]
            nmat = p1_ref[slot, i, d, LANES + CH:2 * LANES + CH, :]
            omat = p1_ref[slot, i, d, 2 * LANES + CH:2 * LANES + 2 * CH, :]
            egl = p1_ref[slot, i, d, 2 * LANES + 2 * CH:2 * LANES + 2 * CH + 1, :]
            ks = _mm(kwq, st[d])
            oref = of_ref if d == 0 else ob_ref
            oref[rows, :] = ks[LANES:] + omat
            st[d] = st[d] * egl - ks[:LANES] + nmat
            yield

    z = jnp.zeros((LANES, LANES), F32)
    _run_pipeline(n_grp, stage_a, stage_b, stage_c, (z, z))

    nw = aux_ref[2:3, :]
    for (po, uo, n) in _row_tiles(tc, tl):
        o = of_ref[uo:uo + n, :] + ob_ref[uo:uo + n, :]
        ms = _mm(o * o, bdm) * (1.0 / HD)
        if uo < tc:
            oc_ref[uo:uo + n, :] = o * lax.rsqrt(ms + EPS) * nw * _silu(zc_ref[uo:uo + n, :])
        else:
            lo = uo - tc
            ol_ref[lo:lo + n, :] = o * lax.rsqrt(ms + EPS) * nw * _silu(zl_ref[lo:lo + n, :])


def _gdn_expand_consts():
    e = np.zeros((N_PAIRS, BA_W, 4 * LANES), np.float32)
    for p in range(N_PAIRS):
        for blk in range(4):
            d, is_g = blk % 2, blk // 2
            for h in range(2):
                col = is_g * 2 * GDN_HEADS + d * GDN_HEADS + 2 * p + h
                e[p, col, blk * LANES + h * HD:blk * LANES + (h + 1) * HD] = 1.0
    return e


def _gdn_mixer(qkv_c, z_c, ba_c, qkv_l, z_l, ba_l, conv_w, a_log, dt_bias, norm_w, consts):
    bsz, tc, _ = qkv_c.shape
    tl = qkv_l.shape[1]
    tall = tc + tl
    n_all = tall // CH
    lane_pad = lambda a: jnp.pad(a.reshape(-1), (2 * GDN_HEADS, LANES - 4 * GDN_HEADS))
    aux = _rows([lane_pad(a_log), lane_pad(dt_bias), jnp.tile(norm_w, 2)], SUBLANES)
    e = jnp.asarray(_gdn_expand_consts())
    cm = lambda *blk: pl.BlockSpec(blk, lambda b, p: (0,) * len(blk))

    def tok(t, col0):
        return pl.BlockSpec((None, t, LANES), lambda b, p: (b, 0, col0 + p))
    in_specs = [tok(tc, 0), tok(tc, N_PAIRS), tok(tc, 2 * N_PAIRS), tok(tc, 0),
                pl.BlockSpec((None, tc, BA_W), lambda b, p: (b, 0, 0)),
                tok(tl, 0), tok(tl, N_PAIRS), tok(tl, 2 * N_PAIRS), tok(tl, 0),
                pl.BlockSpec((None, tl, BA_W), lambda b, p: (b, 0, 0)),
                pl.BlockSpec((CONV_W, LANES), lambda b, p: (0, p)),
                pl.BlockSpec((CONV_W, LANES), lambda b, p: (0, N_PAIRS + p)),
                pl.BlockSpec((CONV_W, LANES), lambda b, p: (0, 2 * N_PAIRS + p)),
                cm(SUBLANES, LANES),
                pl.BlockSpec((None, BA_W, 4 * LANES), lambda b, p: (p, 0, 0)),
                cm(2, CH, LANES), cm(2, CH, LANES), cm(2, 6, CH, LANES), cm(2, 1, LANES), cm(LANES, LANES),
                cm(2, ROWT, ROWT), cm(ROWT, ROWT)]
    pad_rows = tall + 3 * GAP
    scratch = [pltpu.VMEM((pad_rows, LANES), F32)] * 3 + [pltpu.VMEM((tall, LANES), F32)] * 3 + [
        pltpu.VMEM((2, 3, tall, LANES), F32),
        pltpu.VMEM((2, _group_size(n_all), 2, 2 * CH, LANES), F32),
        pltpu.VMEM((2, _group_size(n_all), 2, 2 * LANES + 2 * CH + SUBLANES, LANES), F32),
        pltpu.VMEM((tall, LANES), F32), pltpu.VMEM((tall, LANES), F32)]
    return pl.pallas_call(
        _gdn_kernel,
        grid=(bsz, N_PAIRS),
        in_specs=in_specs,
        out_specs=[pl.BlockSpec((None, tc, LANES), lambda b, p: (b, 0, p)),
                   pl.BlockSpec((None, tl, LANES), lambda b, p: (b, 0, p))],
        out_shape=[jax.ShapeDtypeStruct((bsz, tc, GDN_WIDTH), F32),
                   jax.ShapeDtypeStruct((bsz, tl, GDN_WIDTH), F32)],
        scratch_shapes=scratch,
        compiler_params=pltpu.CompilerParams(dimension_semantics=("arbitrary", "arbitrary"),
                                             vmem_limit_bytes=VMEM_LIMIT),
        name="gdn_mixer",
    )(qkv_c, qkv_c, qkv_c, z_c, ba_c, qkv_l, qkv_l, qkv_l, z_l, ba_l,
      conv_w, conv_w, conv_w, aux, e,
      consts["incl"], consts["strict"], consts["lvl"], consts["hm"], consts["bdm"],
      consts["tri_t"], consts["ones_t"])


def _rwkv_kernel(rc_ref, kc_ref, vc_ref, wdc_ref, adc_ref, gdc_ref,
                 rl_ref, kl_ref, vl_ref, wdl_ref, adl_ref, gdl_ref,
                 mur_ref, muk_ref, muv_ref, mux_ref, wup_ref, aup_ref, gup_ref, vec_ref,
                 incl_ref, strict_ref, lvl_ref, hm_ref, bdm_ref, trit_ref,
                 oc_ref, ol_ref,
                 vs_ref, gs_ref, bvs_ref, dir_ref, ls_ref, p1_ref, yf_ref, yb_ref):
    tc, tl = rc_ref.shape[0], rl_ref.shape[0]
    tall = tc + tl
    n_ctx, n_all = tc // CH, tall // CH
    grp = _group_size(n_all)
    m0, m1 = hm_ref[0].astype(BF16), hm_ref[1].astype(BF16)
    m0w = jnp.concatenate([m0, m0], axis=1)
    m1w = jnp.concatenate([m1, m1], axis=1)
    bdm = bdm_ref[...]

    kkw, kaw, rkw = vec_ref[0:1, :], vec_ref[1:2, :], vec_ref[2:3, :]
    for (_, uo, n) in _row_tiles(tc, tl):
        ctx_tile = uo < tc
        lo = uo if ctx_tile else uo - tc
        t_len = tc if ctx_tile else tl
        rowi = lax.broadcasted_iota(jnp.int32, (n, LANES), 0)

        def lerp(src_c, src_l, mu):
            src = src_c if ctx_tile else src_l
            x = src[lo:lo + n, :]
            prev = src[lo - 1:lo, :] if lo > 0 else jnp.zeros((1, LANES), F32)
            nxt = src[lo + n:lo + n + 1, :] if lo + n < t_len else jnp.zeros((1, LANES), F32)
            xm = jnp.where(rowi == 0, prev, pltpu.roll(x, 1, 0))
            xp = jnp.where(rowi == n - 1, nxt, pltpu.roll(x, n - 1, 0))
            return x + mu * (0.5 * (xm + xp) - x)
        r = lerp(rc_ref, rl_ref, mur_ref[...])
        k = lerp(kc_ref, kl_ref, muk_ref[...])
        v = lerp(vc_ref, vl_ref, muv_ref[...])
        wd = lerp(wdc_ref, wdl_ref, mux_ref[0:1, :])
        ad = lerp(adc_ref, adl_ref, mux_ref[1:2, :])
        gd = lerp(gdc_ref, gdl_ref, mux_ref[2:3, :])
        tw = jnp.tanh(wd)
        kkv = k * kkw
        kk = kkv * lax.rsqrt(_mm(kkv * kkv, bdm) + 1e-6)
        ksum = jnp.zeros_like(k)
        for d in range(2):
            lw = -0.6065306597126334 * _sigmoid(vec_ref[3 + d:4 + d, :] + _mm(tw, wup_ref[d]))
            a = _sigmoid(vec_ref[5 + d:6 + d, :] + _mm(ad, aup_ref[d]))
            kdir = k * (1.0 + (a - 1.0) * kaw)
            ksum = ksum + kdir
            cum = _mm_sel_l(trit_ref[d, 0:n, 0:n], lw)
            einv = jnp.exp(-cum)
            dir_ref[d, 0, uo:uo + n, :] = cum
            dir_ref[d, 1, uo:uo + n, :] = kk * jnp.exp(cum - lw)
            dir_ref[d, 2, uo:uo + n, :] = r * jnp.exp(cum)
            dir_ref[d, 3, uo:uo + n, :] = kk * a * einv
            dir_ref[d, 4, uo:uo + n, :] = kdir * einv
        bonus = _mm_sel_r(r * ksum * rkw, bdm)
        vs_ref[uo:uo + n, :] = v
        gs_ref[uo:uo + n, :] = _mm(_sigmoid(gd), gup_ref[...])
        bvs_ref[uo:uo + n, :] = bonus * v

    n_grp = n_all // grp
    pd = [(i, d) for i in range(grp) for d in range(2)]
    chunk_rows = lambda gi, i, d: _chunk_rows(gi, grp, i, d, n_ctx, n_all)

    def stage_a(gi, slot):
        def finish(pi, pdir, m, vbd):
            ls_ref[slot, pi, pdir, 0:CH, :] = m[:CH, :LANES] * strict_ref[pdir]
            ls_ref[slot, pi, pdir, CH:2 * CH, :] = _mm(m[:CH, LANES:] * strict_ref[pdir], vbd)
            ls_ref[slot, pi, pdir, 2 * CH:3 * CH, :] = m[CH:, :LANES] * incl_ref[pdir]
            ls_ref[slot, pi, pdir, 3 * CH:4 * CH, :] = _mm(m[CH:, LANES:] * incl_ref[pdir], vbd)
        pending = None
        for i, d in pd:
            rows = chunk_rows(gi, i, d)
            kkq, rq, binv, kinv = (dir_ref[d, j, rows, :] for j in range(1, 5))
            lhs = jnp.concatenate([kkq, rq], axis=0)
            rhs = jnp.concatenate([_bd(binv, m0, m1), _bd(kinv, m0, m1)], axis=0)
            cur = (i, d, _mm_nt(lhs, rhs), _bd(vs_ref[rows, :], m0, m1))
            if pending is not None:
                finish(*pending)
            pending = cur
            yield
        finish(*pending)
        yield

    def stage_b(gi, slot):
        mats = [ls_ref[slot, i, d, 0:CH, :] for i, d in pd]
        es = []
        yield from _inv_levels(mats, [d for _, d in pd], lvl_ref, m0, m1, es)
        sols = []
        for e, (i, d) in zip(es, pd):
            rows = chunk_rows(gi, i, d)
            rhs = jnp.concatenate([dir_ref[d, 1, rows, :], ls_ref[slot, i, d, CH:2 * CH, :]], axis=1)
            sols.append(rhs + _mm(e, _bd(rhs, m0w, m1w)))
        yield
        for sol, (i, d) in zip(sols, pd):
            rows = chunk_rows(gi, i, d)
            cum = dir_ref[d, 0, rows, :]
            etot = jnp.exp(cum[CH - 1:CH, :] if d == 0 else cum[0:1, :])
            bdec, kdec = dir_ref[d, 3, rows, :] * etot, dir_ref[d, 4, rows, :] * etot
            ar = _mm(ls_ref[slot, i, d, 2 * CH:3 * CH, :], _bd(sol, m0w, m1w))
            pmat = _mm_tn(sol[:, :LANES], bdec)
            nmat = _mm_tn(jnp.concatenate([-sol[:, LANES:], vs_ref[rows, :]], axis=0),
                          jnp.concatenate([bdec, kdec], axis=0))
            p1_ref[slot, i, d, 0:LANES, :] = pmat * bdm
            p1_ref[slot, i, d, LANES:2 * LANES, :] = nmat * bdm
            p1_ref[slot, i, d, 2 * LANES:2 * LANES + CH, :] = dir_ref[d, 2, rows, :] - ar[:, :LANES]
            p1_ref[slot, i, d, 2 * LANES + CH:2 * LANES + 2 * CH, :] = (
                ls_ref[slot, i, d, 3 * CH:4 * CH, :] - ar[:, LANES:])
            p1_ref[slot, i, d, 2 * LANES + 2 * CH:2 * LANES + 2 * CH + SUBLANES, :] = jnp.broadcast_to(
                etot, (SUBLANES, LANES))
        yield

    def stage_c(gi, slot, st):
        for i, d in pd:
            rows = chunk_rows(gi, i, d)
            pmat = p1_ref[slot, i, d, 0:LANES, :]
            nmat = p1_ref[slot, i, d, LANES:2 * LANES, :]
            rmat = p1_ref[slot, i, d, 2 * LANES:2 * LANES + CH, :]
            ymat = p1_ref[slot, i, d, 2 * LANES + CH:2 * LANES + 2 * CH, :]
            etot = p1_ref[slot, i, d, 2 * LANES + 2 * CH:2 * LANES + 2 * CH + 1, :]
            yref = yf_ref if d == 0 else yb_ref
            yref[rows, :] = _mm_nt(rmat, st[d]) + ymat
            st[d] = st[d] * etot - _mm(st[d], pmat) + nmat
            yield

    z = jnp.zeros((LANES, LANES), F32)
    _run_pipeline(n_grp, stage_a, stage_b, stage_c, (z, z), b_before_a=False)

    gnw, gnb = vec_ref[7:8, :], vec_ref[8:9, :]
    for (po, uo, n) in _row_tiles(tc, tl):
        y = yf_ref[uo:uo + n, :] + yb_ref[uo:uo + n, :]
        mean = _mm_sel_r(y, bdm) * (1.0 / HD)
        yc = y - mean
        var = _mm(yc * yc, bdm) * (1.0 / HD)
        out = (yc * lax.rsqrt(var + RWKV_GN_EPS) * gnw + gnb + bvs_ref[uo:uo + n, :]) * gs_ref[uo:uo + n, :]
        if uo < tc:
            oc_ref[uo:uo + n, :] = out
        else:
            ol_ref[uo - tc:uo - tc + n, :] = out


def _rwkv_mixer(rw_c, rw_l, mu, w0, w_up, a0, a_up, g_up, k_k, k_a, r_k, gn_w, gn_b, consts):
    bsz, tc, _ = rw_c.shape
    tl = rw_l.shape[1]
    tall = tc + tl
    n_all = tall // CH
    W = RWKV_WIDTH
    nb = W // LANES
    zr = jnp.zeros((RWKV_RANK, W), F32)
    pad_dir = lambda u: jnp.stack([jnp.concatenate([u[0], zr], axis=0), jnp.concatenate([zr, u[1]], axis=0)])
    wup, aup = pad_dir(w_up), pad_dir(a_up)
    vec = _rows([k_k, k_a, r_k.reshape(-1), w0[0], w0[1], a0[0], a0[1], gn_w, gn_b], 16)
    mu_rkv = mu[:3 * W].reshape(3, 1, W)
    mux = jnp.pad(mu[3 * W:].reshape(3, LANES), ((0, SUBLANES - 3), (0, 0)))
    cm = lambda *blk: pl.BlockSpec(blk, lambda b, p: (0,) * len(blk))

    def tok(t, col0, per_pair=True):
        if per_pair:
            return pl.BlockSpec((None, t, LANES), lambda b, p: (b, 0, col0 + p))
        return pl.BlockSpec((None, t, LANES), lambda b, p: (b, 0, col0))
    stream = lambda t: [tok(t, 0), tok(t, nb), tok(t, 2 * nb), tok(t, 3 * nb, False),
                        tok(t, 3 * nb + 1, False), tok(t, 3 * nb + 2, False)]
    in_specs = stream(tc) + stream(tl) + [
        pl.BlockSpec((None, 1, LANES), lambda b, p: (0, 0, p)),
        pl.BlockSpec((None, 1, LANES), lambda b, p: (1, 0, p)),
        pl.BlockSpec((None, 1, LANES), lambda b, p: (2, 0, p)),
        cm(SUBLANES, LANES),
        pl.BlockSpec((2, LANES, LANES), lambda b, p: (0, 0, p)),
        pl.BlockSpec((2, LANES, LANES), lambda b, p: (0, 0, p)),
        pl.BlockSpec((RWKV_G_RANK, LANES), lambda b, p: (0, p)),
        pl.BlockSpec((16, LANES), lambda b, p: (0, p)),
        cm(2, CH, LANES), cm(2, CH, LANES), cm(2, 6, CH, LANES), cm(2, 1, LANES), cm(LANES, LANES),
        cm(2, ROWT, ROWT)]
    scratch = [pltpu.VMEM((tall, LANES), F32)] * 3 + [
        pltpu.VMEM((2, 5, tall, LANES), F32),
        pltpu.VMEM((2, _group_size(n_all), 2, 4 * CH, LANES), F32),
        pltpu.VMEM((2, _group_size(n_all), 2, 2 * LANES + 2 * CH + SUBLANES, LANES), F32),
        pltpu.VMEM((tall, LANES), F32), pltpu.VMEM((tall, LANES), F32)]
    return pl.pallas_call(
        _rwkv_kernel,
        grid=(bsz, N_PAIRS),
        in_specs=in_specs,
        out_specs=[pl.BlockSpec((None, tc, LANES), lambda b, p: (b, 0, p)),
                   pl.BlockSpec((None, tl, LANES), lambda b, p: (b, 0, p))],
        out_shape=[jax.ShapeDtypeStruct((bsz, tc, W), F32),
                   jax.ShapeDtypeStruct((bsz, tl, W), F32)],
        scratch_shapes=scratch,
        compiler_params=pltpu.CompilerParams(dimension_semantics=("arbitrary", "arbitrary"),
                                             vmem_limit_bytes=VMEM_LIMIT),
        name="rwkv_mixer",
    )(*([rw_c] * 6), *([rw_l] * 6), mu_rkv, mu_rkv, mu_rkv, mux, wup, aup, g_up, vec,
      consts["incl"], consts["strict"], consts["lvl"], consts["hm"], consts["bdm"], consts["tri_t"])


def _lru_kernel(xc_ref, gc_ref, xl_ref, gl_ref, cw_ref, vec_ref, wcat_ref, bcat_ref,
                oc_ref, ol_ref, xp_ref, ab_ref, hf_ref, hb_ref):
    tc, tl = xc_ref.shape[0], xl_ref.shape[0]
    tall = tc + tl
    rows = tl // GRID_W
    W = xc_ref.shape[1]
    z = jnp.zeros((GAP, W), F32)
    xp_ref[0:GAP, :] = z
    xp_ref[GAP:GAP + tc, :] = xc_ref[...]
    xp_ref[GAP + tc:2 * GAP + tc, :] = z
    base = 2 * GAP + tc
    for c in range(GRID_W):
        xp_ref[base + c * rows:base + (c + 1) * rows, :] = xl_ref[pl.ds(c, rows, stride=GRID_W), :]
    xp_ref[base + tl:base + tl + GAP, :] = z
    cb = vec_ref[0:1, :]
    for (po, uo, n) in _row_tiles(tc, tl):
        acc = cb + cw_ref[0:1, :] * xp_ref[po - 2:po - 2 + n, :]
        for tap in range(1, CONV_W):
            acc = acc + cw_ref[tap:tap + 1, :] * xp_ref[po - 2 + tap:po - 2 + tap + n, :]
        gates = _mm(acc, wcat_ref[...]) + bcat_ref[...]
        for d in range(2):
            rg = _sigmoid(gates[:, (2 * d) * W:(2 * d + 1) * W])
            ig = _sigmoid(gates[:, (2 * d + 1) * W:(2 * d + 2) * W])
            log_a = -LRU_C * rg * _softplus(-vec_ref[1 + d:2 + d, :])
            a = jnp.exp(log_a)
            mult = jnp.sqrt(-jnp.tanh(log_a) * (1.0 + a * a))
            ab_ref[d, 0, uo:uo + n, :] = a
            ab_ref[d, 1, uo:uo + n, :] = mult * (ig * acc)

    sub = lax.broadcasted_iota(jnp.int32, (SUBLANES, W), 0)
    n_tiles_c, n_tiles = tc // SUBLANES, tall // SUBLANES

    def tile_scan(a, b, d):
        for sh in (1, 2, 4):
            if d == 0:
                ok = sub >= sh
                a_s = jnp.where(ok, pltpu.roll(a, sh, 0), 1.0)
                b_s = jnp.where(ok, pltpu.roll(b, sh, 0), 0.0)
            else:
                ok = sub < SUBLANES - sh
                a_s = jnp.where(ok, pltpu.roll(a, SUBLANES - sh, 0), 1.0)
                b_s = jnp.where(ok, pltpu.roll(b, SUBLANES - sh, 0), 0.0)
            b = b + a * b_s
            a = a * a_s
        return a, b

    unroll = 4 if (n_tiles_c % 4 == 0 and n_tiles % 4 == 0) else 1

    def scan_body(s4, carry):
        hf, hb = carry
        tiles = []
        for j in range(unroll):
            s = s4 * unroll + j
            r0 = pl.multiple_of(s * SUBLANES, SUBLANES)
            tb = jnp.where(s < n_tiles_c, n_tiles_c - 1 - s, n_tiles + n_tiles_c - 1 - s)
            r1 = pl.multiple_of(tb * SUBLANES, SUBLANES)
            tiles.append((r0, tile_scan(ab_ref[0, 0, pl.ds(r0, SUBLANES), :], ab_ref[0, 1, pl.ds(r0, SUBLANES), :], 0),
                          r1, tile_scan(ab_ref[1, 0, pl.ds(r1, SUBLANES), :], ab_ref[1, 1, pl.ds(r1, SUBLANES), :], 1)))
        for r0, (af, bf), r1, (ab, bb) in tiles:
            h = bf + af * hf
            hf_ref[pl.ds(r0, SUBLANES), :] = h
            hf = jnp.broadcast_to(h[SUBLANES - 1:SUBLANES, :], (SUBLANES, W))
            h = bb + ab * hb
            hb_ref[pl.ds(r1, SUBLANES), :] = h
            hb = jnp.broadcast_to(h[0:1, :], (SUBLANES, W))
        return hf, hb
    z8 = jnp.zeros((SUBLANES, W), F32)
    lax.fori_loop(0, n_tiles // unroll, scan_body, (z8, z8))

    oc_ref[...] = (hf_ref[0:tc, :] + hb_ref[0:tc, :]) * _gelu_tanh(gc_ref[...])
    for c in range(GRID_W):
        h = hf_ref[tc + c * rows:tc + (c + 1) * rows, :] + hb_ref[tc + c * rows:tc + (c + 1) * rows, :]
        ol_ref[pl.ds(c, rows, stride=GRID_W), :] = h * _gelu_tanh(gl_ref[pl.ds(c, rows, stride=GRID_W), :])


def _lru_mixer(x_c, g_c, x_l, g_l, conv_w, conv_b, w_a, b_a, w_x, b_x, lam):
    bsz, tc, W = x_c.shape
    tl = x_l.shape[1]
    tall = tc + tl
    nh = W // LANES
    bph = LANES // LRU_BW
    w6 = jnp.stack([w_a, w_x], axis=1).reshape(2, 2, nh, bph, LRU_BW, LRU_BW)
    wcat = jnp.einsum("dghnrc,nm->hnrdgmc", w6, jnp.eye(bph, dtype=F32)).reshape(nh, LANES, 4 * LANES)
    bcat = jnp.stack([b_a, b_x], axis=1).reshape(2, 2, nh, LANES).transpose(2, 0, 1, 3).reshape(nh, 1, 4 * LANES)
    vec = _rows([conv_b, lam[0], lam[1]], SUBLANES)
    tok = lambda t: pl.BlockSpec((None, t, LANES), lambda b, h: (b, 0, h))
    return pl.pallas_call(
        _lru_kernel,
        grid=(bsz, nh),
        in_specs=[tok(tc), tok(tc), tok(tl), tok(tl),
                  pl.BlockSpec((CONV_W, LANES), lambda b, h: (0, h)),
                  pl.BlockSpec((SUBLANES, LANES), lambda b, h: (0, h)),
                  pl.BlockSpec((None, LANES, 4 * LANES), lambda b, h: (h, 0, 0)),
                  pl.BlockSpec((None, 1, 4 * LANES), lambda b, h: (h, 0, 0))],
        out_specs=[tok(tc), tok(tl)],
        out_shape=[jax.ShapeDtypeStruct((bsz, tc, W), F32), jax.ShapeDtypeStruct((bsz, tl, W), F32)],
        scratch_shapes=[pltpu.VMEM((tall + 3 * GAP, LANES), F32),
                        pltpu.VMEM((2, 2, tall, LANES), F32),
                        pltpu.VMEM((tall, LANES), F32), pltpu.VMEM((tall, LANES), F32)],
        compiler_params=pltpu.CompilerParams(dimension_semantics=("arbitrary", "arbitrary"),
                                             vmem_limit_bytes=VMEM_LIMIT),
        name="lru_mixer",
    )(x_c, g_c, x_l, g_l, conv_w, vec, wcat.astype(BF16), bcat)


def _finish_kernel(x_ref, gdn_ref, lru_ref, rwk_ref, m2_ref, m3_ref, m4_ref, m5_ref, nrm_ref,
                   wo_ref, up_ref, dn_ref, o_ref):
    x = x_ref[...]
    o = (jnp.dot(gdn_ref[...].astype(BF16), wo_ref[0:GDN_WIDTH, :], preferred_element_type=F32)
         + jnp.dot(lru_ref[...].astype(BF16), wo_ref[GDN_WIDTH:GDN_WIDTH + LRU_WIDTH, :],
                   preferred_element_type=F32)
         + jnp.dot(rwk_ref[...].astype(BF16), wo_ref[GDN_WIDTH + LRU_WIDTH:, :], preferred_element_type=F32))
    x = x + m2_ref[...] * _rms(o, nrm_ref[0:1, :])
    h = (_rms(x, nrm_ref[1:2, :]) * (1.0 + m4_ref[...]) + m3_ref[...]).astype(BF16)
    f = jnp.zeros_like(x)
    fc = 1024
    for j in range(D_FF // fc):
        a = jnp.maximum(jnp.dot(h, up_ref[:, j * fc:(j + 1) * fc], preferred_element_type=F32), 0.0)
        f = f + jnp.dot((a * a).astype(BF16), dn_ref[j * fc:(j + 1) * fc, :], preferred_element_type=F32)
    o_ref[...] = x + m5_ref[...] * _rms(f, nrm_ref[2:3, :])


def _finish(x2, gdn, lru, rwk, mod_rows, rows_per_mod, norms, wo_bf, up_bf, dn_bf, layer):
    n = x2.shape[0]
    tm = DENSE_TM
    tiles_per_mod = rows_per_mod // tm
    modspec = lambda k: pl.BlockSpec((None, 1, D_MODEL), lambda i: (6 * (i // tiles_per_mod) + k, 0, 0))
    cm = lambda *blk: pl.BlockSpec((None,) + blk, lambda i: (layer,) + (0,) * len(blk),
                                   pipeline_mode=pl.Buffered(1))
    return pl.pallas_call(
        _finish_kernel,
        grid=(n // tm,),
        in_specs=[pl.BlockSpec((tm, D_MODEL), lambda i: (i, 0)),
                  pl.BlockSpec((tm, GDN_WIDTH), lambda i: (i, 0)),
                  pl.BlockSpec((tm, LRU_WIDTH), lambda i: (i, 0)),
                  pl.BlockSpec((tm, RWKV_WIDTH), lambda i: (i, 0)),
                  modspec(2), modspec(3), modspec(4), modspec(5),
                  cm(SUBLANES, D_MODEL), cm(D_MODEL, D_MODEL), cm(D_MODEL, D_FF), cm(D_FF, D_MODEL)],
        out_specs=pl.BlockSpec((tm, D_MODEL), lambda i: (i, 0)),
        out_shape=jax.ShapeDtypeStruct((n, D_MODEL), F32),
        compiler_params=pltpu.CompilerParams(dimension_semantics=("arbitrary",),
                                             vmem_limit_bytes=VMEM_LIMIT),
        name="finish",
    )(x2, gdn, lru, rwk, mod_rows, mod_rows, mod_rows, mod_rows, norms, wo_bf, up_bf, dn_bf)


def _arrange_w_in(w):
    s = np.cumsum([0, 3 * GDN_WIDTH, GDN_WIDTH, 2 * GDN_HEADS, 2 * GDN_HEADS, LRU_WIDTH, LRU_WIDTH, RWKV_IN])
    pad = jnp.zeros(w.shape[:-1] + (BA_W - 4 * GDN_HEADS,), w.dtype)
    return jnp.concatenate([w[..., s[0]:s[4]], pad, w[..., s[4]:]], axis=-1)


def kernel(x, c, ctx, c_ctx, ada_w, ada_b, norm_mix_pre, norm_mix_post, norm_ffn_pre, norm_ffn_post, w_in, gdn_conv, gdn_a_log, gdn_dt_bias, gdn_norm, lru_conv, lru_conv_b, lru_wa, lru_ba, lru_wx, lru_bx, lru_lambda, rwkv_mu, rwkv_w0, rwkv_w_up, rwkv_a0, rwkv_a_up, rwkv_g_up, rwkv_k_k, rwkv_k_a, rwkv_r_k, rwkv_gn_w, rwkv_gn_b, w_out, ffn_up, ffn_down):
    bsz, tl, _ = x.shape
    tc = ctx.shape[1]
    depth = w_in.shape[0]
    consts = _device_consts()

    cvec = jnp.pad(jnp.concatenate([c, c_ctx[None, :]], axis=0), ((0, 16 - bsz - 1), (0, 0)))
    mods = _ada_mod(cvec, ada_w, ada_b).reshape(depth, 16, 6, D_MODEL)

    w_in_bf = _arrange_w_in(w_in).astype(BF16)
    wo_bf, up_bf, dn_bf = w_out.astype(BF16), ffn_up.astype(BF16), ffn_down.astype(BF16)
    g_pre = norm_mix_pre.reshape(depth, 1, D_MODEL)
    norms = jnp.pad(jnp.stack([norm_mix_post, norm_ffn_pre, norm_ffn_post], axis=1),
                    ((0, 0), (0, SUBLANES - 3), (0, 0)))

    xl = x.reshape(bsz * tl, D_MODEL)
    xc = ctx.reshape(bsz * tc, D_MODEL)
    for i in range(depth):
        mod_l = mods[i, 0:bsz].reshape(bsz * 6, 1, D_MODEL)
        mod_c = mods[i, bsz:bsz + 1].reshape(6, 1, D_MODEL)
        pl_ = _inproj(xl, mod_l, tl, g_pre, w_in_bf, i)
        pc_ = _inproj(xc, mod_c, bsz * tc, g_pre, w_in_bf, i)
        r3 = lambda a, t: a.reshape(bsz, t, a.shape[-1])
        qkv_l, z_l, ba_l, lx_l, lg_l, rw_l = (r3(a, tl) for a in pl_)
        qkv_c, z_c, ba_c, lx_c, lg_c, rw_c = (r3(a, tc) for a in pc_)

        gdn_c, gdn_l = _gdn_mixer(qkv_c, z_c, ba_c, qkv_l, z_l, ba_l, gdn_conv[i], gdn_a_log[i],
                                  gdn_dt_bias[i], gdn_norm[i], consts)
        lru_c, lru_l = _lru_mixer(lx_c, lg_c, lx_l, lg_l, lru_conv[i], lru_conv_b[i], lru_wa[i], lru_ba[i],
                                  lru_wx[i], lru_bx[i], lru_lambda[i])
        rwk_c, rwk_l = _rwkv_mixer(rw_c, rw_l, rwkv_mu[i], rwkv_w0[i], rwkv_w_up[i], rwkv_a0[i],
                                   rwkv_a_up[i], rwkv_g_up[i], rwkv_k_k[i], rwkv_k_a[i], rwkv_r_k[i],
                                   rwkv_gn_w[i], rwkv_gn_b[i], consts)

        f2 = lambda a: a.reshape(-1, a.shape[-1])
        xl = _finish(xl, f2(gdn_l), f2(lru_l), f2(rwk_l), mod_l, tl, norms, wo_bf, up_bf, dn_bf, i)
        if i < depth - 1:
            xc = _finish(xc, f2(gdn_c), f2(lru_c), f2(rwk_c), mod_c, bsz * tc, norms, wo_bf, up_bf, dn_bf, i)
    return xl.reshape(bsz, tl, D_MODEL)
```

```python
import functools

import numpy as np
import jax
import jax.numpy as jnp
from jax import lax
from jax.experimental import pallas as pl
from jax.experimental.pallas import tpu as pltpu

F32 = jnp.float32
BF16 = jnp.bfloat16

LANES = 128
SUBLANES = 8
VMEM_LIMIT = 56 * 1024 * 1024

D_MODEL = 1024
DEPTH = 2
GRID_W = 64
CONV_W = 4
EPS = 1e-6
D_FF = 4 * D_MODEL
HD = 64
GDN_WIDTH = 3 * D_MODEL // 8
GDN_HEADS = GDN_WIDTH // HD
LRU_WIDTH = D_MODEL // 4
LRU_BLOCKS = 4
LRU_BW = LRU_WIDTH // LRU_BLOCKS
LRU_C = 8.0
RWKV_WIDTH = D_MODEL - GDN_WIDTH - LRU_WIDTH
RWKV_HEADS = RWKV_WIDTH // HD
RWKV_RANK = 64
RWKV_G_RANK = 128
RWKV_GN_EPS = 6.4e-4
RWKV_IN = 3 * RWKV_WIDTH + 2 * RWKV_RANK + 2 * RWKV_RANK + RWKV_G_RANK
N_PAIRS = GDN_HEADS // 2
BA_W = LANES
P_OFF = np.cumsum([0, 3 * GDN_WIDTH, GDN_WIDTH, BA_W, LRU_WIDTH, LRU_WIDTH, RWKV_IN])
D_INP = int(P_OFF[-1])

CH = 64
ROWT = 256
DENSE_TM = 512
GAP = SUBLANES


def _mm(a, b):
    return jnp.dot(a.astype(BF16), b.astype(BF16), preferred_element_type=F32)


def _mm_nt(a, b):
    return lax.dot_general(a.astype(BF16), b.astype(BF16), (((1,), (1,)), ((), ())),
                           preferred_element_type=F32)


def _mm_tn(a, b):
    return lax.dot_general(a.astype(BF16), b.astype(BF16), (((0,), (0,)), ((), ())),
                           preferred_element_type=F32)


def _split2(x):
    hi = x.astype(BF16)
    lo = (x - hi.astype(F32)).astype(BF16)
    return hi, lo


def _mm_sel_l(m01, x):
    mb = m01.astype(BF16)
    h, l = _split2(x)
    d = functools.partial(jnp.dot, preferred_element_type=F32)
    return d(mb, h) + d(mb, l)


def _mm_sel_r(x, m01):
    mb = m01.astype(BF16)
    h, l = _split2(x)
    d = functools.partial(jnp.dot, preferred_element_type=F32)
    return d(h, mb) + d(l, mb)


def _sigmoid(x):
    return 0.5 * jnp.tanh(0.5 * x) + 0.5


def _silu_of_twice(h):
    return h + h * jnp.tanh(h)


def _silu(x):
    return _silu_of_twice(0.5 * x)


def _softplus(x):
    return jnp.maximum(x, 0.0) + jnp.log(1.0 + jnp.exp(-jnp.abs(x)))


def _gelu_tanh(x):
    return 0.5 * x * (1.0 + jnp.tanh(0.7978845608028654 * (x + 0.044715 * (x * x * x))))


def _bd(y, m0, m1):
    yb = y.astype(BF16)
    return jnp.concatenate([yb * m0, yb * m1], axis=0)


def _pair_consts():
    i = np.arange(CH)[:, None]
    j = (np.arange(LANES) % HD)[None, :]
    incl = np.stack([i >= j, i <= j]).astype(np.float32)
    strict = np.stack([i > j, i < j]).astype(np.float32)
    lvls = []
    for d in range(2):
        per = []
        for m in (1, 2, 4, 8, 16, 32):
            same = (i // (2 * m)) == (j // (2 * m))
            lo_i, lo_j = (i % (2 * m)) < m, (j % (2 * m)) < m
            off = same & (~lo_i) & lo_j if d == 0 else same & lo_i & (~lo_j)
            per.append(off)
        lvls.append(np.stack(per))
    lvl = np.stack(lvls).astype(np.float32)
    lane = np.arange(LANES)
    hm = np.stack([lane < HD, lane >= HD]).astype(np.float32)[:, None, :]
    bdm = ((np.arange(LANES)[:, None] // HD) == (lane[None, :] // HD)).astype(np.float32)
    r = np.arange(ROWT)[:, None]
    c = np.arange(ROWT)[None, :]
    same = (r // CH) == (c // CH)
    tri_t = np.stack([same & (r >= c), same & (r <= c)]).astype(np.float32)
    return dict(incl=incl, strict=strict, lvl=lvl, hm=hm, bdm=bdm, tri_t=tri_t)


def _device_consts():
    out = {k: jnp.asarray(v) for k, v in _pair_consts().items()}
    out["tri_t"] = out["tri_t"].astype(BF16)
    return out


def _inv_levels(mats, dirs, lvl_ref, m0, m1, out):
    es = [-(a * lvl_ref[d, 0]) for a, d in zip(mats, dirs)]
    for k in range(1, 6):
        offs = [a * lvl_ref[d, k] for a, d in zip(mats, dirs)]
        xs = [off + p for off, p in zip(offs, _mm_pairs(es, [_bd(off, m0, m1) for off in offs]))]
        yield
        es = [e - x - p for e, x, p in zip(es, xs, _mm_pairs(xs, [_bd(e, m0, m1) for e in es]))]
        yield
    out.extend(es)


def _mm_pairs(lhs, rhs):
    out = []
    for j in range(0, len(lhs) - 1, 2):
        r = jnp.dot(jnp.concatenate([lhs[j], lhs[j + 1]], axis=0).astype(BF16),
                    jnp.concatenate([rhs[j], rhs[j + 1]], axis=1), preferred_element_type=F32)
        out += [r[:CH, :LANES], r[CH:, LANES:]]
    if len(lhs) % 2:
        out.append(_mm(lhs[-1], rhs[-1]))
    return out


def _rows(vectors, n_rows):
    m = jnp.stack([v.astype(F32) for v in vectors])
    return jnp.pad(m, ((0, n_rows - m.shape[0]), (0, 0)))


def _round_robin(gens):
    gens = list(gens)
    while gens:
        alive = []
        for g in gens:
            try:
                next(g)
                alive.append(g)
            except StopIteration:
                pass
        gens = alive


def _group_size(n_all):
    for g in (6, 4, 3, 2):
        if n_all % g == 0:
            return g
    return 1


def _chunk_order(s, n_ctx, n_all, d):
    if d == 0:
        return s
    if isinstance(s, int):
        return n_ctx - 1 - s if s < n_ctx else n_all + n_ctx - 1 - s
    return jnp.where(s < n_ctx, n_ctx - 1 - s, n_all + n_ctx - 1 - s)


def _chunk_rows(gi, grp, i, d, n_ctx, n_all):
    c = _chunk_order(gi * grp + i, n_ctx, n_all, d)
    return pl.ds(c * CH if isinstance(c, int) else pl.multiple_of(c * CH, CH), CH)


def _run_pipeline(n_grp, stage_a, stage_b, stage_c, st, b_before_a=True, fill=((), ()), drain=((), ())):
    def step(t, st, do_a, do_b, do_c, extra=()):
        par = t % 2 if isinstance(t, int) else lax.rem(t, 2)
        st = list(st)
        gens = []
        if do_c:
            gens.append(stage_c(t - 2, par, st))
        ab = ([stage_b(t - 1, 1 - par)] if do_b else []) + ([stage_a(t, par)] if do_a else [])
        _round_robin(gens + (ab if b_before_a else ab[::-1]) + list(extra))
        return tuple(st)
    assert n_grp >= 2
    st = step(0, st, True, False, False, fill[0])
    st = step(1, st, True, True, False, fill[1])
    st = lax.fori_loop(2, n_grp, lambda t, s: step(t, s, True, True, True), st)
    st = step(n_grp, st, False, True, True, drain[0])
    return step(n_grp + 1, st, False, False, True, drain[1])


def _tile_schedule(tc, tl, grp):
    n_ctx, n_all = tc // CH, (tc + tl) // CH
    first, last = {}, {}
    for s in range(n_all):
        g = s // grp
        for d in range(2):
            c = _chunk_order(s, n_ctx, n_all, d)
            first[c] = min(first.get(c, g), g)
            last[c] = max(last.get(c, g + 2), g + 2)
    out = []
    for (_, uo, n) in _row_tiles(tc, tl):
        cs = range(uo // CH, (uo + n) // CH)
        out.append((min(first[c] for c in cs), max(last[c] for c in cs)))
    return out


def _chain(gens):
    for g in gens:
        yield from g


def _fill_padded(dst_ref, src_c_ref, src_l_ref, tc, tl):
    w = dst_ref.shape[1]
    z = jnp.zeros((GAP, w), F32)
    dst_ref[0:GAP, :] = z
    dst_ref[GAP:GAP + tc, :] = src_c_ref[...]
    dst_ref[GAP + tc:2 * GAP + tc, :] = z
    dst_ref[2 * GAP + tc:2 * GAP + tc + tl, :] = src_l_ref[...]
    dst_ref[2 * GAP + tc + tl:3 * GAP + tc + tl, :] = z


def _row_tiles(tc, tl):
    out = []
    for base_p, base_u, n in ((GAP, 0, tc), (2 * GAP + tc, tc, tl)):
        for t0 in range(0, n, ROWT):
            out.append((base_p + t0, base_u + t0, min(ROWT, n - t0)))
    return out


def _ada_kernel(c_ref, w_ref, b_ref, o_ref):
    c = c_ref[...]
    o_ref[...] = _mm(_silu(c), w_ref[...]) + b_ref[...]


def _ada_mod(cvec, ada_w, ada_b):
    L = ada_w.shape[0]
    n = ada_w.shape[2]
    tn = 1536
    return pl.pallas_call(
        _ada_kernel,
        grid=(L, n // tn),
        in_specs=[pl.BlockSpec((16, D_MODEL), lambda l, j: (0, 0)),
                  pl.BlockSpec((None, D_MODEL, tn), lambda l, j: (l, 0, j)),
                  pl.BlockSpec((None, 1, tn), lambda l, j: (l, 0, j))],
        out_specs=pl.BlockSpec((None, 16, tn), lambda l, j: (l, 0, j)),
        out_shape=jax.ShapeDtypeStruct((L, 16, n), F32),
        compiler_params=pltpu.CompilerParams(dimension_semantics=("arbitrary", "arbitrary"),
                                             vmem_limit_bytes=VMEM_LIMIT),
        name="ada_mod",
    )(cvec, ada_w, ada_b.reshape(L, 1, n))


def _rms(x, g):
    return x * lax.rsqrt(jnp.mean(x * x, axis=-1, keepdims=True) + EPS) * g


def _inproj_kernel(x_ref, sh_ref, sc_ref, g_ref, w_ref, qkv_ref, z_ref, ba_ref, lx_ref, lg_ref, rw_ref):
    h = _rms(x_ref[...], g_ref[...]) * (1.0 + sc_ref[...]) + sh_ref[...]
    p = jnp.dot(h.astype(BF16), w_ref[...], preferred_element_type=F32)
    for ref, k in zip((qkv_ref, z_ref, ba_ref, lx_ref, lg_ref, rw_ref), range(6)):
        ref[...] = p[:, int(P_OFF[k]):int(P_OFF[k + 1])]


def _inproj(x2, mod_rows, rows_per_mod, g_all, w_all, layer):
    n = x2.shape[0]
    tm = DENSE_TM
    tiles_per_mod = rows_per_mod // tm
    widths = [int(P_OFF[k + 1] - P_OFF[k]) for k in range(6)]
    return pl.pallas_call(
        _inproj_kernel,
        grid=(n // tm,),
        in_specs=[pl.BlockSpec((tm, D_MODEL), lambda i: (i, 0)),
                  pl.BlockSpec((None, 1, D_MODEL), lambda i: (6 * (i // tiles_per_mod), 0, 0)),
                  pl.BlockSpec((None, 1, D_MODEL), lambda i: (6 * (i // tiles_per_mod) + 1, 0, 0)),
                  pl.BlockSpec((None, 1, D_MODEL), lambda i: (layer, 0, 0)),
                  pl.BlockSpec((None, D_MODEL, D_INP), lambda i: (layer, 0, 0), pipeline_mode=pl.Buffered(1))],
        out_specs=[pl.BlockSpec((tm, w), lambda i: (i, 0)) for w in widths],
        out_shape=[jax.ShapeDtypeStruct((n, w), F32) for w in widths],
        compiler_params=pltpu.CompilerParams(dimension_semantics=("arbitrary",),
                                             vmem_limit_bytes=VMEM_LIMIT),
        name="inproj",
    )(x2, mod_rows, mod_rows, g_all, w_all)


def _gdn_kernel(qc_ref, kc_ref, vc_ref, zc_ref, bac_ref, ql_ref, kl_ref, vl_ref, zl_ref, bal_ref,
                cwq_ref, cwk_ref, cwv_ref, aux_ref, e_ref,
                incl_ref, strict_ref, lvl_ref, hm_ref, bdm_ref, trit_ref,
                oc_ref, ol_ref,
                qp_ref, kp_ref, vp_ref, qs_ref, ks_ref, vs_ref, dir_ref, ls_ref, p1_ref, of_ref, ob_ref):
    tc, tl = qc_ref.shape[0], ql_ref.shape[0]
    tall = tc + tl
    n_ctx, n_all = tc // CH, tall // CH
    grp = _group_size(n_all)
    m0, m1 = hm_ref[0].astype(BF16), hm_ref[1].astype(BF16)
    m0w = jnp.concatenate([m0, m0], axis=1)
    m1w = jnp.concatenate([m1, m1], axis=1)
    bdm = bdm_ref[...]

    _fill_padded(qp_ref, qc_ref, ql_ref, tc, tl)
    _fill_padded(kp_ref, kc_ref, kl_ref, tc, tl)
    _fill_padded(vp_ref, vc_ref, vl_ref, tc, tl)
    lane = lax.broadcasted_iota(jnp.int32, (1, LANES), 1)
    alog, dtb = aux_ref[0:1, :], aux_ref[1:2, :]

    def prep_tile(po, uo, n):
        def conv(src, cw_ref):
            cw = 0.5 * cw_ref[...]
            acc = cw[0:1, :] * src[po - 2:po - 2 + n, :]
            for tap in range(1, CONV_W):
                acc = acc + cw[tap:tap + 1, :] * src[po - 2 + tap:po - 2 + tap + n, :]
            return _silu_of_twice(acc)
        q = conv(qp_ref, cwq_ref)
        k = conv(kp_ref, cwk_ref)
        yield
        q = q * lax.rsqrt(_mm(q * q, bdm) + 1e-6) * (HD ** -0.5)
        k = k * lax.rsqrt(_mm(k * k, bdm) + 1e-6)
        qs_ref[uo:uo + n, :] = q
        ks_ref[uo:uo + n, :] = k
        vs_ref[uo:uo + n, :] = conv(vp_ref, cwv_ref)
        yield
        ba = bac_ref[uo:uo + n, :] if uo < tc else bal_ref[uo - tc:uo - tc + n, :]
        beta = _sigmoid(ba)
        gval = -jnp.exp(alog) * _softplus(ba + dtb)
        bgv = jnp.where(lane < 2 * GDN_HEADS, beta, gval)
        ex = _mm_sel_r(bgv, e_ref[...])
        incl_t = [jnp.concatenate([incl_ref[dd]] * (n // CH), axis=0) for dd in range(2)]
        yield
        for d in range(2):
            g = ex[:, (2 + d) * LANES:(3 + d) * LANES]
            gc = _mm_sel_l(trit_ref[d, 0:n, 0:n], g)
            gt = gc.T
            gr = jnp.concatenate(
                [jnp.broadcast_to(jnp.concatenate([gt[0:1, c * CH:(c + 1) * CH], gt[HD:HD + 1, c * CH:(c + 1) * CH]],
                                                  axis=1), (CH, LANES)) for c in range(n // CH)], axis=0)
            dir_ref[d, 0, uo:uo + n, :] = ex[:, d * LANES:(d + 1) * LANES]
            dir_ref[d, 1, uo:uo + n, :] = gc
            dir_ref[d, 2, uo:uo + n, :] = jnp.exp(jnp.minimum(gc - gr, 0.0)) * incl_t[d]
            yield

    tiles = _row_tiles(tc, tl)
    sched = _tile_schedule(tc, tl, grp)
    for tile, (need, _) in zip(tiles, sched):
        if need == 0:
            _round_robin([prep_tile(*tile)])
    fill = ([_chain([prep_tile(*t) for t, (need, _) in zip(tiles, sched) if need == 1])],
            [_chain([prep_tile(*t) for t, (need, _) in zip(tiles, sched) if need >= 2])])

    n_grp = n_all // grp
    chunk_rows = lambda gi, i, d: _chunk_rows(gi, grp, i, d, n_ctx, n_all)
    pd = [(i, d) for i in range(grp) for d in range(2)]

    def stage_a(gi, slot):
        for i, d in pd:
            rows = chunk_rows(gi, i, d)
            q, k = qs_ref[rows, :], ks_ref[rows, :]
            beta, dec = dir_ref[d, 0, rows, :], dir_ref[d, 2, rows, :]
            la = _mm_nt(jnp.concatenate([k, q], axis=0), _bd(k, m0, m1))
            ls_ref[slot, i, d, 0:CH, :] = la[:CH] * dec * strict_ref[d] * beta
            ls_ref[slot, i, d, CH:2 * CH, :] = la[CH:] * dec
            yield

    def stage_b(gi, slot):
        mats = [ls_ref[slot, i, d, 0:CH, :] for i, d in pd]
        es = []
        yield from _inv_levels(mats, [d for _, d in pd], lvl_ref, m0, m1, es)
        sols, kds, egs, gls = [], [], [], []
        for e, (i, d) in zip(es, pd):
            rows = chunk_rows(gi, i, d)
            k, v = ks_ref[rows, :], vs_ref[rows, :]
            beta, gc = dir_ref[d, 0, rows, :], dir_ref[d, 1, rows, :]
            eg = jnp.exp(gc)
            rhs = jnp.concatenate([v * beta, k * beta * eg], axis=1)
            sols.append(rhs + _mm(e, _bd(rhs, m0w, m1w)))
            glast = gc[CH - 1:CH, :] if d == 0 else gc[0:1, :]
            kds.append(k * jnp.exp(glast - gc))
            egs.append(eg)
            gls.append(glast)
        yield
        for sol, kd, eg, glast, (i, d) in zip(sols, kds, egs, gls, pd):
            rows = chunk_rows(gi, i, d)
            attn = ls_ref[slot, i, d, CH:2 * CH, :]
            au = _mm(attn, _bd(sol, m0w, m1w))
            kn = _mm_tn(kd, sol)
            p1_ref[slot, i, d, 0:LANES, :] = kn[:, LANES:] * bdm
            p1_ref[slot, i, d, LANES:LANES + CH, :] = qs_ref[rows, :] * eg - au[:, LANES:]
            p1_ref[slot, i, d, LANES + CH:2 * LANES + CH, :] = kn[:, :LANES] * bdm
            p1_ref[slot, i, d, 2 * LANES + CH:2 * LANES + 2 * CH, :] = au[:, :LANES]
            p1_ref[slot, i, d, 2 * LANES + 2 * CH:2 * LANES + 2 * CH + SUBLANES, :] = jnp.broadcast_to(
                jnp.exp(glast), (SUBLANES, LANES))
        yield

    def stage_c(gi, slot, st):
        for i, d in pd:
            rows = chunk_rows(gi, i, d)
            kwq = p1_ref[slot, i, d, 0:LANES + CH, :]
            nmat = p1_ref[slot, i, d, LANES + CH:2 * LANES + CH, :]
            omat = p1_ref[slot, i, d, 2 * LANES + CH:2 * LANES + 2 * CH, :]
            egl = p1_ref[slot, i, d, 2 * LANES + 2 * CH:2 * LANES + 2 * CH + 1, :]
            ks = _mm(kwq, st[d])
            oref = of_ref if d == 0 else ob_ref
            oref[rows, :] = ks[LANES:] + omat
            st[d] = st[d] * egl - ks[:LANES] + nmat
            yield

    nw = aux_ref[2:3, :]

    def out_tile(po, uo, n):
        o = of_ref[uo:uo + n, :] + ob_ref[uo:uo + n, :]
        ms = _mm(o * o, bdm) * (1.0 / HD)
        yield
        if uo < tc:
            oc_ref[uo:uo + n, :] = o * lax.rsqrt(ms + EPS) * nw * _silu(zc_ref[uo:uo + n, :])
        else:
            lo = uo - tc
            ol_ref[lo:lo + n, :] = o * lax.rsqrt(ms + EPS) * nw * _silu(zl_ref[lo:lo + n, :])
        yield

    drain = ([_chain([out_tile(*t) for t, (_, rdy) in zip(tiles, sched) if rdy < n_grp])],
             [_chain([out_tile(*t) for t, (_, rdy) in zip(tiles, sched) if rdy == n_grp])])
    z = jnp.zeros((LANES, LANES), F32)
    _run_pipeline(n_grp, stage_a, stage_b, stage_c, (z, z), fill=fill, drain=drain)
    for t, (_, rdy) in zip(tiles, sched):
        if rdy > n_grp:
            _round_robin([out_tile(*t)])


def _gdn_expand_consts():
    e = np.zeros((N_PAIRS, BA_W, 4 * LANES), np.float32)
    for p in range(N_PAIRS):
        for blk in range(4):
            d, is_g = blk % 2, blk // 2
            for h in range(2):
                col = is_g * 2 * GDN_HEADS + d * GDN_HEADS + 2 * p + h
                e[p, col, blk * LANES + h * HD:blk * LANES + (h + 1) * HD] = 1.0
    return e


def _gdn_mixer(qkv_c, z_c, ba_c, qkv_l, z_l, ba_l, conv_w, a_log, dt_bias, norm_w, consts):
    bsz, tc, _ = qkv_c.shape
    tl = qkv_l.shape[1]
    tall = tc + tl
    n_all = tall // CH
    lane_pad = lambda a: jnp.pad(a.reshape(-1), (2 * GDN_HEADS, LANES - 4 * GDN_HEADS))
    aux = _rows([lane_pad(a_log), lane_pad(dt_bias), jnp.tile(norm_w, 2)], SUBLANES)
    e = jnp.asarray(_gdn_expand_consts())
    cm = lambda *blk: pl.BlockSpec(blk, lambda b, p: (0,) * len(blk))

    def tok(t, col0):
        return pl.BlockSpec((None, t, LANES), lambda b, p: (b, 0, col0 + p))
    in_specs = [tok(tc, 0), tok(tc, N_PAIRS), tok(tc, 2 * N_PAIRS), tok(tc, 0),
                pl.BlockSpec((None, tc, BA_W), lambda b, p: (b, 0, 0)),
                tok(tl, 0), tok(tl, N_PAIRS), tok(tl, 2 * N_PAIRS), tok(tl, 0),
                pl.BlockSpec((None, tl, BA_W), lambda b, p: (b, 0, 0)),
                pl.BlockSpec((CONV_W, LANES), lambda b, p: (0, p)),
                pl.BlockSpec((CONV_W, LANES), lambda b, p: (0, N_PAIRS + p)),
                pl.BlockSpec((CONV_W, LANES), lambda b, p: (0, 2 * N_PAIRS + p)),
                cm(SUBLANES, LANES),
                pl.BlockSpec((None, BA_W, 4 * LANES), lambda b, p: (p, 0, 0)),
                cm(2, CH, LANES), cm(2, CH, LANES), cm(2, 6, CH, LANES), cm(2, 1, LANES), cm(LANES, LANES),
                cm(2, ROWT, ROWT)]
    pad_rows = tall + 3 * GAP
    scratch = [pltpu.VMEM((pad_rows, LANES), F32)] * 3 + [pltpu.VMEM((tall, LANES), F32)] * 3 + [
        pltpu.VMEM((2, 3, tall, LANES), F32),
        pltpu.VMEM((2, _group_size(n_all), 2, 2 * CH, LANES), F32),
        pltpu.VMEM((2, _group_size(n_all), 2, 2 * LANES + 2 * CH + SUBLANES, LANES), F32),
        pltpu.VMEM((tall, LANES), F32), pltpu.VMEM((tall, LANES), F32)]
    return pl.pallas_call(
        _gdn_kernel,
        grid=(bsz, N_PAIRS),
        in_specs=in_specs,
        out_specs=[pl.BlockSpec((None, tc, LANES), lambda b, p: (b, 0, p)),
                   pl.BlockSpec((None, tl, LANES), lambda b, p: (b, 0, p))],
        out_shape=[jax.ShapeDtypeStruct((bsz, tc, GDN_WIDTH), F32),
                   jax.ShapeDtypeStruct((bsz, tl, GDN_WIDTH), F32)],
        scratch_shapes=scratch,
        compiler_params=pltpu.CompilerParams(dimension_semantics=("arbitrary", "arbitrary"),
                                             vmem_limit_bytes=VMEM_LIMIT),
        name="gdn_mixer",
    )(qkv_c, qkv_c, qkv_c, z_c, ba_c, qkv_l, qkv_l, qkv_l, z_l, ba_l,
      conv_w, conv_w, conv_w, aux, e,
      consts["incl"], consts["strict"], consts["lvl"], consts["hm"], consts["bdm"],
      consts["tri_t"])


def _rwkv_kernel(rc_ref, kc_ref, vc_ref, wdc_ref, adc_ref, gdc_ref,
                 rl_ref, kl_ref, vl_ref, wdl_ref, adl_ref, gdl_ref,
                 mur_ref, muk_ref, muv_ref, mux_ref, wup_ref, aup_ref, gup_ref, vec_ref,
                 incl_ref, strict_ref, lvl_ref, hm_ref, bdm_ref, trit_ref,
                 oc_ref, ol_ref,
                 vs_ref, gs_ref, bvs_ref, dir_ref, ls_ref, p1_ref, yf_ref, yb_ref):
    tc, tl = rc_ref.shape[0], rl_ref.shape[0]
    tall = tc + tl
    n_ctx, n_all = tc // CH, tall // CH
    grp = _group_size(n_all)
    m0, m1 = hm_ref[0].astype(BF16), hm_ref[1].astype(BF16)
    m0w = jnp.concatenate([m0, m0], axis=1)
    m1w = jnp.concatenate([m1, m1], axis=1)
    bdm = bdm_ref[...]

    kkw, kaw, rkw = vec_ref[0:1, :], vec_ref[1:2, :], vec_ref[2:3, :]

    def prep_tile(_, uo, n):
        ctx_tile = uo < tc
        lo = uo if ctx_tile else uo - tc
        t_len = tc if ctx_tile else tl
        rowi = lax.broadcasted_iota(jnp.int32, (n, LANES), 0)

        def lerp(src_c, src_l, mu):
            src = src_c if ctx_tile else src_l
            x = src[lo:lo + n, :]
            if lo > 0:
                xm = src[lo - 1:lo - 1 + n, :]
            else:
                xm = jnp.where(rowi == 0, 0.0, pltpu.roll(x, 1, 0))
            if lo + n < t_len:
                xp = src[lo + 1:lo + 1 + n, :]
            else:
                xp = jnp.where(rowi == n - 1, 0.0, pltpu.roll(x, n - 1, 0))
            return x + mu * (0.5 * (xm + xp) - x)
        r = lerp(rc_ref, rl_ref, mur_ref[...])
        k = lerp(kc_ref, kl_ref, muk_ref[...])
        v = lerp(vc_ref, vl_ref, muv_ref[...])
        wd = lerp(wdc_ref, wdl_ref, mux_ref[0:1, :])
        ad = lerp(adc_ref, adl_ref, mux_ref[1:2, :])
        gd = lerp(gdc_ref, gdl_ref, mux_ref[2:3, :])
        tw = jnp.tanh(wd)
        kkv = k * kkw
        yield
        kk = kkv * lax.rsqrt(_mm(kkv * kkv, bdm) + 1e-6)
        ksum = jnp.zeros_like(k)
        for d in range(2):
            lw = -0.6065306597126334 * _sigmoid(vec_ref[3 + d:4 + d, :] + _mm(tw, wup_ref[d]))
            a = _sigmoid(vec_ref[5 + d:6 + d, :] + _mm(ad, aup_ref[d]))
            kdir = k * (1.0 + (a - 1.0) * kaw)
            ksum = ksum + kdir
            cum = _mm_sel_l(trit_ref[d, 0:n, 0:n], lw)
            einv = jnp.exp(-cum)
            dir_ref[d, 0, uo:uo + n, :] = cum
            dir_ref[d, 1, uo:uo + n, :] = kk * jnp.exp(cum - lw)
            dir_ref[d, 2, uo:uo + n, :] = r * jnp.exp(cum)
            dir_ref[d, 3, uo:uo + n, :] = kk * a * einv
            dir_ref[d, 4, uo:uo + n, :] = kdir * einv
            yield
        bonus = _mm_sel_r(r * ksum * rkw, bdm)
        vs_ref[uo:uo + n, :] = v
        gs_ref[uo:uo + n, :] = _mm(_sigmoid(gd), gup_ref[...])
        bvs_ref[uo:uo + n, :] = bonus * v
        yield

    tiles = _row_tiles(tc, tl)
    sched = _tile_schedule(tc, tl, grp)
    for tile, (need, _) in zip(tiles, sched):
        if need == 0:
            _round_robin([prep_tile(*tile)])
    fill = ([_chain([prep_tile(*t) for t, (need, _) in zip(tiles, sched) if need == 1])],
            [_chain([prep_tile(*t) for t, (need, _) in zip(tiles, sched) if need >= 2])])

    n_grp = n_all // grp
    pd = [(i, d) for i in range(grp) for d in range(2)]
    chunk_rows = lambda gi, i, d: _chunk_rows(gi, grp, i, d, n_ctx, n_all)

    def stage_a(gi, slot):
        def finish(pi, pdir, m, vbd):
            tri2 = jnp.concatenate([strict_ref[pdir], incl_ref[pdir]], axis=0)
            av = _mm(m[:, LANES:] * tri2, vbd)
            ab = m[:, :LANES] * tri2
            ls_ref[slot, pi, pdir, 0:CH, :] = ab[:CH]
            ls_ref[slot, pi, pdir, CH:2 * CH, :] = av[:CH]
            ls_ref[slot, pi, pdir, 2 * CH:3 * CH, :] = ab[CH:]
            ls_ref[slot, pi, pdir, 3 * CH:4 * CH, :] = av[CH:]
        pending = None
        for i, d in pd:
            rows = chunk_rows(gi, i, d)
            kkq, rq, binv, kinv = (dir_ref[d, j, rows, :] for j in range(1, 5))
            lhs = jnp.concatenate([kkq, rq], axis=0)
            rhs = jnp.concatenate([_bd(binv, m0, m1), _bd(kinv, m0, m1)], axis=0)
            cur = (i, d, _mm_nt(lhs, rhs), _bd(vs_ref[rows, :], m0, m1))
            if pending is not None:
                finish(*pending)
            pending = cur
            yield
        finish(*pending)
        yield

    def stage_b(gi, slot):
        mats = [ls_ref[slot, i, d, 0:CH, :] for i, d in pd]
        es = []
        yield from _inv_levels(mats, [d for _, d in pd], lvl_ref, m0, m1, es)
        sols = []
        for e, (i, d) in zip(es, pd):
            rows = chunk_rows(gi, i, d)
            rhs = jnp.concatenate([dir_ref[d, 1, rows, :], ls_ref[slot, i, d, CH:2 * CH, :]], axis=1)
            sols.append(rhs + _mm(e, _bd(rhs, m0w, m1w)))
        yield
        for sol, (i, d) in zip(sols, pd):
            rows = chunk_rows(gi, i, d)
            cum = dir_ref[d, 0, rows, :]
            etot = jnp.exp(cum[CH - 1:CH, :] if d == 0 else cum[0:1, :])
            bdec, kdec = dir_ref[d, 3, rows, :] * etot, dir_ref[d, 4, rows, :] * etot
            ar = _mm(ls_ref[slot, i, d, 2 * CH:3 * CH, :], _bd(sol, m0w, m1w))
            pmat = _mm_tn(sol[:, :LANES], bdec)
            nmat = _mm_tn(jnp.concatenate([-sol[:, LANES:], vs_ref[rows, :]], axis=0),
                          jnp.concatenate([bdec, kdec], axis=0))
            p1_ref[slot, i, d, 0:LANES, :] = pmat * bdm
            p1_ref[slot, i, d, LANES:2 * LANES, :] = nmat * bdm
            p1_ref[slot, i, d, 2 * LANES:2 * LANES + CH, :] = dir_ref[d, 2, rows, :] - ar[:, :LANES]
            p1_ref[slot, i, d, 2 * LANES + CH:2 * LANES + 2 * CH, :] = (
                ls_ref[slot, i, d, 3 * CH:4 * CH, :] - ar[:, LANES:])
            p1_ref[slot, i, d, 2 * LANES + 2 * CH:2 * LANES + 2 * CH + SUBLANES, :] = jnp.broadcast_to(
                etot, (SUBLANES, LANES))
        yield

    def stage_c(gi, slot, st):
        for i, d in pd:
            rows = chunk_rows(gi, i, d)
            pmat = p1_ref[slot, i, d, 0:LANES, :]
            nmat = p1_ref[slot, i, d, LANES:2 * LANES, :]
            rmat = p1_ref[slot, i, d, 2 * LANES:2 * LANES + CH, :]
            ymat = p1_ref[slot, i, d, 2 * LANES + CH:2 * LANES + 2 * CH, :]
            etot = p1_ref[slot, i, d, 2 * LANES + 2 * CH:2 * LANES + 2 * CH + 1, :]
            yref = yf_ref if d == 0 else yb_ref
            yref[rows, :] = _mm_nt(rmat, st[d]) + ymat
            st[d] = st[d] * etot - _mm(st[d], pmat) + nmat
            yield

    gnw, gnb = vec_ref[7:8, :], vec_ref[8:9, :]

    def out_tile(_, uo, n):
        y = yf_ref[uo:uo + n, :] + yb_ref[uo:uo + n, :]
        mean = _mm_sel_r(y, bdm) * (1.0 / HD)
        yield
        yc = y - mean
        var = _mm(yc * yc, bdm) * (1.0 / HD)
        yield
        out = (yc * lax.rsqrt(var + RWKV_GN_EPS) * gnw + gnb + bvs_ref[uo:uo + n, :]) * gs_ref[uo:uo + n, :]
        if uo < tc:
            oc_ref[uo:uo + n, :] = out
        else:
            ol_ref[uo - tc:uo - tc + n, :] = out
        yield

    drain = ([_chain([out_tile(*t) for t, (_, rdy) in zip(tiles, sched) if rdy < n_grp])],
             [_chain([out_tile(*t) for t, (_, rdy) in zip(tiles, sched) if rdy == n_grp])])
    z = jnp.zeros((LANES, LANES), F32)
    _run_pipeline(n_grp, stage_a, stage_b, stage_c, (z, z), b_before_a=False, fill=fill, drain=drain)
    for t, (_, rdy) in zip(tiles, sched):
        if rdy > n_grp:
            _round_robin([out_tile(*t)])


def _rwkv_mixer(rw_c, rw_l, mu, w0, w_up, a0, a_up, g_up, k_k, k_a, r_k, gn_w, gn_b, consts):
    bsz, tc, _ = rw_c.shape
    tl = rw_l.shape[1]
    tall = tc + tl
    n_all = tall // CH
    W = RWKV_WIDTH
    nb = W // LANES
    zr = jnp.zeros((RWKV_RANK, W), F32)
    pad_dir = lambda u: jnp.stack([jnp.concatenate([u[0], zr], axis=0), jnp.concatenate([zr, u[1]], axis=0)])
    wup, aup = pad_dir(w_up), pad_dir(a_up)
    vec = _rows([k_k, k_a, r_k.reshape(-1), w0[0], w0[1], a0[0], a0[1], gn_w, gn_b], 16)
    mu_rkv = mu[:3 * W].reshape(3, 1, W)
    mux = jnp.pad(mu[3 * W:].reshape(3, LANES), ((0, SUBLANES - 3), (0, 0)))
    cm = lambda *blk: pl.BlockSpec(blk, lambda b, p: (0,) * len(blk))

    def tok(t, col0, per_pair=True):
        if per_pair:
            return pl.BlockSpec((None, t, LANES), lambda b, p: (b, 0, col0 + p))
        return pl.BlockSpec((None, t, LANES), lambda b, p: (b, 0, col0))
    stream = lambda t: [tok(t, 0), tok(t, nb), tok(t, 2 * nb), tok(t, 3 * nb, False),
                        tok(t, 3 * nb + 1, False), tok(t, 3 * nb + 2, False)]
    in_specs = stream(tc) + stream(tl) + [
        pl.BlockSpec((None, 1, LANES), lambda b, p: (0, 0, p)),
        pl.BlockSpec((None, 1, LANES), lambda b, p: (1, 0, p)),
        pl.BlockSpec((None, 1, LANES), lambda b, p: (2, 0, p)),
        cm(SUBLANES, LANES),
        pl.BlockSpec((2, LANES, LANES), lambda b, p: (0, 0, p)),
        pl.BlockSpec((2, LANES, LANES), lambda b, p: (0, 0, p)),
        pl.BlockSpec((RWKV_G_RANK, LANES), lambda b, p: (0, p)),
        pl.BlockSpec((16, LANES), lambda b, p: (0, p)),
        cm(2, CH, LANES), cm(2, CH, LANES), cm(2, 6, CH, LANES), cm(2, 1, LANES), cm(LANES, LANES),
        cm(2, ROWT, ROWT)]
    scratch = [pltpu.VMEM((tall, LANES), F32)] * 3 + [
        pltpu.VMEM((2, 5, tall, LANES), F32),
        pltpu.VMEM((2, _group_size(n_all), 2, 4 * CH, LANES), F32),
        pltpu.VMEM((2, _group_size(n_all), 2, 2 * LANES + 2 * CH + SUBLANES, LANES), F32),
        pltpu.VMEM((tall, LANES), F32), pltpu.VMEM((tall, LANES), F32)]
    return pl.pallas_call(
        _rwkv_kernel,
        grid=(bsz, N_PAIRS),
        in_specs=in_specs,
        out_specs=[pl.BlockSpec((None, tc, LANES), lambda b, p: (b, 0, p)),
                   pl.BlockSpec((None, tl, LANES), lambda b, p: (b, 0, p))],
        out_shape=[jax.ShapeDtypeStruct((bsz, tc, W), F32),
                   jax.ShapeDtypeStruct((bsz, tl, W), F32)],
        scratch_shapes=scratch,
        compiler_params=pltpu.CompilerParams(dimension_semantics=("arbitrary", "arbitrary"),
                                             vmem_limit_bytes=VMEM_LIMIT),
        name="rwkv_mixer",
    )(*([rw_c] * 6), *([rw_l] * 6), mu_rkv, mu_rkv, mu_rkv, mux, wup, aup, g_up, vec,
      consts["incl"], consts["strict"], consts["lvl"], consts["hm"], consts["bdm"], consts["tri_t"])


def _lru_kernel(xc_ref, gc_ref, xl_ref, gl_ref, cw_ref, vec_ref, wcat_ref, bcat_ref,
                oc_ref, ol_ref, xp_ref, ab_ref, hf_ref, hb_ref):
    tc, tl = xc_ref.shape[0], xl_ref.shape[0]
    tall = tc + tl
    rows = tl // GRID_W
    W = xc_ref.shape[1]
    z = jnp.zeros((GAP, W), F32)
    xp_ref[0:GAP, :] = z
    xp_ref[GAP:GAP + tc, :] = xc_ref[...]
    xp_ref[GAP + tc:2 * GAP + tc, :] = z
    base = 2 * GAP + tc
    for c in range(GRID_W):
        xp_ref[base + c * rows:base + (c + 1) * rows, :] = xl_ref[pl.ds(c, rows, stride=GRID_W), :]
    xp_ref[base + tl:base + tl + GAP, :] = z
    cb = vec_ref[0:1, :]
    for (po, uo, n) in _row_tiles(tc, tl):
        acc = cb + cw_ref[0:1, :] * xp_ref[po - 2:po - 2 + n, :]
        for tap in range(1, CONV_W):
            acc = acc + cw_ref[tap:tap + 1, :] * xp_ref[po - 2 + tap:po - 2 + tap + n, :]
        gates = _mm(acc, wcat_ref[...]) + bcat_ref[...]
        for d in range(2):
            rg = _sigmoid(gates[:, (2 * d) * W:(2 * d + 1) * W])
            ig = _sigmoid(gates[:, (2 * d + 1) * W:(2 * d + 2) * W])
            log_a = -LRU_C * rg * _softplus(-vec_ref[1 + d:2 + d, :])
            a = jnp.exp(log_a)
            mult = jnp.sqrt(-jnp.tanh(log_a) * (1.0 + a * a))
            ab_ref[d, 0, uo:uo + n, :] = a
            ab_ref[d, 1, uo:uo + n, :] = mult * (ig * acc)

    sub = lax.broadcasted_iota(jnp.int32, (SUBLANES, W), 0)
    n_tiles_c, n_tiles = tc // SUBLANES, tall // SUBLANES

    def tile_scan(a, b, d):
        for sh in (1, 2, 4):
            if d == 0:
                ok = sub >= sh
                a_s = jnp.where(ok, pltpu.roll(a, sh, 0), 1.0)
                b_s = jnp.where(ok, pltpu.roll(b, sh, 0), 0.0)
            else:
                ok = sub < SUBLANES - sh
                a_s = jnp.where(ok, pltpu.roll(a, SUBLANES - sh, 0), 1.0)
                b_s = jnp.where(ok, pltpu.roll(b, SUBLANES - sh, 0), 0.0)
            b = b + a * b_s
            a = a * a_s
        return a, b

    unroll = 4 if (n_tiles_c % 4 == 0 and n_tiles % 4 == 0) else 1

    def scan_body(s4, carry):
        hf, hb = carry
        tiles = []
        for j in range(unroll):
            s = s4 * unroll + j
            r0 = pl.multiple_of(s * SUBLANES, SUBLANES)
            tb = jnp.where(s < n_tiles_c, n_tiles_c - 1 - s, n_tiles + n_tiles_c - 1 - s)
            r1 = pl.multiple_of(tb * SUBLANES, SUBLANES)
            tiles.append((r0, tile_scan(ab_ref[0, 0, pl.ds(r0, SUBLANES), :], ab_ref[0, 1, pl.ds(r0, SUBLANES), :], 0),
                          r1, tile_scan(ab_ref[1, 0, pl.ds(r1, SUBLANES), :], ab_ref[1, 1, pl.ds(r1, SUBLANES), :], 1)))
        for r0, (af, bf), r1, (ab, bb) in tiles:
            h = bf + af * hf
            hf_ref[pl.ds(r0, SUBLANES), :] = h
            hf = jnp.broadcast_to(h[SUBLANES - 1:SUBLANES, :], (SUBLANES, W))
            h = bb + ab * hb
            hb_ref[pl.ds(r1, SUBLANES), :] = h
            hb = jnp.broadcast_to(h[0:1, :], (SUBLANES, W))
        return hf, hb
    z8 = jnp.zeros((SUBLANES, W), F32)
    lax.fori_loop(0, n_tiles // unroll, scan_body, (z8, z8))

    oc_ref[...] = (hf_ref[0:tc, :] + hb_ref[0:tc, :]) * _gelu_tanh(gc_ref[...])
    for c in range(GRID_W):
        h = hf_ref[tc + c * rows:tc + (c + 1) * rows, :] + hb_ref[tc + c * rows:tc + (c + 1) * rows, :]
        ol_ref[pl.ds(c, rows, stride=GRID_W), :] = h * _gelu_tanh(gl_ref[pl.ds(c, rows, stride=GRID_W), :])


def _lru_mixer(x_c, g_c, x_l, g_l, conv_w, conv_b, w_a, b_a, w_x, b_x, lam):
    bsz, tc, W = x_c.shape
    tl = x_l.shape[1]
    tall = tc + tl
    nh = W // LANES
    bph = LANES // LRU_BW
    w6 = jnp.stack([w_a, w_x], axis=1).reshape(2, 2, nh, bph, LRU_BW, LRU_BW)
    wcat = jnp.einsum("dghnrc,nm->hnrdgmc", w6, jnp.eye(bph, dtype=F32)).reshape(nh, LANES, 4 * LANES)
    bcat = jnp.stack([b_a, b_x], axis=1).reshape(2, 2, nh, LANES).transpose(2, 0, 1, 3).reshape(nh, 1, 4 * LANES)
    vec = _rows([conv_b, lam[0], lam[1]], SUBLANES)
    tok = lambda t: pl.BlockSpec((None, t, LANES), lambda b, h: (b, 0, h))
    return pl.pallas_call(
        _lru_kernel,
        grid=(bsz, nh),
        in_specs=[tok(tc), tok(tc), tok(tl), tok(tl),
                  pl.BlockSpec((CONV_W, LANES), lambda b, h: (0, h)),
                  pl.BlockSpec((SUBLANES, LANES), lambda b, h: (0, h)),
                  pl.BlockSpec((None, LANES, 4 * LANES), lambda b, h: (h, 0, 0)),
                  pl.BlockSpec((None, 1, 4 * LANES), lambda b, h: (h, 0, 0))],
        out_specs=[tok(tc), tok(tl)],
        out_shape=[jax.ShapeDtypeStruct((bsz, tc, W), F32), jax.ShapeDtypeStruct((bsz, tl, W), F32)],
        scratch_shapes=[pltpu.VMEM((tall + 3 * GAP, LANES), F32),
                        pltpu.VMEM((2, 2, tall, LANES), F32),
                        pltpu.VMEM((tall, LANES), F32), pltpu.VMEM((tall, LANES), F32)],
        compiler_params=pltpu.CompilerParams(dimension_semantics=("arbitrary", "arbitrary"),
                                             vmem_limit_bytes=VMEM_LIMIT),
        name="lru_mixer",
    )(x_c, g_c, x_l, g_l, conv_w, vec, wcat.astype(BF16), bcat)


def _finish_kernel(x_ref, gdn_ref, lru_ref, rwk_ref, m2_ref, m3_ref, m4_ref, m5_ref, nrm_ref,
                   wo_ref, up_ref, dn_ref, o_ref):
    x = x_ref[...]
    o = (jnp.dot(gdn_ref[...].astype(BF16), wo_ref[0:GDN_WIDTH, :], preferred_element_type=F32)
         + jnp.dot(lru_ref[...].astype(BF16), wo_ref[GDN_WIDTH:GDN_WIDTH + LRU_WIDTH, :],
                   preferred_element_type=F32)
         + jnp.dot(rwk_ref[...].astype(BF16), wo_ref[GDN_WIDTH + LRU_WIDTH:, :], preferred_element_type=F32))
    x = x + m2_ref[...] * _rms(o, nrm_ref[0:1, :])
    h = (_rms(x, nrm_ref[1:2, :]) * (1.0 + m4_ref[...]) + m3_ref[...]).astype(BF16)
    f = jnp.zeros_like(x)
    fc = 1024
    for j in range(D_FF // fc):
        a = jnp.maximum(jnp.dot(h, up_ref[:, j * fc:(j + 1) * fc], preferred_element_type=F32), 0.0)
        f = f + jnp.dot((a * a).astype(BF16), dn_ref[j * fc:(j + 1) * fc, :], preferred_element_type=F32)
    o_ref[...] = x + m5_ref[...] * _rms(f, nrm_ref[2:3, :])


def _finish(x2, gdn, lru, rwk, mod_rows, rows_per_mod, norms, wo_bf, up_bf, dn_bf, layer):
    n = x2.shape[0]
    tm = DENSE_TM
    tiles_per_mod = rows_per_mod // tm
    modspec = lambda k: pl.BlockSpec((None, 1, D_MODEL), lambda i: (6 * (i // tiles_per_mod) + k, 0, 0))
    cm = lambda *blk: pl.BlockSpec((None,) + blk, lambda i: (layer,) + (0,) * len(blk),
                                   pipeline_mode=pl.Buffered(1))
    return pl.pallas_call(
        _finish_kernel,
        grid=(n // tm,),
        in_specs=[pl.BlockSpec((tm, D_MODEL), lambda i: (i, 0)),
                  pl.BlockSpec((tm, GDN_WIDTH), lambda i: (i, 0)),
                  pl.BlockSpec((tm, LRU_WIDTH), lambda i: (i, 0)),
                  pl.BlockSpec((tm, RWKV_WIDTH), lambda i: (i, 0)),
                  modspec(2), modspec(3), modspec(4), modspec(5),
                  cm(SUBLANES, D_MODEL), cm(D_MODEL, D_MODEL), cm(D_MODEL, D_FF), cm(D_FF, D_MODEL)],
        out_specs=pl.BlockSpec((tm, D_MODEL), lambda i: (i, 0)),
        out_shape=jax.ShapeDtypeStruct((n, D_MODEL), F32),
        compiler_params=pltpu.CompilerParams(dimension_semantics=("arbitrary",),
                                             vmem_limit_bytes=VMEM_LIMIT),
        name="finish",
    )(x2, gdn, lru, rwk, mod_rows, mod_rows, mod_rows, mod_rows, norms, wo_bf, up_bf, dn_bf)


def _arrange_w_in(w):
    s = np.cumsum([0, 3 * GDN_WIDTH, GDN_WIDTH, 2 * GDN_HEADS, 2 * GDN_HEADS, LRU_WIDTH, LRU_WIDTH, RWKV_IN])
    pad = jnp.zeros(w.shape[:-1] + (BA_W - 4 * GDN_HEADS,), w.dtype)
    return jnp.concatenate([w[..., s[0]:s[4]], pad, w[..., s[4]:]], axis=-1)


def kernel(x, c, ctx, c_ctx, ada_w, ada_b, norm_mix_pre, norm_mix_post, norm_ffn_pre, norm_ffn_post, w_in, gdn_conv, gdn_a_log, gdn_dt_bias, gdn_norm, lru_conv, lru_conv_b, lru_wa, lru_ba, lru_wx, lru_bx, lru_lambda, rwkv_mu, rwkv_w0, rwkv_w_up, rwkv_a0, rwkv_a_up, rwkv_g_up, rwkv_k_k, rwkv_k_a, rwkv_r_k, rwkv_gn_w, rwkv_gn_b, w_out, ffn_up, ffn_down):
    bsz, tl, _ = x.shape
    tc = ctx.shape[1]
    depth = w_in.shape[0]
    consts = _device_consts()

    cvec = jnp.pad(jnp.concatenate([c, c_ctx[None, :]], axis=0), ((0, 16 - bsz - 1), (0, 0)))
    mods = _ada_mod(cvec, ada_w, ada_b).reshape(depth, 16, 6, D_MODEL)

    w_in_bf = _arrange_w_in(w_in).astype(BF16)
    wo_bf, up_bf, dn_bf = w_out.astype(BF16), ffn_up.astype(BF16), ffn_down.astype(BF16)
    g_pre = norm_mix_pre.reshape(depth, 1, D_MODEL)
    norms = jnp.pad(jnp.stack([norm_mix_post, norm_ffn_pre, norm_ffn_post], axis=1),
                    ((0, 0), (0, SUBLANES - 3), (0, 0)))

    xl = x.reshape(bsz * tl, D_MODEL)
    xc = ctx.reshape(bsz * tc, D_MODEL)
    for i in range(depth):
        mod_l = mods[i, 0:bsz].reshape(bsz * 6, 1, D_MODEL)
        mod_c = mods[i, bsz:bsz + 1].reshape(6, 1, D_MODEL)
        pl_ = _inproj(xl, mod_l, tl, g_pre, w_in_bf, i)
        pc_ = _inproj(xc, mod_c, bsz * tc, g_pre, w_in_bf, i)
        r3 = lambda a, t: a.reshape(bsz, t, a.shape[-1])
        qkv_l, z_l, ba_l, lx_l, lg_l, rw_l = (r3(a, tl) for a in pl_)
        qkv_c, z_c, ba_c, lx_c, lg_c, rw_c = (r3(a, tc) for a in pc_)

        gdn_c, gdn_l = _gdn_mixer(qkv_c, z_c, ba_c, qkv_l, z_l, ba_l, gdn_conv[i], gdn_a_log[i],
                                  gdn_dt_bias[i], gdn_norm[i], consts)
        lru_c, lru_l = _lru_mixer(lx_c, lg_c, lx_l, lg_l, lru_conv[i], lru_conv_b[i], lru_wa[i], lru_ba[i],
                                  lru_wx[i], lru_bx[i], lru_lambda[i])
        rwk_c, rwk_l = _rwkv_mixer(rw_c, rw_l, rwkv_mu[i], rwkv_w0[i], rwkv_w_up[i], rwkv_a0[i],
                                   rwkv_a_up[i], rwkv_g_up[i], rwkv_k_k[i], rwkv_k_a[i], rwkv_r_k[i],
                                   rwkv_gn_w[i], rwkv_gn_b[i], consts)

        f2 = lambda a: a.reshape(-1, a.shape[-1])
        xl = _finish(xl, f2(gdn_l), f2(lru_l), f2(rwk_l), mod_l, tl, norms, wo_bf, up_bf, dn_bf, i)
        if i < depth - 1:
            xc = _finish(xc, f2(gdn_c), f2(lru_c), f2(rwk_c), mod_c, bsz * tc, norms, wo_bf, up_bf, dn_bf, i)
    return xl.reshape(bsz, tl, D_MODEL)
```

```python
import functools

import numpy as np
import jax
import jax.numpy as jnp
from jax import lax
from jax.experimental import pallas as pl
from jax.experimental.pallas import tpu as pltpu

F32 = jnp.float32
BF16 = jnp.bfloat16

LANES = 128
SUBLANES = 8
VMEM_LIMIT = 56 * 1024 * 1024

D_MODEL = 1024
DEPTH = 2
GRID_W = 64
CONV_W = 4
EPS = 1e-6
D_FF = 4 * D_MODEL
HD = 64
GDN_WIDTH = 3 * D_MODEL // 8
GDN_HEADS = GDN_WIDTH // HD
LRU_WIDTH = D_MODEL // 4
LRU_BLOCKS = 4
LRU_BW = LRU_WIDTH // LRU_BLOCKS
LRU_C = 8.0
RWKV_WIDTH = D_MODEL - GDN_WIDTH - LRU_WIDTH
RWKV_HEADS = RWKV_WIDTH // HD
RWKV_RANK = 64
RWKV_G_RANK = 128
RWKV_GN_EPS = 6.4e-4
RWKV_IN = 3 * RWKV_WIDTH + 2 * RWKV_RANK + 2 * RWKV_RANK + RWKV_G_RANK
N_PAIRS = GDN_HEADS // 2
BA_W = LANES
P_OFF = np.cumsum([0, 3 * GDN_WIDTH, GDN_WIDTH, BA_W, LRU_WIDTH, LRU_WIDTH, RWKV_IN])
D_INP = int(P_OFF[-1])

CH = 64
ROWT = 256
DENSE_TM = 512
GAP = SUBLANES


def _mm(a, b):
    return jnp.dot(a.astype(BF16), b.astype(BF16), preferred_element_type=F32)


def _mm_nt(a, b):
    return lax.dot_general(a.astype(BF16), b.astype(BF16), (((1,), (1,)), ((), ())),
                           preferred_element_type=F32)


def _mm_tn(a, b):
    return lax.dot_general(a.astype(BF16), b.astype(BF16), (((0,), (0,)), ((), ())),
                           preferred_element_type=F32)


def _split2(x):
    hi = x.astype(BF16)
    lo = (x - hi.astype(F32)).astype(BF16)
    return hi, lo


def _mm_sel_l(m01, x):
    mb = m01.astype(BF16)
    h, l = _split2(x)
    d = functools.partial(jnp.dot, preferred_element_type=F32)
    return d(mb, h) + d(mb, l)


def _mm_sel_r(x, m01):
    mb = m01.astype(BF16)
    h, l = _split2(x)
    d = functools.partial(jnp.dot, preferred_element_type=F32)
    return d(h, mb) + d(l, mb)


def _sigmoid(x):
    return 0.5 * jnp.tanh(0.5 * x) + 0.5


def _silu_of_twice(h):
    return h + h * jnp.tanh(h)


def _silu(x):
    return _silu_of_twice(0.5 * x)


def _softplus(x):
    return jnp.maximum(x, 0.0) + jnp.log(1.0 + jnp.exp(-jnp.abs(x)))


def _gelu_tanh(x):
    return 0.5 * x * (1.0 + jnp.tanh(0.7978845608028654 * (x + 0.044715 * (x * x * x))))


def _bd(y, m0, m1):
    yb = y.astype(BF16)
    return jnp.concatenate([yb * m0, yb * m1], axis=0)


def _pair_consts():
    i = np.arange(CH)[:, None]
    j = (np.arange(LANES) % HD)[None, :]
    incl = np.stack([i >= j, i <= j]).astype(np.float32)
    strict = np.stack([i > j, i < j]).astype(np.float32)
    lvls = []
    for d in range(2):
        per = []
        for m in (1, 2, 4, 8, 16, 32):
            same = (i // (2 * m)) == (j // (2 * m))
            lo_i, lo_j = (i % (2 * m)) < m, (j % (2 * m)) < m
            off = same & (~lo_i) & lo_j if d == 0 else same & lo_i & (~lo_j)
            per.append(off)
        lvls.append(np.stack(per))
    lvl = np.stack(lvls).astype(np.float32)
    lane = np.arange(LANES)
    hm = np.stack([lane < HD, lane >= HD]).astype(np.float32)[:, None, :]
    bdm = ((np.arange(LANES)[:, None] // HD) == (lane[None, :] // HD)).astype(np.float32)
    r = np.arange(ROWT)[:, None]
    c = np.arange(ROWT)[None, :]
    same = (r // CH) == (c // CH)
    tri_t = np.stack([same & (r >= c), same & (r <= c)]).astype(np.float32)
    return dict(incl=incl, strict=strict, lvl=lvl, hm=hm, bdm=bdm, tri_t=tri_t)


def _device_consts():
    out = {k: jnp.asarray(v) for k, v in _pair_consts().items()}
    out["tri_t"] = out["tri_t"].astype(BF16)
    return out


def _inv_levels(mats, dirs, lvl_ref, m0, m1, out):
    es = [-(a * lvl_ref[d, 0]) for a, d in zip(mats, dirs)]
    for k in range(1, 6):
        offs = [a * lvl_ref[d, k] for a, d in zip(mats, dirs)]
        xs = [off + p for off, p in zip(offs, _mm_pairs(es, [_bd(off, m0, m1) for off in offs]))]
        yield
        es = [e - x - p for e, x, p in zip(es, xs, _mm_pairs(xs, [_bd(e, m0, m1) for e in es]))]
        yield
    out.extend(es)


def _mm_pairs(lhs, rhs):
    out = []
    for j in range(0, len(lhs) - 1, 2):
        r = jnp.dot(jnp.concatenate([lhs[j], lhs[j + 1]], axis=0).astype(BF16),
                    jnp.concatenate([rhs[j], rhs[j + 1]], axis=1), preferred_element_type=F32)
        out += [r[:CH, :LANES], r[CH:, LANES:]]
    if len(lhs) % 2:
        out.append(_mm(lhs[-1], rhs[-1]))
    return out


def _rows(vectors, n_rows):
    m = jnp.stack([v.astype(F32) for v in vectors])
    return jnp.pad(m, ((0, n_rows - m.shape[0]), (0, 0)))


def _round_robin(gens):
    gens = list(gens)
    while gens:
        alive = []
        for g in gens:
            try:
                next(g)
                alive.append(g)
            except StopIteration:
                pass
        gens = alive


def _group_size(n_all):
    for g in (6, 4, 3, 2):
        if n_all % g == 0:
            return g
    return 1


def _chunk_order(s, n_ctx, n_all, d):
    if d == 0:
        return s
    if isinstance(s, int):
        return n_ctx - 1 - s if s < n_ctx else n_all + n_ctx - 1 - s
    return jnp.where(s < n_ctx, n_ctx - 1 - s, n_all + n_ctx - 1 - s)


def _chunk_rows(gi, grp, i, d, n_ctx, n_all):
    c = _chunk_order(gi * grp + i, n_ctx, n_all, d)
    return pl.ds(c * CH if isinstance(c, int) else pl.multiple_of(c * CH, CH), CH)


def _run_pipeline(n_grp, stage_a, stage_b, stage_c, st, b_before_a=True, fill=((), ()), drain=((), ())):
    def step(t, st, do_a, do_b, do_c, extra=()):
        par = t % 2 if isinstance(t, int) else lax.rem(t, 2)
        st = list(st)
        gens = []
        if do_c:
            gens.append(stage_c(t - 2, par, st))
        ab = ([stage_b(t - 1, 1 - par)] if do_b else []) + ([stage_a(t, par)] if do_a else [])
        _round_robin(gens + (ab if b_before_a else ab[::-1]) + list(extra))
        return tuple(st)
    assert n_grp >= 2
    st = step(0, st, True, False, False, fill[0])
    st = step(1, st, True, True, False, fill[1])
    st = lax.fori_loop(2, n_grp, lambda t, s: step(t, s, True, True, True), st)
    st = step(n_grp, st, False, True, True, drain[0])
    return step(n_grp + 1, st, False, False, True, drain[1])


def _tile_schedule(tc, tl, grp):
    n_ctx, n_all = tc // CH, (tc + tl) // CH
    first, last = {}, {}
    for s in range(n_all):
        g = s // grp
        for d in range(2):
            c = _chunk_order(s, n_ctx, n_all, d)
            first[c] = min(first.get(c, g), g)
            last[c] = max(last.get(c, (0, 0)), (g + 2, 2 * (s % grp) + d))
    out = []
    for (_, uo, n) in _row_tiles(tc, tl):
        cs = range(uo // CH, (uo + n) // CH)
        out.append((min(first[c] for c in cs), max(last[c] for c in cs)))
    return out


def _chain(gens):
    for g in gens:
        yield from g


def _delayed(gen, rounds):
    for _ in range(rounds):
        yield
    yield from gen


def _drain_streams(tiles, sched, n_grp, out_tile):
    drain = ([], [])
    for t, (_, (step, sub)) in zip(tiles, sched):
        if step < n_grp:
            drain[0].append(out_tile(*t))
        else:
            drain[step - n_grp].append(_delayed(out_tile(*t), sub + 1))
    return drain


def _fill_padded(dst_ref, src_c_ref, src_l_ref, tc, tl):
    w = dst_ref.shape[1]
    z = jnp.zeros((GAP, w), F32)
    dst_ref[0:GAP, :] = z
    dst_ref[GAP:GAP + tc, :] = src_c_ref[...]
    dst_ref[GAP + tc:2 * GAP + tc, :] = z
    dst_ref[2 * GAP + tc:2 * GAP + tc + tl, :] = src_l_ref[...]
    dst_ref[2 * GAP + tc + tl:3 * GAP + tc + tl, :] = z


def _row_tiles(tc, tl):
    out = []
    for base_p, base_u, n in ((GAP, 0, tc), (2 * GAP + tc, tc, tl)):
        for t0 in range(0, n, ROWT):
            out.append((base_p + t0, base_u + t0, min(ROWT, n - t0)))
    return out


def _ada_kernel(c_ref, w_ref, b_ref, o_ref):
    c = c_ref[...]
    o_ref[...] = _mm(_silu(c), w_ref[...]) + b_ref[...]


def _ada_mod(cvec, ada_w, ada_b):
    L = ada_w.shape[0]
    n = ada_w.shape[2]
    tn = 1536
    return pl.pallas_call(
        _ada_kernel,
        grid=(L, n // tn),
        in_specs=[pl.BlockSpec((16, D_MODEL), lambda l, j: (0, 0)),
                  pl.BlockSpec((None, D_MODEL, tn), lambda l, j: (l, 0, j)),
                  pl.BlockSpec((None, 1, tn), lambda l, j: (l, 0, j))],
        out_specs=pl.BlockSpec((None, 16, tn), lambda l, j: (l, 0, j)),
        out_shape=jax.ShapeDtypeStruct((L, 16, n), F32),
        compiler_params=pltpu.CompilerParams(dimension_semantics=("arbitrary", "arbitrary"),
                                             vmem_limit_bytes=VMEM_LIMIT),
        name="ada_mod",
    )(cvec, ada_w, ada_b.reshape(L, 1, n))


def _rms(x, g):
    return x * lax.rsqrt(jnp.mean(x * x, axis=-1, keepdims=True) + EPS) * g


def _inproj_kernel(x_ref, sh_ref, sc_ref, g_ref, w_ref, qkv_ref, z_ref, ba_ref, lx_ref, lg_ref, rw_ref):
    h = _rms(x_ref[...], g_ref[...]) * (1.0 + sc_ref[...]) + sh_ref[...]
    p = jnp.dot(h.astype(BF16), w_ref[...], preferred_element_type=F32)
    for ref, k in zip((qkv_ref, z_ref, ba_ref, lx_ref, lg_ref, rw_ref), range(6)):
        ref[...] = p[:, int(P_OFF[k]):int(P_OFF[k + 1])]


def _inproj(x2, mod_rows, rows_per_mod, g_all, w_all, layer):
    n = x2.shape[0]
    tm = DENSE_TM
    tiles_per_mod = rows_per_mod // tm
    widths = [int(P_OFF[k + 1] - P_OFF[k]) for k in range(6)]
    return pl.pallas_call(
        _inproj_kernel,
        grid=(n // tm,),
        in_specs=[pl.BlockSpec((tm, D_MODEL), lambda i: (i, 0)),
                  pl.BlockSpec((None, 1, D_MODEL), lambda i: (6 * (i // tiles_per_mod), 0, 0)),
                  pl.BlockSpec((None, 1, D_MODEL), lambda i: (6 * (i // tiles_per_mod) + 1, 0, 0)),
                  pl.BlockSpec((None, 1, D_MODEL), lambda i: (layer, 0, 0)),
                  pl.BlockSpec((None, D_MODEL, D_INP), lambda i: (layer, 0, 0), pipeline_mode=pl.Buffered(1))],
        out_specs=[pl.BlockSpec((tm, w), lambda i: (i, 0)) for w in widths],
        out_shape=[jax.ShapeDtypeStruct((n, w), F32) for w in widths],
        compiler_params=pltpu.CompilerParams(dimension_semantics=("arbitrary",),
                                             vmem_limit_bytes=VMEM_LIMIT),
        name="inproj",
    )(x2, mod_rows, mod_rows, g_all, w_all)


def _gdn_kernel(qc_ref, kc_ref, vc_ref, zc_ref, bac_ref, ql_ref, kl_ref, vl_ref, zl_ref, bal_ref,
                cwq_ref, cwk_ref, cwv_ref, aux_ref, e_ref,
                incl_ref, strict_ref, lvl_ref, hm_ref, bdm_ref, trit_ref,
                oc_ref, ol_ref,
                qp_ref, kp_ref, vp_ref, qs_ref, ks_ref, vs_ref, dir_ref, ls_ref, p1_ref, of_ref, ob_ref):
    tc, tl = qc_ref.shape[0], ql_ref.shape[0]
    tall = tc + tl
    n_ctx, n_all = tc // CH, tall // CH
    grp = _group_size(n_all)
    m0, m1 = hm_ref[0].astype(BF16), hm_ref[1].astype(BF16)
    m0w = jnp.concatenate([m0, m0], axis=1)
    m1w = jnp.concatenate([m1, m1], axis=1)
    bdm = bdm_ref[...]

    _fill_padded(qp_ref, qc_ref, ql_ref, tc, tl)
    _fill_padded(kp_ref, kc_ref, kl_ref, tc, tl)
    _fill_padded(vp_ref, vc_ref, vl_ref, tc, tl)
    lane = lax.broadcasted_iota(jnp.int32, (1, LANES), 1)
    alog, dtb = aux_ref[0:1, :], aux_ref[1:2, :]

    def prep_tile(po, uo, n):
        def conv(src, cw_ref):
            cw = 0.5 * cw_ref[...]
            acc = cw[0:1, :] * src[po - 2:po - 2 + n, :]
            for tap in range(1, CONV_W):
                acc = acc + cw[tap:tap + 1, :] * src[po - 2 + tap:po - 2 + tap + n, :]
            return _silu_of_twice(acc)
        q = conv(qp_ref, cwq_ref)
        k = conv(kp_ref, cwk_ref)
        yield
        q = q * lax.rsqrt(_mm(q * q, bdm) + 1e-6) * (HD ** -0.5)
        k = k * lax.rsqrt(_mm(k * k, bdm) + 1e-6)
        qs_ref[uo:uo + n, :] = q
        ks_ref[uo:uo + n, :] = k
        vs_ref[uo:uo + n, :] = conv(vp_ref, cwv_ref)
        yield
        ba = bac_ref[uo:uo + n, :] if uo < tc else bal_ref[uo - tc:uo - tc + n, :]
        beta = _sigmoid(ba)
        gval = -jnp.exp(alog) * _softplus(ba + dtb)
        bgv = jnp.where(lane < 2 * GDN_HEADS, beta, gval)
        ex = _mm_sel_r(bgv, e_ref[...])
        incl_t = [jnp.concatenate([incl_ref[dd]] * (n // CH), axis=0) for dd in range(2)]
        yield
        for d in range(2):
            g = ex[:, (2 + d) * LANES:(3 + d) * LANES]
            gc = _mm_sel_l(trit_ref[d, 0:n, 0:n], g)
            gt = gc.T
            gr = jnp.concatenate(
                [jnp.broadcast_to(jnp.concatenate([gt[0:1, c * CH:(c + 1) * CH], gt[HD:HD + 1, c * CH:(c + 1) * CH]],
                                                  axis=1), (CH, LANES)) for c in range(n // CH)], axis=0)
            dir_ref[d, 0, uo:uo + n, :] = ex[:, d * LANES:(d + 1) * LANES]
            dir_ref[d, 1, uo:uo + n, :] = gc
            dir_ref[d, 2, uo:uo + n, :] = jnp.exp(jnp.minimum(gc - gr, 0.0)) * incl_t[d]
            yield

    tiles = _row_tiles(tc, tl)
    sched = _tile_schedule(tc, tl, grp)
    for tile, (need, _) in zip(tiles, sched):
        if need == 0:
            _round_robin([prep_tile(*tile)])
    fill = ([_chain([prep_tile(*t) for t, (need, _) in zip(tiles, sched) if need == 1])],
            [_chain([prep_tile(*t) for t, (need, _) in zip(tiles, sched) if need >= 2])])

    n_grp = n_all // grp
    chunk_rows = lambda gi, i, d: _chunk_rows(gi, grp, i, d, n_ctx, n_all)
    pd = [(i, d) for i in range(grp) for d in range(2)]

    def stage_a(gi, slot):
        for i, d in pd:
            rows = chunk_rows(gi, i, d)
            q, k = qs_ref[rows, :], ks_ref[rows, :]
            beta, dec = dir_ref[d, 0, rows, :], dir_ref[d, 2, rows, :]
            la = _mm_nt(jnp.concatenate([k, q], axis=0), _bd(k, m0, m1))
            ls_ref[slot, i, d, 0:CH, :] = la[:CH] * dec * strict_ref[d] * beta
            ls_ref[slot, i, d, CH:2 * CH, :] = la[CH:] * dec
            yield

    def stage_b(gi, slot):
        mats = [ls_ref[slot, i, d, 0:CH, :] for i, d in pd]
        es = []
        yield from _inv_levels(mats, [d for _, d in pd], lvl_ref, m0, m1, es)
        sols, kds, egs, gls = [], [], [], []
        for e, (i, d) in zip(es, pd):
            rows = chunk_rows(gi, i, d)
            k, v = ks_ref[rows, :], vs_ref[rows, :]
            beta, gc = dir_ref[d, 0, rows, :], dir_ref[d, 1, rows, :]
            eg = jnp.exp(gc)
            rhs = jnp.concatenate([v * beta, k * beta * eg], axis=1)
            sols.append(rhs + _mm(e, _bd(rhs, m0w, m1w)))
            glast = gc[CH - 1:CH, :] if d == 0 else gc[0:1, :]
            kds.append(k * jnp.exp(glast - gc))
            egs.append(eg)
            gls.append(glast)
        yield
        for sol, kd, eg, glast, (i, d) in zip(sols, kds, egs, gls, pd):
            rows = chunk_rows(gi, i, d)
            attn = ls_ref[slot, i, d, CH:2 * CH, :]
            au = _mm(attn, _bd(sol, m0w, m1w))
            kn = _mm_tn(kd, sol)
            p1_ref[slot, i, d, 0:LANES, :] = kn[:, LANES:] * bdm
            p1_ref[slot, i, d, LANES:LANES + CH, :] = qs_ref[rows, :] * eg - au[:, LANES:]
            p1_ref[slot, i, d, LANES + CH:2 * LANES + CH, :] = kn[:, :LANES] * bdm
            p1_ref[slot, i, d, 2 * LANES + CH:2 * LANES + 2 * CH, :] = au[:, :LANES]
            p1_ref[slot, i, d, 2 * LANES + 2 * CH:2 * LANES + 2 * CH + SUBLANES, :] = jnp.broadcast_to(
                jnp.exp(glast), (SUBLANES, LANES))
        yield

    def stage_c(gi, slot, st):
        for i, d in pd:
            rows = chunk_rows(gi, i, d)
            kwq = p1_ref[slot, i, d, 0:LANES + CH, :]
            nmat = p1_ref[slot, i, d, LANES + CH:2 * LANES + CH, :]
            omat = p1_ref[slot, i, d, 2 * LANES + CH:2 * LANES + 2 * CH, :]
            egl = p1_ref[slot, i, d, 2 * LANES + 2 * CH:2 * LANES + 2 * CH + 1, :]
            ks = _mm(kwq, st[d])
            oref = of_ref if d == 0 else ob_ref
            oref[rows, :] = ks[LANES:] + omat
            st[d] = st[d] * egl - ks[:LANES] + nmat
            yield

    nw = aux_ref[2:3, :]

    def out_tile(po, uo, n):
        o = of_ref[uo:uo + n, :] + ob_ref[uo:uo + n, :]
        ms = _mm(o * o, bdm) * (1.0 / HD)
        yield
        if uo < tc:
            oc_ref[uo:uo + n, :] = o * lax.rsqrt(ms + EPS) * nw * _silu(zc_ref[uo:uo + n, :])
        else:
            lo = uo - tc
            ol_ref[lo:lo + n, :] = o * lax.rsqrt(ms + EPS) * nw * _silu(zl_ref[lo:lo + n, :])
        yield

    z = jnp.zeros((LANES, LANES), F32)
    _run_pipeline(n_grp, stage_a, stage_b, stage_c, (z, z), fill=fill,
                  drain=_drain_streams(tiles, sched, n_grp, out_tile))


def _gdn_expand_consts():
    e = np.zeros((N_PAIRS, BA_W, 4 * LANES), np.float32)
    for p in range(N_PAIRS):
        for blk in range(4):
            d, is_g = blk % 2, blk // 2
            for h in range(2):
                col = is_g * 2 * GDN_HEADS + d * GDN_HEADS + 2 * p + h
                e[p, col, blk * LANES + h * HD:blk * LANES + (h + 1) * HD] = 1.0
    return e


def _gdn_mixer(qkv_c, z_c, ba_c, qkv_l, z_l, ba_l, conv_w, a_log, dt_bias, norm_w, consts):
    bsz, tc, _ = qkv_c.shape
    tl = qkv_l.shape[1]
    tall = tc + tl
    n_all = tall // CH
    lane_pad = lambda a: jnp.pad(a.reshape(-1), (2 * GDN_HEADS, LANES - 4 * GDN_HEADS))
    aux = _rows([lane_pad(a_log), lane_pad(dt_bias), jnp.tile(norm_w, 2)], SUBLANES)
    e = jnp.asarray(_gdn_expand_consts())
    cm = lambda *blk: pl.BlockSpec(blk, lambda b, p: (0,) * len(blk))

    def tok(t, col0):
        return pl.BlockSpec((None, t, LANES), lambda b, p: (b, 0, col0 + p))
    in_specs = [tok(tc, 0), tok(tc, N_PAIRS), tok(tc, 2 * N_PAIRS), tok(tc, 0),
                pl.BlockSpec((None, tc, BA_W), lambda b, p: (b, 0, 0)),
                tok(tl, 0), tok(tl, N_PAIRS), tok(tl, 2 * N_PAIRS), tok(tl, 0),
                pl.BlockSpec((None, tl, BA_W), lambda b, p: (b, 0, 0)),
                pl.BlockSpec((CONV_W, LANES), lambda b, p: (0, p)),
                pl.BlockSpec((CONV_W, LANES), lambda b, p: (0, N_PAIRS + p)),
                pl.BlockSpec((CONV_W, LANES), lambda b, p: (0, 2 * N_PAIRS + p)),
                cm(SUBLANES, LANES),
                pl.BlockSpec((None, BA_W, 4 * LANES), lambda b, p: (p, 0, 0)),
                cm(2, CH, LANES), cm(2, CH, LANES), cm(2, 6, CH, LANES), cm(2, 1, LANES), cm(LANES, LANES),
                cm(2, ROWT, ROWT)]
    pad_rows = tall + 3 * GAP
    scratch = [pltpu.VMEM((pad_rows, LANES), F32)] * 3 + [pltpu.VMEM((tall, LANES), F32)] * 3 + [
        pltpu.VMEM((2, 3, tall, LANES), F32),
        pltpu.VMEM((2, _group_size(n_all), 2, 2 * CH, LANES), F32),
        pltpu.VMEM((2, _group_size(n_all), 2, 2 * LANES + 2 * CH + SUBLANES, LANES), F32),
        pltpu.VMEM((tall, LANES), F32), pltpu.VMEM((tall, LANES), F32)]
    return pl.pallas_call(
        _gdn_kernel,
        grid=(bsz, N_PAIRS),
        in_specs=in_specs,
        out_specs=[pl.BlockSpec((None, tc, LANES), lambda b, p: (b, 0, p)),
                   pl.BlockSpec((None, tl, LANES), lambda b, p: (b, 0, p))],
        out_shape=[jax.ShapeDtypeStruct((bsz, tc, GDN_WIDTH), F32),
                   jax.ShapeDtypeStruct((bsz, tl, GDN_WIDTH), F32)],
        scratch_shapes=scratch,
        compiler_params=pltpu.CompilerParams(dimension_semantics=("arbitrary", "arbitrary"),
                                             vmem_limit_bytes=VMEM_LIMIT),
        name="gdn_mixer",
    )(qkv_c, qkv_c, qkv_c, z_c, ba_c, qkv_l, qkv_l, qkv_l, z_l, ba_l,
      conv_w, conv_w, conv_w, aux, e,
      consts["incl"], consts["strict"], consts["lvl"], consts["hm"], consts["bdm"],
      consts["tri_t"])


def _rwkv_kernel(rc_ref, kc_ref, vc_ref, wdc_ref, adc_ref, gdc_ref,
                 rl_ref, kl_ref, vl_ref, wdl_ref, adl_ref, gdl_ref,
                 mur_ref, muk_ref, muv_ref, mux_ref, wup_ref, aup_ref, gup_ref, vec_ref,
                 incl_ref, strict_ref, lvl_ref, hm_ref, bdm_ref, trit_ref,
                 oc_ref, ol_ref,
                 vs_ref, gs_ref, bvs_ref, dir_ref, ls_ref, p1_ref, yf_ref, yb_ref):
    tc, tl = rc_ref.shape[0], rl_ref.shape[0]
    tall = tc + tl
    n_ctx, n_all = tc // CH, tall // CH
    grp = _group_size(n_all)
    m0, m1 = hm_ref[0].astype(BF16), hm_ref[1].astype(BF16)
    m0w = jnp.concatenate([m0, m0], axis=1)
    m1w = jnp.concatenate([m1, m1], axis=1)
    bdm = bdm_ref[...]

    kkw, kaw, rkw = vec_ref[0:1, :], vec_ref[1:2, :], vec_ref[2:3, :]

    def prep_tile(_, uo, n):
        ctx_tile = uo < tc
        lo = uo if ctx_tile else uo - tc
        t_len = tc if ctx_tile else tl
        rowi = lax.broadcasted_iota(jnp.int32, (n, LANES), 0)

        def lerp(src_c, src_l, mu):
            src = src_c if ctx_tile else src_l
            x = src[lo:lo + n, :]
            if lo > 0:
                xm = src[lo - 1:lo - 1 + n, :]
            else:
                xm = jnp.where(rowi == 0, 0.0, pltpu.roll(x, 1, 0))
            if lo + n < t_len:
                xp = src[lo + 1:lo + 1 + n, :]
            else:
                xp = jnp.where(rowi == n - 1, 0.0, pltpu.roll(x, n - 1, 0))
            return x + mu * (0.5 * (xm + xp) - x)
        r = lerp(rc_ref, rl_ref, mur_ref[...])
        k = lerp(kc_ref, kl_ref, muk_ref[...])
        v = lerp(vc_ref, vl_ref, muv_ref[...])
        wd = lerp(wdc_ref, wdl_ref, mux_ref[0:1, :])
        ad = lerp(adc_ref, adl_ref, mux_ref[1:2, :])
        gd = lerp(gdc_ref, gdl_ref, mux_ref[2:3, :])
        tw = jnp.tanh(wd)
        kkv = k * kkw
        yield
        kk = kkv * lax.rsqrt(_mm(kkv * kkv, bdm) + 1e-6)
        ksum = jnp.zeros_like(k)
        for d in range(2):
            lw = -0.6065306597126334 * _sigmoid(vec_ref[3 + d:4 + d, :] + _mm(tw, wup_ref[d]))
            a = _sigmoid(vec_ref[5 + d:6 + d, :] + _mm(ad, aup_ref[d]))
            kdir = k * (1.0 + (a - 1.0) * kaw)
            ksum = ksum + kdir
            cum = _mm_sel_l(trit_ref[d, 0:n, 0:n], lw)
            einv = jnp.exp(-cum)
            dir_ref[d, 0, uo:uo + n, :] = cum
            dir_ref[d, 1, uo:uo + n, :] = kk * jnp.exp(cum - lw)
            dir_ref[d, 2, uo:uo + n, :] = r * jnp.exp(cum)
            dir_ref[d, 3, uo:uo + n, :] = kk * a * einv
            dir_ref[d, 4, uo:uo + n, :] = kdir * einv
            yield
        bonus = _mm_sel_r(r * ksum * rkw, bdm)
        vs_ref[uo:uo + n, :] = v
        gs_ref[uo:uo + n, :] = _mm(_sigmoid(gd), gup_ref[...])
        bvs_ref[uo:uo + n, :] = bonus * v
        yield

    tiles = _row_tiles(tc, tl)
    sched = _tile_schedule(tc, tl, grp)
    for tile, (need, _) in zip(tiles, sched):
        if need == 0:
            _round_robin([prep_tile(*tile)])
    fill = ([_chain([prep_tile(*t) for t, (need, _) in zip(tiles, sched) if need == 1])],
            [_chain([prep_tile(*t) for t, (need, _) in zip(tiles, sched) if need >= 2])])

    n_grp = n_all // grp
    pd = [(i, d) for i in range(grp) for d in range(2)]
    chunk_rows = lambda gi, i, d: _chunk_rows(gi, grp, i, d, n_ctx, n_all)

    def stage_a(gi, slot):
        def finish(pi, pdir, m, vbd):
            tri2 = jnp.concatenate([strict_ref[pdir], incl_ref[pdir]], axis=0)
            av = _mm(m[:, LANES:] * tri2, vbd)
            ab = m[:, :LANES] * tri2
            ls_ref[slot, pi, pdir, 0:CH, :] = ab[:CH]
            ls_ref[slot, pi, pdir, CH:2 * CH, :] = av[:CH]
            ls_ref[slot, pi, pdir, 2 * CH:3 * CH, :] = ab[CH:]
            ls_ref[slot, pi, pdir, 3 * CH:4 * CH, :] = av[CH:]
        pending = None
        for i, d in pd:
            rows = chunk_rows(gi, i, d)
            kkq, rq, binv, kinv = (dir_ref[d, j, rows, :] for j in range(1, 5))
            lhs = jnp.concatenate([kkq, rq], axis=0)
            rhs = jnp.concatenate([_bd(binv, m0, m1), _bd(kinv, m0, m1)], axis=0)
            cur = (i, d, _mm_nt(lhs, rhs), _bd(vs_ref[rows, :], m0, m1))
            if pending is not None:
                finish(*pending)
            pending = cur
            yield
        finish(*pending)
        yield

    def stage_b(gi, slot):
        mats = [ls_ref[slot, i, d, 0:CH, :] for i, d in pd]
        es = []
        yield from _inv_levels(mats, [d for _, d in pd], lvl_ref, m0, m1, es)
        sols = []
        for e, (i, d) in zip(es, pd):
            rows = chunk_rows(gi, i, d)
            rhs = jnp.concatenate([dir_ref[d, 1, rows, :], ls_ref[slot, i, d, CH:2 * CH, :]], axis=1)
            sols.append(rhs + _mm(e, _bd(rhs, m0w, m1w)))
        yield
        for sol, (i, d) in zip(sols, pd):
            rows = chunk_rows(gi, i, d)
            cum = dir_ref[d, 0, rows, :]
            etot = jnp.exp(cum[CH - 1:CH, :] if d == 0 else cum[0:1, :])
            bdec, kdec = dir_ref[d, 3, rows, :] * etot, dir_ref[d, 4, rows, :] * etot
            ar = _mm(ls_ref[slot, i, d, 2 * CH:3 * CH, :], _bd(sol, m0w, m1w))
            pmat = _mm_tn(sol[:, :LANES], bdec)
            nmat = _mm_tn(jnp.concatenate([-sol[:, LANES:], vs_ref[rows, :]], axis=0),
                          jnp.concatenate([bdec, kdec], axis=0))
            p1_ref[slot, i, d, 0:LANES, :] = pmat * bdm
            p1_ref[slot, i, d, LANES:2 * LANES, :] = nmat * bdm
            p1_ref[slot, i, d, 2 * LANES:2 * LANES + CH, :] = dir_ref[d, 2, rows, :] - ar[:, :LANES]
            p1_ref[slot, i, d, 2 * LANES + CH:2 * LANES + 2 * CH, :] = (
                ls_ref[slot, i, d, 3 * CH:4 * CH, :] - ar[:, LANES:])
            p1_ref[slot, i, d, 2 * LANES + 2 * CH:2 * LANES + 2 * CH + SUBLANES, :] = jnp.broadcast_to(
                etot, (SUBLANES, LANES))
        yield

    def stage_c(gi, slot, st):
        for i, d in pd:
            rows = chunk_rows(gi, i, d)
            pmat = p1_ref[slot, i, d, 0:LANES, :]
            nmat = p1_ref[slot, i, d, LANES:2 * LANES, :]
            rmat = p1_ref[slot, i, d, 2 * LANES:2 * LANES + CH, :]
            ymat = p1_ref[slot, i, d, 2 * LANES + CH:2 * LANES + 2 * CH, :]
            etot = p1_ref[slot, i, d, 2 * LANES + 2 * CH:2 * LANES + 2 * CH + 1, :]
            yref = yf_ref if d == 0 else yb_ref
            yref[rows, :] = _mm_nt(rmat, st[d]) + ymat
            st[d] = st[d] * etot - _mm(st[d], pmat) + nmat
            yield

    gnw, gnb = vec_ref[7:8, :], vec_ref[8:9, :]

    def out_tile(_, uo, n):
        y = yf_ref[uo:uo + n, :] + yb_ref[uo:uo + n, :]
        mean = _mm_sel_r(y, bdm) * (1.0 / HD)
        yield
        yc = y - mean
        var = _mm(yc * yc, bdm) * (1.0 / HD)
        yield
        out = (yc * lax.rsqrt(var + RWKV_GN_EPS) * gnw + gnb + bvs_ref[uo:uo + n, :]) * gs_ref[uo:uo + n, :]
        if uo < tc:
            oc_ref[uo:uo + n, :] = out
        else:
            ol_ref[uo - tc:uo - tc + n, :] = out
        yield

    z = jnp.zeros((LANES, LANES), F32)
    _run_pipeline(n_grp, stage_a, stage_b, stage_c, (z, z), b_before_a=False, fill=fill,
                  drain=_drain_streams(tiles, sched, n_grp, out_tile))


def _rwkv_mixer(rw_c, rw_l, mu, w0, w_up, a0, a_up, g_up, k_k, k_a, r_k, gn_w, gn_b, consts):
    bsz, tc, _ = rw_c.shape
    tl = rw_l.shape[1]
    tall = tc + tl
    n_all = tall // CH
    W = RWKV_WIDTH
    nb = W // LANES
    zr = jnp.zeros((RWKV_RANK, W), F32)
    pad_dir = lambda u: jnp.stack([jnp.concatenate([u[0], zr], axis=0), jnp.concatenate([zr, u[1]], axis=0)])
    wup, aup = pad_dir(w_up), pad_dir(a_up)
    vec = _rows([k_k, k_a, r_k.reshape(-1), w0[0], w0[1], a0[0], a0[1], gn_w, gn_b], 16)
    mu_rkv = mu[:3 * W].reshape(3, 1, W)
    mux = jnp.pad(mu[3 * W:].reshape(3, LANES), ((0, SUBLANES - 3), (0, 0)))
    cm = lambda *blk: pl.BlockSpec(blk, lambda b, p: (0,) * len(blk))

    def tok(t, col0, per_pair=True):
        if per_pair:
            return pl.BlockSpec((None, t, LANES), lambda b, p: (b, 0, col0 + p))
        return pl.BlockSpec((None, t, LANES), lambda b, p: (b, 0, col0))
    stream = lambda t: [tok(t, 0), tok(t, nb), tok(t, 2 * nb), tok(t, 3 * nb, False),
                        tok(t, 3 * nb + 1, False), tok(t, 3 * nb + 2, False)]
    in_specs = stream(tc) + stream(tl) + [
        pl.BlockSpec((None, 1, LANES), lambda b, p: (0, 0, p)),
        pl.BlockSpec((None, 1, LANES), lambda b, p: (1, 0, p)),
        pl.BlockSpec((None, 1, LANES), lambda b, p: (2, 0, p)),
        cm(SUBLANES, LANES),
        pl.BlockSpec((2, LANES, LANES), lambda b, p: (0, 0, p)),
        pl.BlockSpec((2, LANES, LANES), lambda b, p: (0, 0, p)),
        pl.BlockSpec((RWKV_G_RANK, LANES), lambda b, p: (0, p)),
        pl.BlockSpec((16, LANES), lambda b, p: (0, p)),
        cm(2, CH, LANES), cm(2, CH, LANES), cm(2, 6, CH, LANES), cm(2, 1, LANES), cm(LANES, LANES),
        cm(2, ROWT, ROWT)]
    scratch = [pltpu.VMEM((tall, LANES), F32)] * 3 + [
        pltpu.VMEM((2, 5, tall, LANES), F32),
        pltpu.VMEM((2, _group_size(n_all), 2, 4 * CH, LANES), F32),
        pltpu.VMEM((2, _group_size(n_all), 2, 2 * LANES + 2 * CH + SUBLANES, LANES), F32),
        pltpu.VMEM((tall, LANES), F32), pltpu.VMEM((tall, LANES), F32)]
    return pl.pallas_call(
        _rwkv_kernel,
        grid=(bsz, N_PAIRS),
        in_specs=in_specs,
        out_specs=[pl.BlockSpec((None, tc, LANES), lambda b, p: (b, 0, p)),
                   pl.BlockSpec((None, tl, LANES), lambda b, p: (b, 0, p))],
        out_shape=[jax.ShapeDtypeStruct((bsz, tc, W), F32),
                   jax.ShapeDtypeStruct((bsz, tl, W), F32)],
        scratch_shapes=scratch,
        compiler_params=pltpu.CompilerParams(dimension_semantics=("arbitrary", "arbitrary"),
                                             vmem_limit_bytes=VMEM_LIMIT),
        name="rwkv_mixer",
    )(*([rw_c] * 6), *([rw_l] * 6), mu_rkv, mu_rkv, mu_rkv, mux, wup, aup, g_up, vec,
      consts["incl"], consts["strict"], consts["lvl"], consts["hm"], consts["bdm"], consts["tri_t"])


def _lru_kernel(xc_ref, gc_ref, xl_ref, gl_ref, cw_ref, vec_ref, wcat_ref, bcat_ref,
                oc_ref, ol_ref, xp_ref, ab_ref, hf_ref, hb_ref):
    tc, tl = xc_ref.shape[0], xl_ref.shape[0]
    tall = tc + tl
    rows = tl // GRID_W
    W = xc_ref.shape[1]
    z = jnp.zeros((GAP, W), F32)
    xp_ref[0:GAP, :] = z
    xp_ref[GAP:GAP + tc, :] = xc_ref[...]
    xp_ref[GAP + tc:2 * GAP + tc, :] = z
    base = 2 * GAP + tc
    for c in range(GRID_W):
        xp_ref[base + c * rows:base + (c + 1) * rows, :] = xl_ref[pl.ds(c, rows, stride=GRID_W), :]
    xp_ref[base + tl:base + tl + GAP, :] = z
    cb = vec_ref[0:1, :]
    for (po, uo, n) in _row_tiles(tc, tl):
        acc = cb + cw_ref[0:1, :] * xp_ref[po - 2:po - 2 + n, :]
        for tap in range(1, CONV_W):
            acc = acc + cw_ref[tap:tap + 1, :] * xp_ref[po - 2 + tap:po - 2 + tap + n, :]
        gates = _mm(acc, wcat_ref[...]) + bcat_ref[...]
        for d in range(2):
            rg = _sigmoid(gates[:, (2 * d) * W:(2 * d + 1) * W])
            ig = _sigmoid(gates[:, (2 * d + 1) * W:(2 * d + 2) * W])
            log_a = -LRU_C * rg * _softplus(-vec_ref[1 + d:2 + d, :])
            a = jnp.exp(log_a)
            m2 = -jnp.tanh(log_a) * (1.0 + a * a)
            mult = jnp.where(m2 > 0.0, m2 * lax.rsqrt(m2), 0.0)
            ab_ref[d, 0, uo:uo + n, :] = a
            ab_ref[d, 1, uo:uo + n, :] = mult * (ig * acc)

    sub = lax.broadcasted_iota(jnp.int32, (SUBLANES, W), 0)
    n_tiles_c, n_tiles = tc // SUBLANES, tall // SUBLANES

    def tile_scan(a, b, d):
        for sh in (1, 2, 4):
            if d == 0:
                ok = sub >= sh
                a_s = jnp.where(ok, pltpu.roll(a, sh, 0), 1.0)
                b_s = jnp.where(ok, pltpu.roll(b, sh, 0), 0.0)
            else:
                ok = sub < SUBLANES - sh
                a_s = jnp.where(ok, pltpu.roll(a, SUBLANES - sh, 0), 1.0)
                b_s = jnp.where(ok, pltpu.roll(b, SUBLANES - sh, 0), 0.0)
            b = b + a * b_s
            a = a * a_s
        return a, b

    unroll = 4 if (n_tiles_c % 4 == 0 and n_tiles % 4 == 0) else 1

    def scan_body(s4, carry):
        hf, hb = carry
        tiles = []
        for j in range(unroll):
            s = s4 * unroll + j
            r0 = pl.multiple_of(s * SUBLANES, SUBLANES)
            tb = jnp.where(s < n_tiles_c, n_tiles_c - 1 - s, n_tiles + n_tiles_c - 1 - s)
            r1 = pl.multiple_of(tb * SUBLANES, SUBLANES)
            tiles.append((r0, tile_scan(ab_ref[0, 0, pl.ds(r0, SUBLANES), :], ab_ref[0, 1, pl.ds(r0, SUBLANES), :], 0),
                          r1, tile_scan(ab_ref[1, 0, pl.ds(r1, SUBLANES), :], ab_ref[1, 1, pl.ds(r1, SUBLANES), :], 1)))
        for r0, (af, bf), r1, (ab, bb) in tiles:
            h = bf + af * hf
            hf_ref[pl.ds(r0, SUBLANES), :] = h
            hf = jnp.broadcast_to(h[SUBLANES - 1:SUBLANES, :], (SUBLANES, W))
            h = bb + ab * hb
            hb_ref[pl.ds(r1, SUBLANES), :] = h
            hb = jnp.broadcast_to(h[0:1, :], (SUBLANES, W))
        return hf, hb
    z8 = jnp.zeros((SUBLANES, W), F32)
    lax.fori_loop(0, n_tiles // unroll, scan_body, (z8, z8))

    oc_ref[...] = (hf_ref[0:tc, :] + hb_ref[0:tc, :]) * _gelu_tanh(gc_ref[...])
    for c in range(GRID_W):
        h = hf_ref[tc + c * rows:tc + (c + 1) * rows, :] + hb_ref[tc + c * rows:tc + (c + 1) * rows, :]
        ol_ref[pl.ds(c, rows, stride=GRID_W), :] = h * _gelu_tanh(gl_ref[pl.ds(c, rows, stride=GRID_W), :])


def _lru_mixer(x_c, g_c, x_l, g_l, conv_w, conv_b, w_a, b_a, w_x, b_x, lam):
    bsz, tc, W = x_c.shape
    tl = x_l.shape[1]
    tall = tc + tl
    nh = W // LANES
    bph = LANES // LRU_BW
    w6 = jnp.stack([w_a, w_x], axis=1).reshape(2, 2, nh, bph, LRU_BW, LRU_BW)
    wcat = jnp.einsum("dghnrc,nm->hnrdgmc", w6, jnp.eye(bph, dtype=F32)).reshape(nh, LANES, 4 * LANES)
    bcat = jnp.stack([b_a, b_x], axis=1).reshape(2, 2, nh, LANES).transpose(2, 0, 1, 3).reshape(nh, 1, 4 * LANES)
    vec = _rows([conv_b, lam[0], lam[1]], SUBLANES)
    tok = lambda t: pl.BlockSpec((None, t, LANES), lambda b, h: (b, 0, h))
    return pl.pallas_call(
        _lru_kernel,
        grid=(bsz, nh),
        in_specs=[tok(tc), tok(tc), tok(tl), tok(tl),
                  pl.BlockSpec((CONV_W, LANES), lambda b, h: (0, h)),
                  pl.BlockSpec((SUBLANES, LANES), lambda b, h: (0, h)),
                  pl.BlockSpec((None, LANES, 4 * LANES), lambda b, h: (h, 0, 0)),
                  pl.BlockSpec((None, 1, 4 * LANES), lambda b, h: (h, 0, 0))],
        out_specs=[tok(tc), tok(tl)],
        out_shape=[jax.ShapeDtypeStruct((bsz, tc, W), F32), jax.ShapeDtypeStruct((bsz, tl, W), F32)],
        scratch_shapes=[pltpu.VMEM((tall + 3 * GAP, LANES), F32),
                        pltpu.VMEM((2, 2, tall, LANES), F32),
                        pltpu.VMEM((tall, LANES), F32), pltpu.VMEM((tall, LANES), F32)],
        compiler_params=pltpu.CompilerParams(dimension_semantics=("arbitrary", "arbitrary"),
                                             vmem_limit_bytes=VMEM_LIMIT),
        name="lru_mixer",
    )(x_c, g_c, x_l, g_l, conv_w, vec, wcat.astype(BF16), bcat)


def _finish_kernel(x_ref, gdn_ref, lru_ref, rwk_ref, m2_ref, m3_ref, m4_ref, m5_ref, nrm_ref,
                   wo_ref, up_ref, dn_ref, o_ref):
    x = x_ref[...]
    o = (jnp.dot(gdn_ref[...].astype(BF16), wo_ref[0:GDN_WIDTH, :], preferred_element_type=F32)
         + jnp.dot(lru_ref[...].astype(BF16), wo_ref[GDN_WIDTH:GDN_WIDTH + LRU_WIDTH, :],
                   preferred_element_type=F32)
         + jnp.dot(rwk_ref[...].astype(BF16), wo_ref[GDN_WIDTH + LRU_WIDTH:, :], preferred_element_type=F32))
    x = x + m2_ref[...] * _rms(o, nrm_ref[0:1, :])
    h = (_rms(x, nrm_ref[1:2, :]) * (1.0 + m4_ref[...]) + m3_ref[...]).astype(BF16)
    f = jnp.zeros_like(x)
    fc = 1024
    for j in range(D_FF // fc):
        a = jnp.maximum(jnp.dot(h, up_ref[:, j * fc:(j + 1) * fc], preferred_element_type=F32), 0.0)
        f = f + jnp.dot((a * a).astype(BF16), dn_ref[j * fc:(j + 1) * fc, :], preferred_element_type=F32)
    o_ref[...] = x + m5_ref[...] * _rms(f, nrm_ref[2:3, :])


def _finish(x2, gdn, lru, rwk, mod_rows, rows_per_mod, norms, wo_bf, up_bf, dn_bf, layer):
    n = x2.shape[0]
    tm = DENSE_TM
    tiles_per_mod = rows_per_mod // tm
    modspec = lambda k: pl.BlockSpec((None, 1, D_MODEL), lambda i: (6 * (i // tiles_per_mod) + k, 0, 0))
    cm = lambda *blk: pl.BlockSpec((None,) + blk, lambda i: (layer,) + (0,) * len(blk),
                                   pipeline_mode=pl.Buffered(1))
    return pl.pallas_call(
        _finish_kernel,
        grid=(n // tm,),
        in_specs=[pl.BlockSpec((tm, D_MODEL), lambda i: (i, 0)),
                  pl.BlockSpec((tm, GDN_WIDTH), lambda i: (i, 0)),
                  pl.BlockSpec((tm, LRU_WIDTH), lambda i: (i, 0)),
                  pl.BlockSpec((tm, RWKV_WIDTH), lambda i: (i, 0)),
                  modspec(2), modspec(3), modspec(4), modspec(5),
                  cm(SUBLANES, D_MODEL), cm(D_MODEL, D_MODEL), cm(D_MODEL, D_FF), cm(D_FF, D_MODEL)],
        out_specs=pl.BlockSpec((tm, D_MODEL), lambda i: (i, 0)),
        out_shape=jax.ShapeDtypeStruct((n, D_MODEL), F32),
        compiler_params=pltpu.CompilerParams(dimension_semantics=("arbitrary",),
                                             vmem_limit_bytes=VMEM_LIMIT),
        name="finish",
    )(x2, gdn, lru, rwk, mod_rows, mod_rows, mod_rows, mod_rows, norms, wo_bf, up_bf, dn_bf)


def _arrange_w_in(w):
    s = np.cumsum([0, 3 * GDN_WIDTH, GDN_WIDTH, 2 * GDN_HEADS, 2 * GDN_HEADS, LRU_WIDTH, LRU_WIDTH, RWKV_IN])
    pad = jnp.zeros(w.shape[:-1] + (BA_W - 4 * GDN_HEADS,), w.dtype)
    return jnp.concatenate([w[..., s[0]:s[4]], pad, w[..., s[4]:]], axis=-1)


def kernel(x, c, ctx, c_ctx, ada_w, ada_b, norm_mix_pre, norm_mix_post, norm_ffn_pre, norm_ffn_post, w_in, gdn_conv, gdn_a_log, gdn_dt_bias, gdn_norm, lru_conv, lru_conv_b, lru_wa, lru_ba, lru_wx, lru_bx, lru_lambda, rwkv_mu, rwkv_w0, rwkv_w_up, rwkv_a0, rwkv_a_up, rwkv_g_up, rwkv_k_k, rwkv_k_a, rwkv_r_k, rwkv_gn_w, rwkv_gn_b, w_out, ffn_up, ffn_down):
    bsz, tl, _ = x.shape
    tc = ctx.shape[1]
    depth = w_in.shape[0]
    consts = _device_consts()

    cvec = jnp.pad(jnp.concatenate([c, c_ctx[None, :]], axis=0), ((0, 16 - bsz - 1), (0, 0)))
    mods = _ada_mod(cvec, ada_w, ada_b).reshape(depth, 16, 6, D_MODEL)

    w_in_bf = _arrange_w_in(w_in.astype(BF16))
    wo_bf, up_bf, dn_bf = w_out.astype(BF16), ffn_up.astype(BF16), ffn_down.astype(BF16)
    g_pre = norm_mix_pre.reshape(depth, 1, D_MODEL)
    norms = jnp.pad(jnp.stack([norm_mix_post, norm_ffn_pre, norm_ffn_post], axis=1),
                    ((0, 0), (0, SUBLANES - 3), (0, 0)))

    xl = x.reshape(bsz * tl, D_MODEL)
    xc = ctx.reshape(bsz * tc, D_MODEL)
    for i in range(depth):
        mod_l = mods[i, 0:bsz].reshape(bsz * 6, 1, D_MODEL)
        mod_c = mods[i, bsz:bsz + 1].reshape(6, 1, D_MODEL)
        pl_ = _inproj(xl, mod_l, tl, g_pre, w_in_bf, i)
        pc_ = _inproj(xc, mod_c, bsz * tc, g_pre, w_in_bf, i)
        r3 = lambda a, t: a.reshape(bsz, t, a.shape[-1])
        qkv_l, z_l, ba_l, lx_l, lg_l, rw_l = (r3(a, tl) for a in pl_)
        qkv_c, z_c, ba_c, lx_c, lg_c, rw_c = (r3(a, tc) for a in pc_)

        gdn_c, gdn_l = _gdn_mixer(qkv_c, z_c, ba_c, qkv_l, z_l, ba_l, gdn_conv[i], gdn_a_log[i],
                                  gdn_dt_bias[i], gdn_norm[i], consts)
        lru_c, lru_l = _lru_mixer(lx_c, lg_c, lx_l, lg_l, lru_conv[i], lru_conv_b[i], lru_wa[i], lru_ba[i],
                                  lru_wx[i], lru_bx[i], lru_lambda[i])
        rwk_c, rwk_l = _rwkv_mixer(rw_c, rw_l, rwkv_mu[i], rwkv_w0[i], rwkv_w_up[i], rwkv_a0[i],
                                   rwkv_a_up[i], rwkv_g_up[i], rwkv_k_k[i], rwkv_k_a[i], rwkv_r_k[i],
                                   rwkv_gn_w[i], rwkv_gn_b[i], consts)

        f2 = lambda a: a.reshape(-1, a.shape[-1])
        xl = _finish(xl, f2(gdn_l), f2(lru_l), f2(rwk_l), mod_l, tl, norms, wo_bf, up_bf, dn_bf, i)
        if i < depth - 1:
            xc = _finish(xc, f2(gdn_c), f2(lru_c), f2(rwk_c), mod_c, bsz * tc, norms, wo_bf, up_bf, dn_bf, i)
    return xl.reshape(bsz, tl, D_MODEL)
```

```python
import functools

import numpy as np
import jax
import jax.numpy as jnp
from jax import lax
from jax.experimental import pallas as pl
from jax.experimental.pallas import tpu as pltpu

F32 = jnp.float32
BF16 = jnp.bfloat16

LANES = 128
SUBLANES = 8
VMEM_LIMIT = 56 * 1024 * 1024

D_MODEL = 1024
DEPTH = 2
GRID_W = 64
CONV_W = 4
EPS = 1e-6
D_FF = 4 * D_MODEL
HD = 64
GDN_WIDTH = 3 * D_MODEL // 8
GDN_HEADS = GDN_WIDTH // HD
LRU_WIDTH = D_MODEL // 4
LRU_BLOCKS = 4
LRU_BW = LRU_WIDTH // LRU_BLOCKS
LRU_C = 8.0
RWKV_WIDTH = D_MODEL - GDN_WIDTH - LRU_WIDTH
RWKV_HEADS = RWKV_WIDTH // HD
RWKV_RANK = 64
RWKV_G_RANK = 128
RWKV_GN_EPS = 6.4e-4
RWKV_IN = 3 * RWKV_WIDTH + 2 * RWKV_RANK + 2 * RWKV_RANK + RWKV_G_RANK
N_PAIRS = GDN_HEADS // 2
BA_W = LANES
_W_OFF = np.cumsum([0, 3 * GDN_WIDTH, GDN_WIDTH, 2 * GDN_HEADS, 2 * GDN_HEADS, LRU_WIDTH, LRU_WIDTH, RWKV_IN])
D_INP = int(_W_OFF[-1])
P_SLICES = [(0, 3 * GDN_WIDTH), (int(_W_OFF[1]), GDN_WIDTH), (int(_W_OFF[2]), BA_W),
            (int(_W_OFF[4]), LRU_WIDTH), (int(_W_OFF[5]), LRU_WIDTH), (int(_W_OFF[6]), RWKV_IN)]

CH = 64
ROWT = 256
DENSE_TM = 512
GAP = SUBLANES


def _mm(a, b):
    return jnp.dot(a.astype(BF16), b.astype(BF16), preferred_element_type=F32)


def _mm_nt(a, b):
    return lax.dot_general(a.astype(BF16), b.astype(BF16), (((1,), (1,)), ((), ())),
                           preferred_element_type=F32)


def _mm_tn(a, b):
    return lax.dot_general(a.astype(BF16), b.astype(BF16), (((0,), (0,)), ((), ())),
                           preferred_element_type=F32)


def _split2(x):
    hi = x.astype(BF16)
    lo = (x - hi.astype(F32)).astype(BF16)
    return hi, lo


def _mm_sel_l(m01, x):
    mb = m01.astype(BF16)
    h, l = _split2(x)
    d = functools.partial(jnp.dot, preferred_element_type=F32)
    return d(mb, h) + d(mb, l)


def _mm_sel_r(x, m01):
    mb = m01.astype(BF16)
    h, l = _split2(x)
    d = functools.partial(jnp.dot, preferred_element_type=F32)
    return d(h, mb) + d(l, mb)


def _sigmoid(x):
    return 0.5 * jnp.tanh(0.5 * x) + 0.5


def _silu_of_twice(h):
    return h + h * jnp.tanh(h)


def _silu(x):
    return _silu_of_twice(0.5 * x)


def _softplus(x):
    return jnp.maximum(x, 0.0) + jnp.log(1.0 + jnp.exp(-jnp.abs(x)))


def _gelu_tanh(x):
    return 0.5 * x * (1.0 + jnp.tanh(0.7978845608028654 * (x + 0.044715 * (x * x * x))))


def _bd(y, m0, m1):
    yb = y.astype(BF16)
    return jnp.concatenate([yb * m0, yb * m1], axis=0)


def _pair_consts():
    i = np.arange(CH)[:, None]
    j = (np.arange(LANES) % HD)[None, :]
    incl = np.stack([i >= j, i <= j]).astype(np.float32)
    strict = np.stack([i > j, i < j]).astype(np.float32)
    lvls = []
    for d in range(2):
        per = []
        for m in (1, 2, 4, 8, 16, 32):
            same = (i // (2 * m)) == (j // (2 * m))
            lo_i, lo_j = (i % (2 * m)) < m, (j % (2 * m)) < m
            off = same & (~lo_i) & lo_j if d == 0 else same & lo_i & (~lo_j)
            per.append(off)
        lvls.append(np.stack(per))
    lvl = np.stack(lvls).astype(np.float32)
    lane = np.arange(LANES)
    hm = np.stack([lane < HD, lane >= HD]).astype(np.float32)[:, None, :]
    bdm = ((np.arange(LANES)[:, None] // HD) == (lane[None, :] // HD)).astype(np.float32)
    r = np.arange(ROWT)[:, None]
    c = np.arange(ROWT)[None, :]
    same = (r // CH) == (c // CH)
    tri_t = np.stack([same & (r >= c), same & (r <= c)]).astype(np.float32)
    return dict(incl=incl, strict=strict, lvl=lvl, hm=hm, bdm=bdm, tri_t=tri_t)


def _device_consts():
    out = {k: jnp.asarray(v) for k, v in _pair_consts().items()}
    out["tri_t"] = out["tri_t"].astype(BF16)
    return out


def _inv_levels(mats, dirs, lvl_ref, m0, m1, out):
    es = [-(a * lvl_ref[d, 0]) for a, d in zip(mats, dirs)]
    for k in range(1, 6):
        offs = [a * lvl_ref[d, k] for a, d in zip(mats, dirs)]
        xs = [off + p for off, p in zip(offs, _mm_pairs(es, [_bd(off, m0, m1) for off in offs]))]
        yield
        es = [e - x - p for e, x, p in zip(es, xs, _mm_pairs(xs, [_bd(e, m0, m1) for e in es]))]
        yield
    out.extend(es)


def _mm_pairs(lhs, rhs):
    out = []
    for j in range(0, len(lhs) - 1, 2):
        r = jnp.dot(jnp.concatenate([lhs[j], lhs[j + 1]], axis=0).astype(BF16),
                    jnp.concatenate([rhs[j], rhs[j + 1]], axis=1), preferred_element_type=F32)
        out += [r[:CH, :LANES], r[CH:, LANES:]]
    if len(lhs) % 2:
        out.append(_mm(lhs[-1], rhs[-1]))
    return out


def _rows(vectors, n_rows):
    m = jnp.stack([v.astype(F32) for v in vectors])
    return jnp.pad(m, ((0, n_rows - m.shape[0]), (0, 0)))


def _round_robin(gens):
    gens = list(gens)
    while gens:
        alive = []
        for g in gens:
            try:
                next(g)
                alive.append(g)
            except StopIteration:
                pass
        gens = alive


def _group_size(n_all):
    for g in (6, 4, 3, 2):
        if n_all % g == 0:
            return g
    return 1


def _chunk_order(s, n_ctx, n_all, d):
    if d == 0:
        return s
    if isinstance(s, int):
        return n_ctx - 1 - s if s < n_ctx else n_all + n_ctx - 1 - s
    return jnp.where(s < n_ctx, n_ctx - 1 - s, n_all + n_ctx - 1 - s)


def _chunk_rows(gi, grp, i, d, n_ctx, n_all):
    c = _chunk_order(gi * grp + i, n_ctx, n_all, d)
    return pl.ds(c * CH if isinstance(c, int) else pl.multiple_of(c * CH, CH), CH)


def _run_pipeline(n_grp, stage_a, stage_b, stage_c, st, b_before_a=True, fill=((), ()), drain=((), ())):
    def step(t, st, do_a, do_b, do_c, extra=()):
        par = t % 2 if isinstance(t, int) else lax.rem(t, 2)
        st = list(st)
        gens = []
        if do_c:
            gens.append(stage_c(t - 2, par, st))
        ab = ([stage_b(t - 1, 1 - par)] if do_b else []) + ([stage_a(t, par)] if do_a else [])
        _round_robin(gens + (ab if b_before_a else ab[::-1]) + list(extra))
        return tuple(st)
    assert n_grp >= 2
    st = step(0, st, True, False, False, fill[0])
    st = step(1, st, True, True, False, fill[1])
    st = lax.fori_loop(2, n_grp, lambda t, s: step(t, s, True, True, True), st)
    st = step(n_grp, st, False, True, True, drain[0])
    return step(n_grp + 1, st, False, False, True, drain[1])


def _tile_schedule(tc, tl, grp):
    n_ctx, n_all = tc // CH, (tc + tl) // CH
    first, last = {}, {}
    for s in range(n_all):
        g = s // grp
        for d in range(2):
            c = _chunk_order(s, n_ctx, n_all, d)
            first[c] = min(first.get(c, g), g)
            last[c] = max(last.get(c, (0, 0)), (g + 2, 2 * (s % grp) + d))
    out = []
    for (_, uo, n) in _row_tiles(tc, tl):
        cs = range(uo // CH, (uo + n) // CH)
        out.append((min(first[c] for c in cs), max(last[c] for c in cs)))
    return out


def _chain(gens):
    for g in gens:
        yield from g


def _delayed(gen, rounds):
    for _ in range(rounds):
        yield
    yield from gen


def _drain_streams(tiles, sched, n_grp, out_tile):
    drain = ([], [])
    for t, (_, (step, sub)) in zip(tiles, sched):
        if step < n_grp:
            drain[0].append(out_tile(*t))
        else:
            drain[step - n_grp].append(_delayed(out_tile(*t), sub + 1))
    return drain


def _fill_padded(dst_ref, src_c_ref, src_l_ref, tc, tl):
    w = dst_ref.shape[1]
    z = jnp.zeros((GAP, w), F32)
    dst_ref[0:GAP, :] = z
    dst_ref[GAP:GAP + tc, :] = src_c_ref[...]
    dst_ref[GAP + tc:2 * GAP + tc, :] = z
    dst_ref[2 * GAP + tc:2 * GAP + tc + tl, :] = src_l_ref[...]
    dst_ref[2 * GAP + tc + tl:3 * GAP + tc + tl, :] = z


def _row_tiles(tc, tl):
    out = []
    for base_p, base_u, n in ((GAP, 0, tc), (2 * GAP + tc, tc, tl)):
        for t0 in range(0, n, ROWT):
            out.append((base_p + t0, base_u + t0, min(ROWT, n - t0)))
    return out


def _ada_kernel(c_ref, w_ref, b_ref, o_ref):
    c = c_ref[...]
    o_ref[...] = _mm(_silu(c), w_ref[...]) + b_ref[...]


def _ada_mod(cvec, ada_w, ada_b):
    L = ada_w.shape[0]
    n = ada_w.shape[2]
    tn = 1536
    return pl.pallas_call(
        _ada_kernel,
        grid=(L, n // tn),
        in_specs=[pl.BlockSpec((16, D_MODEL), lambda l, j: (0, 0)),
                  pl.BlockSpec((None, D_MODEL, tn), lambda l, j: (l, 0, j)),
                  pl.BlockSpec((None, 1, tn), lambda l, j: (l, 0, j))],
        out_specs=pl.BlockSpec((None, 16, tn), lambda l, j: (l, 0, j)),
        out_shape=jax.ShapeDtypeStruct((L, 16, n), F32),
        compiler_params=pltpu.CompilerParams(dimension_semantics=("arbitrary", "arbitrary"),
                                             vmem_limit_bytes=VMEM_LIMIT),
        name="ada_mod",
    )(cvec, ada_w, ada_b.reshape(L, 1, n))


def _rms(x, g):
    return x * lax.rsqrt(jnp.mean(x * x, axis=-1, keepdims=True) + EPS) * g


def _inproj_kernel(x_ref, sh_ref, sc_ref, g_ref, w_ref, qkv_ref, z_ref, ba_ref, lx_ref, lg_ref, rw_ref):
    h = _rms(x_ref[...], g_ref[...]) * (1.0 + sc_ref[...]) + sh_ref[...]
    p = jnp.dot(h.astype(BF16), w_ref[...], preferred_element_type=F32)
    for ref, (start, width) in zip((qkv_ref, z_ref, ba_ref, lx_ref, lg_ref, rw_ref), P_SLICES):
        ref[...] = p[:, start:start + width]


def _inproj(x2, mod_rows, rows_per_mod, g_all, w_all, layer):
    n = x2.shape[0]
    tm = DENSE_TM
    tiles_per_mod = rows_per_mod // tm
    widths = [w for _, w in P_SLICES]
    return pl.pallas_call(
        _inproj_kernel,
        grid=(n // tm,),
        in_specs=[pl.BlockSpec((tm, D_MODEL), lambda i: (i, 0)),
                  pl.BlockSpec((None, 1, D_MODEL), lambda i: (6 * (i // tiles_per_mod), 0, 0)),
                  pl.BlockSpec((None, 1, D_MODEL), lambda i: (6 * (i // tiles_per_mod) + 1, 0, 0)),
                  pl.BlockSpec((None, 1, D_MODEL), lambda i: (layer, 0, 0)),
                  pl.BlockSpec((None, D_MODEL, D_INP), lambda i: (layer, 0, 0), pipeline_mode=pl.Buffered(1))],
        out_specs=[pl.BlockSpec((tm, w), lambda i: (i, 0)) for w in widths],
        out_shape=[jax.ShapeDtypeStruct((n, w), F32) for w in widths],
        compiler_params=pltpu.CompilerParams(dimension_semantics=("arbitrary",),
                                             vmem_limit_bytes=VMEM_LIMIT),
        name="inproj",
    )(x2, mod_rows, mod_rows, g_all, w_all)


def _gdn_kernel(qc_ref, kc_ref, vc_ref, zc_ref, bac_ref, ql_ref, kl_ref, vl_ref, zl_ref, bal_ref,
                cwq_ref, cwk_ref, cwv_ref, aux_ref, e_ref,
                incl_ref, strict_ref, lvl_ref, hm_ref, bdm_ref, trit_ref,
                oc_ref, ol_ref,
                qp_ref, kp_ref, vp_ref, qs_ref, ks_ref, vs_ref, dir_ref, ls_ref, p1_ref, of_ref, ob_ref):
    tc, tl = qc_ref.shape[0], ql_ref.shape[0]
    tall = tc + tl
    n_ctx, n_all = tc // CH, tall // CH
    grp = _group_size(n_all)
    m0, m1 = hm_ref[0].astype(BF16), hm_ref[1].astype(BF16)
    m0w = jnp.concatenate([m0, m0], axis=1)
    m1w = jnp.concatenate([m1, m1], axis=1)
    bdm = bdm_ref[...]

    _fill_padded(qp_ref, qc_ref, ql_ref, tc, tl)
    _fill_padded(kp_ref, kc_ref, kl_ref, tc, tl)
    _fill_padded(vp_ref, vc_ref, vl_ref, tc, tl)
    lane = lax.broadcasted_iota(jnp.int32, (1, LANES), 1)
    alog, dtb = aux_ref[0:1, :], aux_ref[1:2, :]

    def prep_tile(po, uo, n):
        def conv(src, cw_ref):
            cw = 0.5 * cw_ref[...]
            acc = cw[0:1, :] * src[po - 2:po - 2 + n, :]
            for tap in range(1, CONV_W):
                acc = acc + cw[tap:tap + 1, :] * src[po - 2 + tap:po - 2 + tap + n, :]
            return _silu_of_twice(acc)
        q = conv(qp_ref, cwq_ref)
        k = conv(kp_ref, cwk_ref)
        yield
        q = q * lax.rsqrt(_mm(q * q, bdm) + 1e-6) * (HD ** -0.5)
        k = k * lax.rsqrt(_mm(k * k, bdm) + 1e-6)
        qs_ref[uo:uo + n, :] = q
        ks_ref[uo:uo + n, :] = k
        vs_ref[uo:uo + n, :] = conv(vp_ref, cwv_ref)
        yield
        ba = bac_ref[uo:uo + n, :] if uo < tc else bal_ref[uo - tc:uo - tc + n, :]
        beta = _sigmoid(ba)
        gval = -jnp.exp(alog) * _softplus(ba + dtb)
        bgv = jnp.where(lane < 2 * GDN_HEADS, beta, gval)
        ex = _mm_sel_r(bgv, e_ref[...])
        incl_t = [jnp.concatenate([incl_ref[dd]] * (n // CH), axis=0) for dd in range(2)]
        yield
        for d in range(2):
            g = ex[:, (2 + d) * LANES:(3 + d) * LANES]
            gc = _mm_sel_l(trit_ref[d, 0:n, 0:n], g)
            gt = gc.T
            gr = jnp.concatenate(
                [jnp.broadcast_to(jnp.concatenate([gt[0:1, c * CH:(c + 1) * CH], gt[HD:HD + 1, c * CH:(c + 1) * CH]],
                                                  axis=1), (CH, LANES)) for c in range(n // CH)], axis=0)
            dir_ref[d, 0, uo:uo + n, :] = ex[:, d * LANES:(d + 1) * LANES]
            dir_ref[d, 1, uo:uo + n, :] = gc
            dir_ref[d, 2, uo:uo + n, :] = jnp.exp(jnp.minimum(gc - gr, 0.0)) * incl_t[d]
            yield

    tiles = _row_tiles(tc, tl)
    sched = _tile_schedule(tc, tl, grp)
    for tile, (need, _) in zip(tiles, sched):
        if need == 0:
            _round_robin([prep_tile(*tile)])
    fill = ([_chain([prep_tile(*t) for t, (need, _) in zip(tiles, sched) if need == 1])],
            [_chain([prep_tile(*t) for t, (need, _) in zip(tiles, sched) if need >= 2])])

    n_grp = n_all // grp
    chunk_rows = lambda gi, i, d: _chunk_rows(gi, grp, i, d, n_ctx, n_all)
    pd = [(i, d) for i in range(grp) for d in range(2)]

    def stage_a(gi, slot):
        for i, d in pd:
            rows = chunk_rows(gi, i, d)
            q, k = qs_ref[rows, :], ks_ref[rows, :]
            beta, dec = dir_ref[d, 0, rows, :], dir_ref[d, 2, rows, :]
            la = _mm_nt(jnp.concatenate([k, q], axis=0), _bd(k, m0, m1))
            ls_ref[slot, i, d, 0:CH, :] = la[:CH] * dec * strict_ref[d] * beta
            ls_ref[slot, i, d, CH:2 * CH, :] = la[CH:] * dec
            yield

    def stage_b(gi, slot):
        mats = [ls_ref[slot, i, d, 0:CH, :] for i, d in pd]
        es = []
        yield from _inv_levels(mats, [d for _, d in pd], lvl_ref, m0, m1, es)
        sols, kds, egs, gls = [], [], [], []
        for e, (i, d) in zip(es, pd):
            rows = chunk_rows(gi, i, d)
            k, v = ks_ref[rows, :], vs_ref[rows, :]
            beta, gc = dir_ref[d, 0, rows, :], dir_ref[d, 1, rows, :]
            eg = jnp.exp(gc)
            rhs = jnp.concatenate([v * beta, k * beta * eg], axis=1)
            sols.append(rhs + _mm(e, _bd(rhs, m0w, m1w)))
            glast = gc[CH - 1:CH, :] if d == 0 else gc[0:1, :]
            kds.append(k * jnp.exp(glast - gc))
            egs.append(eg)
            gls.append(glast)
        yield
        for sol, kd, eg, glast, (i, d) in zip(sols, kds, egs, gls, pd):
            rows = chunk_rows(gi, i, d)
            attn = ls_ref[slot, i, d, CH:2 * CH, :]
            au = _mm(attn, _bd(sol, m0w, m1w))
            kn = _mm_tn(kd, sol)
            p1_ref[slot, i, d, 0:LANES, :] = kn[:, LANES:] * bdm
            p1_ref[slot, i, d, LANES:LANES + CH, :] = qs_ref[rows, :] * eg - au[:, LANES:]
            p1_ref[slot, i, d, LANES + CH:2 * LANES + CH, :] = kn[:, :LANES] * bdm
            p1_ref[slot, i, d, 2 * LANES + CH:2 * LANES + 2 * CH, :] = au[:, :LANES]
            p1_ref[slot, i, d, 2 * LANES + 2 * CH:2 * LANES + 2 * CH + SUBLANES, :] = jnp.broadcast_to(
                jnp.exp(glast), (SUBLANES, LANES))
        yield

    def stage_c(gi, slot, st):
        for i, d in pd:
            rows = chunk_rows(gi, i, d)
            kwq = p1_ref[slot, i, d, 0:LANES + CH, :]
            nmat = p1_ref[slot, i, d, LANES + CH:2 * LANES + CH, :]
            omat = p1_ref[slot, i, d, 2 * LANES + CH:2 * LANES + 2 * CH, :]
            egl = p1_ref[slot, i, d, 2 * LANES + 2 * CH:2 * LANES + 2 * CH + 1, :]
            ks = _mm(kwq, st[d])
            oref = of_ref if d == 0 else ob_ref
            oref[rows, :] = ks[LANES:] + omat
            st[d] = st[d] * egl - ks[:LANES] + nmat
            yield

    nw = aux_ref[2:3, :]

    def out_tile(po, uo, n):
        o = of_ref[uo:uo + n, :] + ob_ref[uo:uo + n, :]
        ms = _mm(o * o, bdm) * (1.0 / HD)
        yield
        if uo < tc:
            oc_ref[uo:uo + n, :] = o * lax.rsqrt(ms + EPS) * nw * _silu(zc_ref[uo:uo + n, :])
        else:
            lo = uo - tc
            ol_ref[lo:lo + n, :] = o * lax.rsqrt(ms + EPS) * nw * _silu(zl_ref[lo:lo + n, :])
        yield

    z = jnp.zeros((LANES, LANES), F32)
    _run_pipeline(n_grp, stage_a, stage_b, stage_c, (z, z), fill=fill,
                  drain=_drain_streams(tiles, sched, n_grp, out_tile))


def _gdn_expand_consts():
    e = np.zeros((N_PAIRS, BA_W, 4 * LANES), np.float32)
    for p in range(N_PAIRS):
        for blk in range(4):
            d, is_g = blk % 2, blk // 2
            for h in range(2):
                col = is_g * 2 * GDN_HEADS + d * GDN_HEADS + 2 * p + h
                e[p, col, blk * LANES + h * HD:blk * LANES + (h + 1) * HD] = 1.0
    return e


def _gdn_mixer(qkv_c, z_c, ba_c, qkv_l, z_l, ba_l, conv_w, a_log, dt_bias, norm_w, consts):
    bsz, tc, _ = qkv_c.shape
    tl = qkv_l.shape[1]
    tall = tc + tl
    n_all = tall // CH
    lane_pad = lambda a: jnp.pad(a.reshape(-1), (2 * GDN_HEADS, LANES - 4 * GDN_HEADS))
    aux = _rows([lane_pad(a_log), lane_pad(dt_bias), jnp.tile(norm_w, 2)], SUBLANES)
    e = jnp.asarray(_gdn_expand_consts())
    cm = lambda *blk: pl.BlockSpec(blk, lambda b, p: (0,) * len(blk))

    def tok(t, col0):
        return pl.BlockSpec((None, t, LANES), lambda b, p: (b, 0, col0 + p))
    in_specs = [tok(tc, 0), tok(tc, N_PAIRS), tok(tc, 2 * N_PAIRS), tok(tc, 0),
                pl.BlockSpec((None, tc, BA_W), lambda b, p: (b, 0, 0)),
                tok(tl, 0), tok(tl, N_PAIRS), tok(tl, 2 * N_PAIRS), tok(tl, 0),
                pl.BlockSpec((None, tl, BA_W), lambda b, p: (b, 0, 0)),
                pl.BlockSpec((CONV_W, LANES), lambda b, p: (0, p)),
                pl.BlockSpec((CONV_W, LANES), lambda b, p: (0, N_PAIRS + p)),
                pl.BlockSpec((CONV_W, LANES), lambda b, p: (0, 2 * N_PAIRS + p)),
                cm(SUBLANES, LANES),
                pl.BlockSpec((None, BA_W, 4 * LANES), lambda b, p: (p, 0, 0)),
                cm(2, CH, LANES), cm(2, CH, LANES), cm(2, 6, CH, LANES), cm(2, 1, LANES), cm(LANES, LANES),
                cm(2, ROWT, ROWT)]
    pad_rows = tall + 3 * GAP
    scratch = [pltpu.VMEM((pad_rows, LANES), F32)] * 3 + [pltpu.VMEM((tall, LANES), F32)] * 3 + [
        pltpu.VMEM((2, 3, tall, LANES), F32),
        pltpu.VMEM((2, _group_size(n_all), 2, 2 * CH, LANES), F32),
        pltpu.VMEM((2, _group_size(n_all), 2, 2 * LANES + 2 * CH + SUBLANES, LANES), F32),
        pltpu.VMEM((tall, LANES), F32), pltpu.VMEM((tall, LANES), F32)]
    return pl.pallas_call(
        _gdn_kernel,
        grid=(bsz, N_PAIRS),
        in_specs=in_specs,
        out_specs=[pl.BlockSpec((None, tc, LANES), lambda b, p: (b, 0, p)),
                   pl.BlockSpec((None, tl, LANES), lambda b, p: (b, 0, p))],
        out_shape=[jax.ShapeDtypeStruct((bsz, tc, GDN_WIDTH), F32),
                   jax.ShapeDtypeStruct((bsz, tl, GDN_WIDTH), F32)],
        scratch_shapes=scratch,
        compiler_params=pltpu.CompilerParams(dimension_semantics=("arbitrary", "arbitrary"),
                                             vmem_limit_bytes=VMEM_LIMIT),
        name="gdn_mixer",
    )(qkv_c, qkv_c, qkv_c, z_c, ba_c, qkv_l, qkv_l, qkv_l, z_l, ba_l,
      conv_w, conv_w, conv_w, aux, e,
      consts["incl"], consts["strict"], consts["lvl"], consts["hm"], consts["bdm"],
      consts["tri_t"])


def _rwkv_kernel(rc_ref, kc_ref, vc_ref, wdc_ref, adc_ref, gdc_ref,
                 rl_ref, kl_ref, vl_ref, wdl_ref, adl_ref, gdl_ref,
                 mur_ref, muk_ref, muv_ref, mux_ref, wup_ref, aup_ref, gup_ref, vec_ref,
                 incl_ref, strict_ref, lvl_ref, hm_ref, bdm_ref, trit_ref,
                 oc_ref, ol_ref,
                 vs_ref, gs_ref, bvs_ref, dir_ref, ls_ref, p1_ref, yf_ref, yb_ref):
    tc, tl = rc_ref.shape[0], rl_ref.shape[0]
    tall = tc + tl
    n_ctx, n_all = tc // CH, tall // CH
    grp = _group_size(n_all)
    m0, m1 = hm_ref[0].astype(BF16), hm_ref[1].astype(BF16)
    m0w = jnp.concatenate([m0, m0], axis=1)
    m1w = jnp.concatenate([m1, m1], axis=1)
    bdm = bdm_ref[...]

    kkw, kaw, rkw = vec_ref[0:1, :], vec_ref[1:2, :], vec_ref[2:3, :]

    def prep_tile(_, uo, n):
        ctx_tile = uo < tc
        lo = uo if ctx_tile else uo - tc
        t_len = tc if ctx_tile else tl
        rowi = lax.broadcasted_iota(jnp.int32, (n, LANES), 0)

        def lerp(src_c, src_l, mu):
            src = src_c if ctx_tile else src_l
            x = src[lo:lo + n, :]
            if lo > 0:
                xm = src[lo - 1:lo - 1 + n, :]
            else:
                xm = jnp.where(rowi == 0, 0.0, pltpu.roll(x, 1, 0))
            if lo + n < t_len:
                xp = src[lo + 1:lo + 1 + n, :]
            else:
                xp = jnp.where(rowi == n - 1, 0.0, pltpu.roll(x, n - 1, 0))
            return x + mu * (0.5 * (xm + xp) - x)
        r = lerp(rc_ref, rl_ref, mur_ref[...])
        k = lerp(kc_ref, kl_ref, muk_ref[...])
        v = lerp(vc_ref, vl_ref, muv_ref[...])
        wd = lerp(wdc_ref, wdl_ref, mux_ref[0:1, :])
        ad = lerp(adc_ref, adl_ref, mux_ref[1:2, :])
        gd = lerp(gdc_ref, gdl_ref, mux_ref[2:3, :])
        tw = jnp.tanh(wd)
        kkv = k * kkw
        yield
        kk = kkv * lax.rsqrt(_mm(kkv * kkv, bdm) + 1e-6)
        ksum = jnp.zeros_like(k)
        for d in range(2):
            lw = -0.6065306597126334 * _sigmoid(vec_ref[3 + d:4 + d, :] + _mm(tw, wup_ref[d]))
            a = _sigmoid(vec_ref[5 + d:6 + d, :] + _mm(ad, aup_ref[d]))
            kdir = k * (1.0 + (a - 1.0) * kaw)
            ksum = ksum + kdir
            cum = _mm_sel_l(trit_ref[d, 0:n, 0:n], lw)
            einv = jnp.exp(-cum)
            dir_ref[d, 0, uo:uo + n, :] = cum
            dir_ref[d, 1, uo:uo + n, :] = kk * jnp.exp(cum - lw)
            dir_ref[d, 2, uo:uo + n, :] = r * jnp.exp(cum)
            dir_ref[d, 3, uo:uo + n, :] = kk * a * einv
            dir_ref[d, 4, uo:uo + n, :] = kdir * einv
            yield
        bonus = _mm_sel_r(r * ksum * rkw, bdm)
        vs_ref[uo:uo + n, :] = v
        gs_ref[uo:uo + n, :] = _mm(_sigmoid(gd), gup_ref[...])
        bvs_ref[uo:uo + n, :] = bonus * v
        yield

    tiles = _row_tiles(tc, tl)
    sched = _tile_schedule(tc, tl, grp)
    for tile, (need, _) in zip(tiles, sched):
        if need == 0:
            _round_robin([prep_tile(*tile)])
    fill = ([_chain([prep_tile(*t) for t, (need, _) in zip(tiles, sched) if need == 1])],
            [_chain([prep_tile(*t) for t, (need, _) in zip(tiles, sched) if need >= 2])])

    n_grp = n_all // grp
    pd = [(i, d) for i in range(grp) for d in range(2)]
    chunk_rows = lambda gi, i, d: _chunk_rows(gi, grp, i, d, n_ctx, n_all)

    def stage_a(gi, slot):
        def finish(pi, pdir, m, vbd):
            tri2 = jnp.concatenate([strict_ref[pdir], incl_ref[pdir]], axis=0)
            av = _mm(m[:, LANES:] * tri2, vbd)
            ab = m[:, :LANES] * tri2
            ls_ref[slot, pi, pdir, 0:CH, :] = ab[:CH]
            ls_ref[slot, pi, pdir, CH:2 * CH, :] = av[:CH]
            ls_ref[slot, pi, pdir, 2 * CH:3 * CH, :] = ab[CH:]
            ls_ref[slot, pi, pdir, 3 * CH:4 * CH, :] = av[CH:]
        pending = None
        for i, d in pd:
            rows = chunk_rows(gi, i, d)
            kkq, rq, binv, kinv = (dir_ref[d, j, rows, :] for j in range(1, 5))
            lhs = jnp.concatenate([kkq, rq], axis=0)
            rhs = jnp.concatenate([_bd(binv, m0, m1), _bd(kinv, m0, m1)], axis=0)
            cur = (i, d, _mm_nt(lhs, rhs), _bd(vs_ref[rows, :], m0, m1))
            if pending is not None:
                finish(*pending)
            pending = cur
            yield
        finish(*pending)
        yield

    def stage_b(gi, slot):
        mats = [ls_ref[slot, i, d, 0:CH, :] for i, d in pd]
        es = []
        yield from _inv_levels(mats, [d for _, d in pd], lvl_ref, m0, m1, es)
        sols = []
        for e, (i, d) in zip(es, pd):
            rows = chunk_rows(gi, i, d)
            rhs = jnp.concatenate([dir_ref[d, 1, rows, :], ls_ref[slot, i, d, CH:2 * CH, :]], axis=1)
            sols.append(rhs + _mm(e, _bd(rhs, m0w, m1w)))
        yield
        for sol, (i, d) in zip(sols, pd):
            rows = chunk_rows(gi, i, d)
            cum = dir_ref[d, 0, rows, :]
            etot = jnp.exp(cum[CH - 1:CH, :] if d == 0 else cum[0:1, :])
            bdec, kdec = dir_ref[d, 3, rows, :] * etot, dir_ref[d, 4, rows, :] * etot
            ar = _mm(ls_ref[slot, i, d, 2 * CH:3 * CH, :], _bd(sol, m0w, m1w))
            pmat = _mm_tn(sol[:, :LANES], bdec)
            nmat = _mm_tn(jnp.concatenate([-sol[:, LANES:], vs_ref[rows, :]], axis=0),
                          jnp.concatenate([bdec, kdec], axis=0))
            p1_ref[slot, i, d, 0:LANES, :] = pmat * bdm
            p1_ref[slot, i, d, LANES:2 * LANES, :] = nmat * bdm
            p1_ref[slot, i, d, 2 * LANES:2 * LANES + CH, :] = dir_ref[d, 2, rows, :] - ar[:, :LANES]
            p1_ref[slot, i, d, 2 * LANES + CH:2 * LANES + 2 * CH, :] = (
                ls_ref[slot, i, d, 3 * CH:4 * CH, :] - ar[:, LANES:])
            p1_ref[slot, i, d, 2 * LANES + 2 * CH:2 * LANES + 2 * CH + SUBLANES, :] = jnp.broadcast_to(
                etot, (SUBLANES, LANES))
        yield

    def stage_c(gi, slot, st):
        for i, d in pd:
            rows = chunk_rows(gi, i, d)
            pmat = p1_ref[slot, i, d, 0:LANES, :]
            nmat = p1_ref[slot, i, d, LANES:2 * LANES, :]
            rmat = p1_ref[slot, i, d, 2 * LANES:2 * LANES + CH, :]
            ymat = p1_ref[slot, i, d, 2 * LANES + CH:2 * LANES + 2 * CH, :]
            etot = p1_ref[slot, i, d, 2 * LANES + 2 * CH:2 * LANES + 2 * CH + 1, :]
            yref = yf_ref if d == 0 else yb_ref
            yref[rows, :] = _mm_nt(rmat, st[d]) + ymat
            st[d] = st[d] * etot - _mm(st[d], pmat) + nmat
            yield

    gnw, gnb = vec_ref[7:8, :], vec_ref[8:9, :]

    def out_tile(_, uo, n):
        y = yf_ref[uo:uo + n, :] + yb_ref[uo:uo + n, :]
        mean = _mm_sel_r(y, bdm) * (1.0 / HD)
        yield
        yc = y - mean
        var = _mm(yc * yc, bdm) * (1.0 / HD)
        yield
        out = (yc * lax.rsqrt(var + RWKV_GN_EPS) * gnw + gnb + bvs_ref[uo:uo + n, :]) * gs_ref[uo:uo + n, :]
        if uo < tc:
            oc_ref[uo:uo + n, :] = out
        else:
            ol_ref[uo - tc:uo - tc + n, :] = out
        yield

    z = jnp.zeros((LANES, LANES), F32)
    _run_pipeline(n_grp, stage_a, stage_b, stage_c, (z, z), b_before_a=False, fill=fill,
                  drain=_drain_streams(tiles, sched, n_grp, out_tile))


def _rwkv_mixer(rw_c, rw_l, mu, w0, w_up, a0, a_up, g_up, k_k, k_a, r_k, gn_w, gn_b, consts):
    bsz, tc, _ = rw_c.shape
    tl = rw_l.shape[1]
    tall = tc + tl
    n_all = tall // CH
    W = RWKV_WIDTH
    nb = W // LANES
    zr = jnp.zeros((RWKV_RANK, W), F32)
    pad_dir = lambda u: jnp.stack([jnp.concatenate([u[0], zr], axis=0), jnp.concatenate([zr, u[1]], axis=0)])
    wup, aup = pad_dir(w_up), pad_dir(a_up)
    vec = _rows([k_k, k_a, r_k.reshape(-1), w0[0], w0[1], a0[0], a0[1], gn_w, gn_b], 16)
    mu_rkv = mu[:3 * W].reshape(3, 1, W)
    mux = jnp.pad(mu[3 * W:].reshape(3, LANES), ((0, SUBLANES - 3), (0, 0)))
    cm = lambda *blk: pl.BlockSpec(blk, lambda b, p: (0,) * len(blk))

    def tok(t, col0, per_pair=True):
        if per_pair:
            return pl.BlockSpec((None, t, LANES), lambda b, p: (b, 0, col0 + p))
        return pl.BlockSpec((None, t, LANES), lambda b, p: (b, 0, col0))
    stream = lambda t: [tok(t, 0), tok(t, nb), tok(t, 2 * nb), tok(t, 3 * nb, False),
                        tok(t, 3 * nb + 1, False), tok(t, 3 * nb + 2, False)]
    in_specs = stream(tc) + stream(tl) + [
        pl.BlockSpec((None, 1, LANES), lambda b, p: (0, 0, p)),
        pl.BlockSpec((None, 1, LANES), lambda b, p: (1, 0, p)),
        pl.BlockSpec((None, 1, LANES), lambda b, p: (2, 0, p)),
        cm(SUBLANES, LANES),
        pl.BlockSpec((2, LANES, LANES), lambda b, p: (0, 0, p)),
        pl.BlockSpec((2, LANES, LANES), lambda b, p: (0, 0, p)),
        pl.BlockSpec((RWKV_G_RANK, LANES), lambda b, p: (0, p)),
        pl.BlockSpec((16, LANES), lambda b, p: (0, p)),
        cm(2, CH, LANES), cm(2, CH, LANES), cm(2, 6, CH, LANES), cm(2, 1, LANES), cm(LANES, LANES),
        cm(2, ROWT, ROWT)]
    scratch = [pltpu.VMEM((tall, LANES), F32)] * 3 + [
        pltpu.VMEM((2, 5, tall, LANES), F32),
        pltpu.VMEM((2, _group_size(n_all), 2, 4 * CH, LANES), F32),
        pltpu.VMEM((2, _group_size(n_all), 2, 2 * LANES + 2 * CH + SUBLANES, LANES), F32),
        pltpu.VMEM((tall, LANES), F32), pltpu.VMEM((tall, LANES), F32)]
    return pl.pallas_call(
        _rwkv_kernel,
        grid=(bsz, N_PAIRS),
        in_specs=in_specs,
        out_specs=[pl.BlockSpec((None, tc, LANES), lambda b, p: (b, 0, p)),
                   pl.BlockSpec((None, tl, LANES), lambda b, p: (b, 0, p))],
        out_shape=[jax.ShapeDtypeStruct((bsz, tc, W), F32),
                   jax.ShapeDtypeStruct((bsz, tl, W), F32)],
        scratch_shapes=scratch,
        compiler_params=pltpu.CompilerParams(dimension_semantics=("arbitrary", "arbitrary"),
                                             vmem_limit_bytes=VMEM_LIMIT),
        name="rwkv_mixer",
    )(*([rw_c] * 6), *([rw_l] * 6), mu_rkv, mu_rkv, mu_rkv, mux, wup, aup, g_up, vec,
      consts["incl"], consts["strict"], consts["lvl"], consts["hm"], consts["bdm"], consts["tri_t"])


def _lru_kernel(xc_ref, gc_ref, xl_ref, gl_ref, cw_ref, vec_ref, wcat_ref, bcat_ref,
                oc_ref, ol_ref, xp_ref, ab_ref, hf_ref, hb_ref):
    tc, tl = xc_ref.shape[0], xl_ref.shape[0]
    tall = tc + tl
    rows = tl // GRID_W
    W = xc_ref.shape[1]
    z = jnp.zeros((GAP, W), F32)
    xp_ref[0:GAP, :] = z
    xp_ref[GAP:GAP + tc, :] = xc_ref[...]
    xp_ref[GAP + tc:2 * GAP + tc, :] = z
    base = 2 * GAP + tc
    for c in range(GRID_W):
        xp_ref[base + c * rows:base + (c + 1) * rows, :] = xl_ref[pl.ds(c, rows, stride=GRID_W), :]
    xp_ref[base + tl:base + tl + GAP, :] = z
    cb = vec_ref[0:1, :]
    for (po, uo, n) in _row_tiles(tc, tl):
        acc = cb + cw_ref[0:1, :] * xp_ref[po - 2:po - 2 + n, :]
        for tap in range(1, CONV_W):
            acc = acc + cw_ref[tap:tap + 1, :] * xp_ref[po - 2 + tap:po - 2 + tap + n, :]
        gates = _mm(acc, wcat_ref[...]) + bcat_ref[...]
        for d in range(2):
            rg = _sigmoid(gates[:, (2 * d) * W:(2 * d + 1) * W])
            ig = _sigmoid(gates[:, (2 * d + 1) * W:(2 * d + 2) * W])
            log_a = -LRU_C * rg * _softplus(-vec_ref[1 + d:2 + d, :])
            a = jnp.exp(log_a)
            m2 = -jnp.tanh(log_a) * (1.0 + a * a)
            mult = jnp.where(m2 > 0.0, m2 * lax.rsqrt(m2), 0.0)
            ab_ref[d, 0, uo:uo + n, :] = a
            ab_ref[d, 1, uo:uo + n, :] = mult * (ig * acc)

    sub = lax.broadcasted_iota(jnp.int32, (SUBLANES, W), 0)
    n_tiles_c, n_tiles = tc // SUBLANES, tall // SUBLANES

    def tile_scan(a, b, d):
        for sh in (1, 2, 4):
            if d == 0:
                ok = sub >= sh
                a_s = jnp.where(ok, pltpu.roll(a, sh, 0), 1.0)
                b_s = jnp.where(ok, pltpu.roll(b, sh, 0), 0.0)
            else:
                ok = sub < SUBLANES - sh
                a_s = jnp.where(ok, pltpu.roll(a, SUBLANES - sh, 0), 1.0)
                b_s = jnp.where(ok, pltpu.roll(b, SUBLANES - sh, 0), 0.0)
            b = b + a * b_s
            a = a * a_s
        return a, b

    unroll = 4 if (n_tiles_c % 4 == 0 and n_tiles % 4 == 0) else 1

    def scan_body(s4, carry):
        hf, hb = carry
        tiles = []
        for j in range(unroll):
            s = s4 * unroll + j
            r0 = pl.multiple_of(s * SUBLANES, SUBLANES)
            tb = jnp.where(s < n_tiles_c, n_tiles_c - 1 - s, n_tiles + n_tiles_c - 1 - s)
            r1 = pl.multiple_of(tb * SUBLANES, SUBLANES)
            tiles.append((r0, tile_scan(ab_ref[0, 0, pl.ds(r0, SUBLANES), :], ab_ref[0, 1, pl.ds(r0, SUBLANES), :], 0),
                          r1, tile_scan(ab_ref[1, 0, pl.ds(r1, SUBLANES), :], ab_ref[1, 1, pl.ds(r1, SUBLANES), :], 1)))
        for r0, (af, bf), r1, (ab, bb) in tiles:
            h = bf + af * hf
            hf_ref[pl.ds(r0, SUBLANES), :] = h
            hf = jnp.broadcast_to(h[SUBLANES - 1:SUBLANES, :], (SUBLANES, W))
            h = bb + ab * hb
            hb_ref[pl.ds(r1, SUBLANES), :] = h
            hb = jnp.broadcast_to(h[0:1, :], (SUBLANES, W))
        return hf, hb
    z8 = jnp.zeros((SUBLANES, W), F32)
    lax.fori_loop(0, n_tiles // unroll, scan_body, (z8, z8))

    oc_ref[...] = (hf_ref[0:tc, :] + hb_ref[0:tc, :]) * _gelu_tanh(gc_ref[...])
    for c in range(GRID_W):
        h = hf_ref[tc + c * rows:tc + (c + 1) * rows, :] + hb_ref[tc + c * rows:tc + (c + 1) * rows, :]
        ol_ref[pl.ds(c, rows, stride=GRID_W), :] = h * _gelu_tanh(gl_ref[pl.ds(c, rows, stride=GRID_W), :])


def _lru_mixer(x_c, g_c, x_l, g_l, conv_w, conv_b, w_a, b_a, w_x, b_x, lam):
    bsz, tc, W = x_c.shape
    tl = x_l.shape[1]
    tall = tc + tl
    nh = W // LANES
    bph = LANES // LRU_BW
    w6 = jnp.stack([w_a, w_x], axis=1).reshape(2, 2, nh, bph, LRU_BW, LRU_BW)
    wcat = jnp.einsum("dghnrc,nm->hnrdgmc", w6, jnp.eye(bph, dtype=F32)).reshape(nh, LANES, 4 * LANES)
    bcat = jnp.stack([b_a, b_x], axis=1).reshape(2, 2, nh, LANES).transpose(2, 0, 1, 3).reshape(nh, 1, 4 * LANES)
    vec = _rows([conv_b, lam[0], lam[1]], SUBLANES)
    tok = lambda t: pl.BlockSpec((None, t, LANES), lambda b, h: (b, 0, h))
    return pl.pallas_call(
        _lru_kernel,
        grid=(bsz, nh),
        in_specs=[tok(tc), tok(tc), tok(tl), tok(tl),
                  pl.BlockSpec((CONV_W, LANES), lambda b, h: (0, h)),
                  pl.BlockSpec((SUBLANES, LANES), lambda b, h: (0, h)),
                  pl.BlockSpec((None, LANES, 4 * LANES), lambda b, h: (h, 0, 0)),
                  pl.BlockSpec((None, 1, 4 * LANES), lambda b, h: (h, 0, 0))],
        out_specs=[tok(tc), tok(tl)],
        out_shape=[jax.ShapeDtypeStruct((bsz, tc, W), F32), jax.ShapeDtypeStruct((bsz, tl, W), F32)],
        scratch_shapes=[pltpu.VMEM((tall + 3 * GAP, LANES), F32),
                        pltpu.VMEM((2, 2, tall, LANES), F32),
                        pltpu.VMEM((tall, LANES), F32), pltpu.VMEM((tall, LANES), F32)],
        compiler_params=pltpu.CompilerParams(dimension_semantics=("arbitrary", "arbitrary"),
                                             vmem_limit_bytes=VMEM_LIMIT),
        name="lru_mixer",
    )(x_c, g_c, x_l, g_l, conv_w, vec, wcat.astype(BF16), bcat)


def _finish_kernel(x_ref, gdn_ref, lru_ref, rwk_ref, m2_ref, m3_ref, m4_ref, m5_ref, nrm_ref,
                   wo_ref, up_ref, dn_ref, o_ref):
    x = x_ref[...]
    o = (jnp.dot(gdn_ref[...].astype(BF16), wo_ref[0:GDN_WIDTH, :], preferred_element_type=F32)
         + jnp.dot(lru_ref[...].astype(BF16), wo_ref[GDN_WIDTH:GDN_WIDTH + LRU_WIDTH, :],
                   preferred_element_type=F32)
         + jnp.dot(rwk_ref[...].astype(BF16), wo_ref[GDN_WIDTH + LRU_WIDTH:, :], preferred_element_type=F32))
    x = x + m2_ref[...] * _rms(o, nrm_ref[0:1, :])
    h = (_rms(x, nrm_ref[1:2, :]) * (1.0 + m4_ref[...]) + m3_ref[...]).astype(BF16)
    f = jnp.zeros_like(x)
    fc = 1024
    for j in range(D_FF // fc):
        a = jnp.maximum(jnp.dot(h, up_ref[:, j * fc:(j + 1) * fc], preferred_element_type=F32), 0.0)
        f = f + jnp.dot((a * a).astype(BF16), dn_ref[j * fc:(j + 1) * fc, :], preferred_element_type=F32)
    o_ref[...] = x + m5_ref[...] * _rms(f, nrm_ref[2:3, :])


def _finish(x2, gdn, lru, rwk, mod_rows, rows_per_mod, norms, wo_bf, up_bf, dn_bf, layer):
    n = x2.shape[0]
    tm = DENSE_TM
    tiles_per_mod = rows_per_mod // tm
    modspec = lambda k: pl.BlockSpec((None, 1, D_MODEL), lambda i: (6 * (i // tiles_per_mod) + k, 0, 0))
    cm = lambda *blk: pl.BlockSpec((None,) + blk, lambda i: (layer,) + (0,) * len(blk),
                                   pipeline_mode=pl.Buffered(1))
    return pl.pallas_call(
        _finish_kernel,
        grid=(n // tm,),
        in_specs=[pl.BlockSpec((tm, D_MODEL), lambda i: (i, 0)),
                  pl.BlockSpec((tm, GDN_WIDTH), lambda i: (i, 0)),
                  pl.BlockSpec((tm, LRU_WIDTH), lambda i: (i, 0)),
                  pl.BlockSpec((tm, RWKV_WIDTH), lambda i: (i, 0)),
                  modspec(2), modspec(3), modspec(4), modspec(5),
                  cm(SUBLANES, D_MODEL), cm(D_MODEL, D_MODEL), cm(D_MODEL, D_FF), cm(D_FF, D_MODEL)],
        out_specs=pl.BlockSpec((tm, D_MODEL), lambda i: (i, 0)),
        out_shape=jax.ShapeDtypeStruct((n, D_MODEL), F32),
        compiler_params=pltpu.CompilerParams(dimension_semantics=("arbitrary",),
                                             vmem_limit_bytes=VMEM_LIMIT),
        name="finish",
    )(x2, gdn, lru, rwk, mod_rows, mod_rows, mod_rows, mod_rows, norms, wo_bf, up_bf, dn_bf)


def kernel(x, c, ctx, c_ctx, ada_w, ada_b, norm_mix_pre, norm_mix_post, norm_ffn_pre, norm_ffn_post, w_in, gdn_conv, gdn_a_log, gdn_dt_bias, gdn_norm, lru_conv, lru_conv_b, lru_wa, lru_ba, lru_wx, lru_bx, lru_lambda, rwkv_mu, rwkv_w0, rwkv_w_up, rwkv_a0, rwkv_a_up, rwkv_g_up, rwkv_k_k, rwkv_k_a, rwkv_r_k, rwkv_gn_w, rwkv_gn_b, w_out, ffn_up, ffn_down):
    bsz, tl, _ = x.shape
    tc = ctx.shape[1]
    depth = w_in.shape[0]
    consts = _device_consts()

    cvec = jnp.pad(jnp.concatenate([c, c_ctx[None, :]], axis=0), ((0, 16 - bsz - 1), (0, 0)))
    mods = _ada_mod(cvec, ada_w, ada_b).reshape(depth, 16, 6, D_MODEL)

    w_in_bf = w_in.astype(BF16)
    wo_bf, up_bf, dn_bf = w_out.astype(BF16), ffn_up.astype(BF16), ffn_down.astype(BF16)
    g_pre = norm_mix_pre.reshape(depth, 1, D_MODEL)
    norms = jnp.pad(jnp.stack([norm_mix_post, norm_ffn_pre, norm_ffn_post], axis=1),
                    ((0, 0), (0, SUBLANES - 3), (0, 0)))

    xl = x.reshape(bsz * tl, D_MODEL)
    xc = ctx.reshape(bsz * tc, D_MODEL)
    for i in range(depth):
        mod_l = mods[i, 0:bsz].reshape(bsz * 6, 1, D_MODEL)
        mod_c = mods[i, bsz:bsz + 1].reshape(6, 1, D_MODEL)
        pl_ = _inproj(xl, mod_l, tl, g_pre, w_in_bf, i)
        pc_ = _inproj(xc, mod_c, bsz * tc, g_pre, w_in_bf, i)
        r3 = lambda a, t: a.reshape(bsz, t, a.shape[-1])
        qkv_l, z_l, ba_l, lx_l, lg_l, rw_l = (r3(a, tl) for a in pl_)
        qkv_c, z_c, ba_c, lx_c, lg_c, rw_c = (r3(a, tc) for a in pc_)

        gdn_c, gdn_l = _gdn_mixer(qkv_c, z_c, ba_c, qkv_l, z_l, ba_l, gdn_conv[i], gdn_a_log[i],
                                  gdn_dt_bias[i], gdn_norm[i], consts)
        lru_c, lru_l = _lru_mixer(lx_c, lg_c, lx_l, lg_l, lru_conv[i], lru_conv_b[i], lru_wa[i], lru_ba[i],
                                  lru_wx[i], lru_bx[i], lru_lambda[i])
        rwk_c, rwk_l = _rwkv_mixer(rw_c, rw_l, rwkv_mu[i], rwkv_w0[i], rwkv_w_up[i], rwkv_a0[i],
                                   rwkv_a_up[i], rwkv_g_up[i], rwkv_k_k[i], rwkv_k_a[i], rwkv_r_k[i],
                                   rwkv_gn_w[i], rwkv_gn_b[i], consts)

        f2 = lambda a: a.reshape(-1, a.shape[-1])
        xl = _finish(xl, f2(gdn_l), f2(lru_l), f2(rwk_l), mod_l, tl, norms, wo_bf, up_bf, dn_bf, i)
        if i < depth - 1:
            xc = _finish(xc, f2(gdn_c), f2(lru_c), f2(rwk_c), mod_c, bsz * tc, norms, wo_bf, up_bf, dn_bf, i)
    return xl.reshape(bsz, tl, D_MODEL)
```

```python
import functools

import numpy as np
import jax
import jax.numpy as jnp
from jax import lax
from jax.experimental import pallas as pl
from jax.experimental.pallas import tpu as pltpu

F32 = jnp.float32
BF16 = jnp.bfloat16

LANES = 128
SUBLANES = 8
VMEM_LIMIT = 56 * 1024 * 1024

D_MODEL = 1024
DEPTH = 2
GRID_W = 64
CONV_W = 4
EPS = 1e-6
D_FF = 4 * D_MODEL
HD = 64
GDN_WIDTH = 3 * D_MODEL // 8
GDN_HEADS = GDN_WIDTH // HD
LRU_WIDTH = D_MODEL // 4
LRU_BLOCKS = 4
LRU_BW = LRU_WIDTH // LRU_BLOCKS
LRU_C = 8.0
RWKV_WIDTH = D_MODEL - GDN_WIDTH - LRU_WIDTH
RWKV_HEADS = RWKV_WIDTH // HD
RWKV_RANK = 64
RWKV_G_RANK = 128
RWKV_GN_EPS = 6.4e-4
RWKV_IN = 3 * RWKV_WIDTH + 2 * RWKV_RANK + 2 * RWKV_RANK + RWKV_G_RANK
N_PAIRS = GDN_HEADS // 2
BA_W = LANES
_W_OFF = np.cumsum([0, 3 * GDN_WIDTH, GDN_WIDTH, 2 * GDN_HEADS, 2 * GDN_HEADS, LRU_WIDTH, LRU_WIDTH, RWKV_IN])
D_INP = int(_W_OFF[-1])
P_SLICES = [(0, 3 * GDN_WIDTH), (int(_W_OFF[1]), GDN_WIDTH), (int(_W_OFF[2]), BA_W),
            (int(_W_OFF[4]), LRU_WIDTH), (int(_W_OFF[5]), LRU_WIDTH), (int(_W_OFF[6]), RWKV_IN)]

CH = 64
ROWT = 256
DENSE_TM = 512
GAP = SUBLANES


def _mm(a, b):
    return jnp.dot(a.astype(BF16), b.astype(BF16), preferred_element_type=F32)


def _mm_nt(a, b):
    return lax.dot_general(a.astype(BF16), b.astype(BF16), (((1,), (1,)), ((), ())),
                           preferred_element_type=F32)


def _mm_tn(a, b):
    return lax.dot_general(a.astype(BF16), b.astype(BF16), (((0,), (0,)), ((), ())),
                           preferred_element_type=F32)


def _split2(x):
    hi = x.astype(BF16)
    lo = (x - hi.astype(F32)).astype(BF16)
    return hi, lo


def _mm_sel_l(m01, x):
    mb = m01.astype(BF16)
    h, l = _split2(x)
    d = functools.partial(jnp.dot, preferred_element_type=F32)
    return d(mb, h) + d(mb, l)


def _mm_sel_r(x, m01):
    mb = m01.astype(BF16)
    h, l = _split2(x)
    d = functools.partial(jnp.dot, preferred_element_type=F32)
    return d(h, mb) + d(l, mb)


def _sigmoid(x):
    return 0.5 * jnp.tanh(0.5 * x) + 0.5


def _silu_of_twice(h):
    return h + h * jnp.tanh(h)


def _silu(x):
    return _silu_of_twice(0.5 * x)


def _softplus(x):
    return jnp.maximum(x, 0.0) + jnp.log(1.0 + jnp.exp(-jnp.abs(x)))


def _gelu_tanh(x):
    return 0.5 * x * (1.0 + jnp.tanh(0.7978845608028654 * (x + 0.044715 * (x * x * x))))


def _bd(y, m0, m1):
    yb = y.astype(BF16)
    return jnp.concatenate([yb * m0, yb * m1], axis=0)


def _pair_consts():
    i = np.arange(CH)[:, None]
    j = (np.arange(LANES) % HD)[None, :]
    incl = np.stack([i >= j, i <= j]).astype(np.float32)
    strict = np.stack([i > j, i < j]).astype(np.float32)
    lvls = []
    for d in range(2):
        per = []
        for m in (1, 2, 4, 8, 16, 32):
            same = (i // (2 * m)) == (j // (2 * m))
            lo_i, lo_j = (i % (2 * m)) < m, (j % (2 * m)) < m
            off = same & (~lo_i) & lo_j if d == 0 else same & lo_i & (~lo_j)
            per.append(off)
        lvls.append(np.stack(per))
    lvl = np.stack(lvls).astype(np.float32)
    lane = np.arange(LANES)
    hm = np.stack([lane < HD, lane >= HD]).astype(np.float32)[:, None, :]
    bdm = ((np.arange(LANES)[:, None] // HD) == (lane[None, :] // HD)).astype(np.float32)
    r = np.arange(ROWT)[:, None]
    c = np.arange(ROWT)[None, :]
    same = (r // CH) == (c // CH)
    tri_t = np.stack([same & (r >= c), same & (r <= c)]).astype(np.float32)
    return dict(incl=incl, strict=strict, lvl=lvl, hm=hm, bdm=bdm, tri_t=tri_t)


def _device_consts():
    out = {k: jnp.asarray(v) for k, v in _pair_consts().items()}
    out["tri_t"] = out["tri_t"].astype(BF16)
    return out


def _inv_levels(mats, dirs, lvl_ref, m0, m1, out):
    es = [-(a * lvl_ref[d, 0]) for a, d in zip(mats, dirs)]
    for k in range(1, 6):
        offs = [a * lvl_ref[d, k] for a, d in zip(mats, dirs)]
        xs = [off + p for off, p in zip(offs, _mm_pairs(es, [_bd(off, m0, m1) for off in offs]))]
        yield
        es = [e - x - p for e, x, p in zip(es, xs, _mm_pairs(xs, [_bd(e, m0, m1) for e in es]))]
        yield
    out.extend(es)


def _mm_pairs(lhs, rhs):
    out = []
    for j in range(0, len(lhs) - 1, 2):
        r = jnp.dot(jnp.concatenate([lhs[j], lhs[j + 1]], axis=0).astype(BF16),
                    jnp.concatenate([rhs[j], rhs[j + 1]], axis=1), preferred_element_type=F32)
        out += [r[:CH, :LANES], r[CH:, LANES:]]
    if len(lhs) % 2:
        out.append(_mm(lhs[-1], rhs[-1]))
    return out


def _rows(vectors, n_rows):
    m = jnp.stack([v.astype(F32) for v in vectors])
    return jnp.pad(m, ((0, n_rows - m.shape[0]), (0, 0)))


def _round_robin(gens):
    gens = list(gens)
    while gens:
        alive = []
        for g in gens:
            try:
                next(g)
                alive.append(g)
            except StopIteration:
                pass
        gens = alive


def _group_size(n_all):
    for g in (6, 4, 3, 2):
        if n_all % g == 0:
            return g
    return 1


def _chunk_order(s, n_ctx, n_all, d):
    if d == 0:
        return s
    if isinstance(s, int):
        return n_ctx - 1 - s if s < n_ctx else n_all + n_ctx - 1 - s
    return jnp.where(s < n_ctx, n_ctx - 1 - s, n_all + n_ctx - 1 - s)


def _chunk_rows(gi, grp, i, d, n_ctx, n_all):
    c = _chunk_order(gi * grp + i, n_ctx, n_all, d)
    return pl.ds(c * CH if isinstance(c, int) else pl.multiple_of(c * CH, CH), CH)


def _run_pipeline(n_grp, stage_a, stage_b, stage_c, st, b_before_a=True, fill=((), ()), drain=((), ())):
    def step(t, st, do_a, do_b, do_c, extra=()):
        par = t % 2 if isinstance(t, int) else lax.rem(t, 2)
        st = list(st)
        gens = []
        if do_c:
            gens.append(stage_c(t - 2, par, st))
        ab = ([stage_b(t - 1, 1 - par)] if do_b else []) + ([stage_a(t, par)] if do_a else [])
        _round_robin(gens + (ab if b_before_a else ab[::-1]) + list(extra))
        return tuple(st)
    assert n_grp >= 2
    st = step(0, st, True, False, False, fill[0])
    st = step(1, st, True, True, False, fill[1])
    st = lax.fori_loop(2, n_grp, lambda t, s: step(t, s, True, True, True), st)
    st = step(n_grp, st, False, True, True, drain[0])
    return step(n_grp + 1, st, False, False, True, drain[1])


def _tile_schedule(tc, tl, grp):
    n_ctx, n_all = tc // CH, (tc + tl) // CH
    first, last = {}, {}
    for s in range(n_all):
        g = s // grp
        for d in range(2):
            c = _chunk_order(s, n_ctx, n_all, d)
            first[c] = min(first.get(c, g), g)
            last[c] = max(last.get(c, (0, 0)), (g + 2, 2 * (s % grp) + d))
    out = []
    for (_, uo, n) in _row_tiles(tc, tl):
        cs = range(uo // CH, (uo + n) // CH)
        out.append((min(first[c] for c in cs), max(last[c] for c in cs)))
    return out


def _chain(gens):
    for g in gens:
        yield from g


def _delayed(gen, rounds):
    for _ in range(rounds):
        yield
    yield from gen


def _drain_streams(tiles, sched, n_grp, out_tile):
    drain = ([], [])
    for t, (_, (step, sub)) in zip(tiles, sched):
        if step < n_grp:
            drain[0].append(out_tile(*t))
        else:
            drain[step - n_grp].append(_delayed(out_tile(*t), sub + 1))
    return drain


def _fill_padded(dst_ref, src_c_ref, src_l_ref, tc, tl):
    w = dst_ref.shape[1]
    z = jnp.zeros((GAP, w), F32)
    dst_ref[0:GAP, :] = z
    dst_ref[GAP:GAP + tc, :] = src_c_ref[...]
    dst_ref[GAP + tc:2 * GAP + tc, :] = z
    dst_ref[2 * GAP + tc:2 * GAP + tc + tl, :] = src_l_ref[...]
    dst_ref[2 * GAP + tc + tl:3 * GAP + tc + tl, :] = z


def _row_tiles(tc, tl):
    out = []
    for base_p, base_u, n in ((GAP, 0, tc), (2 * GAP + tc, tc, tl)):
        for t0 in range(0, n, ROWT):
            out.append((base_p + t0, base_u + t0, min(ROWT, n - t0)))
    return out


def _ada_kernel(c_ref, w_ref, b_ref, o_ref):
    c = c_ref[...]
    o_ref[...] = _mm(_silu(c), w_ref[...]) + b_ref[...]


def _ada_mod(cvec, ada_w, ada_b):
    L = ada_w.shape[0]
    n = ada_w.shape[2]
    tn = 1536
    return pl.pallas_call(
        _ada_kernel,
        grid=(L, n // tn),
        in_specs=[pl.BlockSpec((16, D_MODEL), lambda l, j: (0, 0)),
                  pl.BlockSpec((None, D_MODEL, tn), lambda l, j: (l, 0, j)),
                  pl.BlockSpec((None, 1, tn), lambda l, j: (l, 0, j))],
        out_specs=pl.BlockSpec((None, 16, tn), lambda l, j: (l, 0, j)),
        out_shape=jax.ShapeDtypeStruct((L, 16, n), F32),
        compiler_params=pltpu.CompilerParams(dimension_semantics=("arbitrary", "arbitrary"),
                                             vmem_limit_bytes=VMEM_LIMIT),
        name="ada_mod",
    )(cvec, ada_w, ada_b.reshape(L, 1, n))


def _rms(x, g):
    return x * lax.rsqrt(jnp.mean(x * x, axis=-1, keepdims=True) + EPS) * g


def _inproj_kernel(x_ref, sh_ref, sc_ref, g_ref, w_ref, qkv_ref, z_ref, ba_ref, lx_ref, lg_ref, rw_ref):
    h = _rms(x_ref[...], g_ref[...]) * (1.0 + sc_ref[...]) + sh_ref[...]
    p = jnp.dot(h.astype(BF16), w_ref[...], preferred_element_type=F32)
    for ref, (start, width) in zip((qkv_ref, z_ref, ba_ref, lx_ref, lg_ref, rw_ref), P_SLICES):
        ref[...] = p[:, start:start + width]


def _inproj(x2, mod_rows, rows_per_mod, g_all, w_all, layer):
    n = x2.shape[0]
    tm = DENSE_TM
    tiles_per_mod = rows_per_mod // tm
    widths = [w for _, w in P_SLICES]
    return pl.pallas_call(
        _inproj_kernel,
        grid=(n // tm,),
        in_specs=[pl.BlockSpec((tm, D_MODEL), lambda i: (i, 0)),
                  pl.BlockSpec((None, 1, D_MODEL), lambda i: (6 * (i // tiles_per_mod), 0, 0)),
                  pl.BlockSpec((None, 1, D_MODEL), lambda i: (6 * (i // tiles_per_mod) + 1, 0, 0)),
                  pl.BlockSpec((None, 1, D_MODEL), lambda i: (layer, 0, 0)),
                  pl.BlockSpec((None, D_MODEL, D_INP), lambda i: (layer, 0, 0), pipeline_mode=pl.Buffered(1))],
        out_specs=[pl.BlockSpec((tm, w), lambda i: (i, 0)) for w in widths],
        out_shape=[jax.ShapeDtypeStruct((n, w), F32) for w in widths],
        compiler_params=pltpu.CompilerParams(dimension_semantics=("arbitrary",),
                                             vmem_limit_bytes=VMEM_LIMIT),
        name="inproj",
    )(x2, mod_rows, mod_rows, g_all, w_all)


def _gdn_kernel(qc_ref, kc_ref, vc_ref, zc_ref, bac_ref, ql_ref, kl_ref, vl_ref, zl_ref, bal_ref,
                cwq_ref, cwk_ref, cwv_ref, aux_ref, e_ref,
                incl_ref, strict_ref, lvl_ref, hm_ref, bdm_ref, trit_ref,
                oc_ref, ol_ref,
                qp_ref, kp_ref, vp_ref, qs_ref, ks_ref, vs_ref, dir_ref, ls_ref, p1_ref, of_ref, ob_ref):
    tc, tl = qc_ref.shape[0], ql_ref.shape[0]
    tall = tc + tl
    n_ctx, n_all = tc // CH, tall // CH
    grp = _group_size(n_all)
    m0, m1 = hm_ref[0].astype(BF16), hm_ref[1].astype(BF16)
    m0w = jnp.concatenate([m0, m0], axis=1)
    m1w = jnp.concatenate([m1, m1], axis=1)
    bdm = bdm_ref[...]

    _fill_padded(qp_ref, qc_ref, ql_ref, tc, tl)
    _fill_padded(kp_ref, kc_ref, kl_ref, tc, tl)
    _fill_padded(vp_ref, vc_ref, vl_ref, tc, tl)
    lane = lax.broadcasted_iota(jnp.int32, (1, LANES), 1)
    alog, dtb = aux_ref[0:1, :], aux_ref[1:2, :]

    def prep_tile(po, uo, n):
        def conv(src, cw_ref):
            cw = 0.5 * cw_ref[...]
            acc = cw[0:1, :] * src[po - 2:po - 2 + n, :]
            for tap in range(1, CONV_W):
                acc = acc + cw[tap:tap + 1, :] * src[po - 2 + tap:po - 2 + tap + n, :]
            return _silu_of_twice(acc)
        q = conv(qp_ref, cwq_ref)
        k = conv(kp_ref, cwk_ref)
        yield
        q = q * lax.rsqrt(_mm(q * q, bdm) + 1e-6) * (HD ** -0.5)
        k = k * lax.rsqrt(_mm(k * k, bdm) + 1e-6)
        qs_ref[uo:uo + n, :] = q
        ks_ref[uo:uo + n, :] = k
        vs_ref[uo:uo + n, :] = conv(vp_ref, cwv_ref)
        yield
        ba = bac_ref[uo:uo + n, :] if uo < tc else bal_ref[uo - tc:uo - tc + n, :]
        beta = _sigmoid(ba)
        gval = -jnp.exp(alog) * _softplus(ba + dtb)
        bgv = jnp.where(lane < 2 * GDN_HEADS, beta, gval)
        ex = _mm_sel_r(bgv, e_ref[...])
        incl_t = [jnp.concatenate([incl_ref[dd]] * (n // CH), axis=0) for dd in range(2)]
        yield
        for d in range(2):
            g = ex[:, (2 + d) * LANES:(3 + d) * LANES]
            gc = _mm_sel_l(trit_ref[d, 0:n, 0:n], g)
            gt = gc.T
            gr = jnp.concatenate(
                [jnp.broadcast_to(jnp.concatenate([gt[0:1, c * CH:(c + 1) * CH], gt[HD:HD + 1, c * CH:(c + 1) * CH]],
                                                  axis=1), (CH, LANES)) for c in range(n // CH)], axis=0)
            dir_ref[d, 0, uo:uo + n, :] = ex[:, d * LANES:(d + 1) * LANES]
            dir_ref[d, 1, uo:uo + n, :] = gc
            dir_ref[d, 2, uo:uo + n, :] = jnp.exp(jnp.minimum(gc - gr, 0.0)) * incl_t[d]
            yield

    tiles = _row_tiles(tc, tl)
    sched = _tile_schedule(tc, tl, grp)
    _round_robin([prep_tile(*t) for t, (need, _) in zip(tiles, sched) if need == 0])
    fill = ([prep_tile(*t) for t, (need, _) in zip(tiles, sched) if need == 1],
            [prep_tile(*t) for t, (need, _) in zip(tiles, sched) if need >= 2])

    n_grp = n_all // grp
    chunk_rows = lambda gi, i, d: _chunk_rows(gi, grp, i, d, n_ctx, n_all)
    pd = [(i, d) for i in range(grp) for d in range(2)]

    def stage_a(gi, slot):
        for i, d in pd:
            rows = chunk_rows(gi, i, d)
            q, k = qs_ref[rows, :], ks_ref[rows, :]
            beta, dec = dir_ref[d, 0, rows, :], dir_ref[d, 2, rows, :]
            la = _mm_nt(jnp.concatenate([k, q], axis=0), _bd(k, m0, m1))
            ls_ref[slot, i, d, 0:CH, :] = la[:CH] * dec * strict_ref[d] * beta
            ls_ref[slot, i, d, CH:2 * CH, :] = la[CH:] * dec
            yield

    def stage_b(gi, slot):
        mats = [ls_ref[slot, i, d, 0:CH, :] for i, d in pd]
        es = []
        yield from _inv_levels(mats, [d for _, d in pd], lvl_ref, m0, m1, es)
        sols, kds, egs, gls = [], [], [], []
        for e, (i, d) in zip(es, pd):
            rows = chunk_rows(gi, i, d)
            k, v = ks_ref[rows, :], vs_ref[rows, :]
            beta, gc = dir_ref[d, 0, rows, :], dir_ref[d, 1, rows, :]
            eg = jnp.exp(gc)
            rhs = jnp.concatenate([v * beta, k * beta * eg], axis=1)
            sols.append(rhs + _mm(e, _bd(rhs, m0w, m1w)))
            glast = gc[CH - 1:CH, :] if d == 0 else gc[0:1, :]
            kds.append(k * jnp.exp(glast - gc))
            egs.append(eg)
            gls.append(glast)
        yield
        for sol, kd, eg, glast, (i, d) in zip(sols, kds, egs, gls, pd):
            rows = chunk_rows(gi, i, d)
            attn = ls_ref[slot, i, d, CH:2 * CH, :]
            au = _mm(attn, _bd(sol, m0w, m1w))
            kn = _mm_tn(kd, sol)
            p1_ref[slot, i, d, 0:LANES, :] = kn[:, LANES:] * bdm
            p1_ref[slot, i, d, LANES:LANES + CH, :] = qs_ref[rows, :] * eg - au[:, LANES:]
            p1_ref[slot, i, d, LANES + CH:2 * LANES + CH, :] = kn[:, :LANES] * bdm
            p1_ref[slot, i, d, 2 * LANES + CH:2 * LANES + 2 * CH, :] = au[:, :LANES]
            p1_ref[slot, i, d, 2 * LANES + 2 * CH:2 * LANES + 2 * CH + SUBLANES, :] = jnp.broadcast_to(
                jnp.exp(glast), (SUBLANES, LANES))
        yield

    def stage_c(gi, slot, st):
        for i, d in pd:
            rows = chunk_rows(gi, i, d)
            kwq = p1_ref[slot, i, d, 0:LANES + CH, :]
            nmat = p1_ref[slot, i, d, LANES + CH:2 * LANES + CH, :]
            omat = p1_ref[slot, i, d, 2 * LANES + CH:2 * LANES + 2 * CH, :]
            egl = p1_ref[slot, i, d, 2 * LANES + 2 * CH:2 * LANES + 2 * CH + 1, :]
            ks = _mm(kwq, st[d])
            oref = of_ref if d == 0 else ob_ref
            oref[rows, :] = ks[LANES:] + omat
            st[d] = st[d] * egl - ks[:LANES] + nmat
            yield

    nw = aux_ref[2:3, :]

    def out_tile(po, uo, n):
        o = of_ref[uo:uo + n, :] + ob_ref[uo:uo + n, :]
        ms = _mm(o * o, bdm) * (1.0 / HD)
        yield
        if uo < tc:
            oc_ref[uo:uo + n, :] = o * lax.rsqrt(ms + EPS) * nw * _silu(zc_ref[uo:uo + n, :])
        else:
            lo = uo - tc
            ol_ref[lo:lo + n, :] = o * lax.rsqrt(ms + EPS) * nw * _silu(zl_ref[lo:lo + n, :])
        yield

    z = jnp.zeros((LANES, LANES), F32)
    _run_pipeline(n_grp, stage_a, stage_b, stage_c, (z, z), fill=fill,
                  drain=_drain_streams(tiles, sched, n_grp, out_tile))


def _gdn_expand_consts():
    e = np.zeros((N_PAIRS, BA_W, 4 * LANES), np.float32)
    for p in range(N_PAIRS):
        for blk in range(4):
            d, is_g = blk % 2, blk // 2
            for h in range(2):
                col = is_g * 2 * GDN_HEADS + d * GDN_HEADS + 2 * p + h
                e[p, col, blk * LANES + h * HD:blk * LANES + (h + 1) * HD] = 1.0
    return e


def _gdn_mixer(qkv_c, z_c, ba_c, qkv_l, z_l, ba_l, conv_w, a_log, dt_bias, norm_w, consts):
    bsz, tc, _ = qkv_c.shape
    tl = qkv_l.shape[1]
    tall = tc + tl
    n_all = tall // CH
    lane_pad = lambda a: jnp.pad(a.reshape(-1), (2 * GDN_HEADS, LANES - 4 * GDN_HEADS))
    aux = _rows([lane_pad(a_log), lane_pad(dt_bias), jnp.tile(norm_w, 2)], SUBLANES)
    e = jnp.asarray(_gdn_expand_consts())
    cm = lambda *blk: pl.BlockSpec(blk, lambda b, p: (0,) * len(blk))

    def tok(t, col0):
        return pl.BlockSpec((None, t, LANES), lambda b, p: (b, 0, col0 + p))
    in_specs = [tok(tc, 0), tok(tc, N_PAIRS), tok(tc, 2 * N_PAIRS), tok(tc, 0),
                pl.BlockSpec((None, tc, BA_W), lambda b, p: (b, 0, 0)),
                tok(tl, 0), tok(tl, N_PAIRS), tok(tl, 2 * N_PAIRS), tok(tl, 0),
                pl.BlockSpec((None, tl, BA_W), lambda b, p: (b, 0, 0)),
                pl.BlockSpec((CONV_W, LANES), lambda b, p: (0, p)),
                pl.BlockSpec((CONV_W, LANES), lambda b, p: (0, N_PAIRS + p)),
                pl.BlockSpec((CONV_W, LANES), lambda b, p: (0, 2 * N_PAIRS + p)),
                cm(SUBLANES, LANES),
                pl.BlockSpec((None, BA_W, 4 * LANES), lambda b, p: (p, 0, 0)),
                cm(2, CH, LANES), cm(2, CH, LANES), cm(2, 6, CH, LANES), cm(2, 1, LANES), cm(LANES, LANES),
                cm(2, ROWT, ROWT)]
    pad_rows = tall + 3 * GAP
    scratch = [pltpu.VMEM((pad_rows, LANES), F32)] * 3 + [pltpu.VMEM((tall, LANES), F32)] * 3 + [
        pltpu.VMEM((2, 3, tall, LANES), F32),
        pltpu.VMEM((2, _group_size(n_all), 2, 2 * CH, LANES), F32),
        pltpu.VMEM((2, _group_size(n_all), 2, 2 * LANES + 2 * CH + SUBLANES, LANES), F32),
        pltpu.VMEM((tall, LANES), F32), pltpu.VMEM((tall, LANES), F32)]
    return pl.pallas_call(
        _gdn_kernel,
        grid=(bsz, N_PAIRS),
        in_specs=in_specs,
        out_specs=[pl.BlockSpec((None, tc, LANES), lambda b, p: (b, 0, p)),
                   pl.BlockSpec((None, tl, LANES), lambda b, p: (b, 0, p))],
        out_shape=[jax.ShapeDtypeStruct((bsz, tc, GDN_WIDTH), F32),
                   jax.ShapeDtypeStruct((bsz, tl, GDN_WIDTH), F32)],
        scratch_shapes=scratch,
        compiler_params=pltpu.CompilerParams(dimension_semantics=("arbitrary", "arbitrary"),
                                             vmem_limit_bytes=VMEM_LIMIT),
        name="gdn_mixer",
    )(qkv_c, qkv_c, qkv_c, z_c, ba_c, qkv_l, qkv_l, qkv_l, z_l, ba_l,
      conv_w, conv_w, conv_w, aux, e,
      consts["incl"], consts["strict"], consts["lvl"], consts["hm"], consts["bdm"],
      consts["tri_t"])


def _rwkv_kernel(rc_ref, kc_ref, vc_ref, wdc_ref, adc_ref, gdc_ref,
                 rl_ref, kl_ref, vl_ref, wdl_ref, adl_ref, gdl_ref,
                 mur_ref, muk_ref, muv_ref, mux_ref, wup_ref, aup_ref, gup_ref, vec_ref,
                 incl_ref, strict_ref, lvl_ref, hm_ref, bdm_ref, trit_ref,
                 oc_ref, ol_ref,
                 vs_ref, gs_ref, bvs_ref, dir_ref, ls_ref, p1_ref, yf_ref, yb_ref):
    tc, tl = rc_ref.shape[0], rl_ref.shape[0]
    tall = tc + tl
    n_ctx, n_all = tc // CH, tall // CH
    grp = _group_size(n_all)
    m0, m1 = hm_ref[0].astype(BF16), hm_ref[1].astype(BF16)
    m0w = jnp.concatenate([m0, m0], axis=1)
    m1w = jnp.concatenate([m1, m1], axis=1)
    bdm = bdm_ref[...]

    kkw, kaw, rkw = vec_ref[0:1, :], vec_ref[1:2, :], vec_ref[2:3, :]

    def prep_tile(_, uo, n):
        ctx_tile = uo < tc
        lo = uo if ctx_tile else uo - tc
        t_len = tc if ctx_tile else tl
        rowi = lax.broadcasted_iota(jnp.int32, (n, LANES), 0)

        def lerp(src_c, src_l, mu):
            src = src_c if ctx_tile else src_l
            x = src[lo:lo + n, :]
            if lo > 0:
                xm = src[lo - 1:lo - 1 + n, :]
            else:
                xm = jnp.where(rowi == 0, 0.0, pltpu.roll(x, 1, 0))
            if lo + n < t_len:
                xp = src[lo + 1:lo + 1 + n, :]
            else:
                xp = jnp.where(rowi == n - 1, 0.0, pltpu.roll(x, n - 1, 0))
            return x + mu * (0.5 * (xm + xp) - x)
        r = lerp(rc_ref, rl_ref, mur_ref[...])
        k = lerp(kc_ref, kl_ref, muk_ref[...])
        v = lerp(vc_ref, vl_ref, muv_ref[...])
        wd = lerp(wdc_ref, wdl_ref, mux_ref[0:1, :])
        ad = lerp(adc_ref, adl_ref, mux_ref[1:2, :])
        gd = lerp(gdc_ref, gdl_ref, mux_ref[2:3, :])
        tw = jnp.tanh(wd)
        kkv = k * kkw
        yield
        kk = kkv * lax.rsqrt(_mm(kkv * kkv, bdm) + 1e-6)
        ksum = jnp.zeros_like(k)
        for d in range(2):
            lw = -0.6065306597126334 * _sigmoid(vec_ref[3 + d:4 + d, :] + _mm(tw, wup_ref[d]))
            a = _sigmoid(vec_ref[5 + d:6 + d, :] + _mm(ad, aup_ref[d]))
            kdir = k * (1.0 + (a - 1.0) * kaw)
            ksum = ksum + kdir
            cum = _mm_sel_l(trit_ref[d, 0:n, 0:n], lw)
            einv = jnp.exp(-cum)
            dir_ref[d, 0, uo:uo + n, :] = cum
            dir_ref[d, 1, uo:uo + n, :] = kk * jnp.exp(cum - lw)
            dir_ref[d, 2, uo:uo + n, :] = r * jnp.exp(cum)
            dir_ref[d, 3, uo:uo + n, :] = kk * a * einv
            dir_ref[d, 4, uo:uo + n, :] = kdir * einv
            yield
        bonus = _mm_sel_r(r * ksum * rkw, bdm)
        vs_ref[uo:uo + n, :] = v
        gs_ref[uo:uo + n, :] = _mm(_sigmoid(gd), gup_ref[...])
        bvs_ref[uo:uo + n, :] = bonus * v
        yield

    tiles = _row_tiles(tc, tl)
    sched = _tile_schedule(tc, tl, grp)
    _round_robin([_chain([prep_tile(*t) for t, (need, _) in zip(tiles, sched) if need == 0])])
    fill = ([_chain([prep_tile(*t) for t, (need, _) in zip(tiles, sched) if need == 1])],
            [_chain([prep_tile(*t) for t, (need, _) in zip(tiles, sched) if need >= 2])])

    n_grp = n_all // grp
    pd = [(i, d) for i in range(grp) for d in range(2)]
    chunk_rows = lambda gi, i, d: _chunk_rows(gi, grp, i, d, n_ctx, n_all)

    def stage_a(gi, slot):
        def finish(pi, pdir, m, vbd):
            tri2 = jnp.concatenate([strict_ref[pdir], incl_ref[pdir]], axis=0)
            av = _mm(m[:, LANES:] * tri2, vbd)
            ab = m[:, :LANES] * tri2
            ls_ref[slot, pi, pdir, 0:CH, :] = ab[:CH]
            ls_ref[slot, pi, pdir, CH:2 * CH, :] = av[:CH]
            ls_ref[slot, pi, pdir, 2 * CH:3 * CH, :] = ab[CH:]
            ls_ref[slot, pi, pdir, 3 * CH:4 * CH, :] = av[CH:]
        pending = None
        for i, d in pd:
            rows = chunk_rows(gi, i, d)
            kkq, rq, binv, kinv = (dir_ref[d, j, rows, :] for j in range(1, 5))
            lhs = jnp.concatenate([kkq, rq], axis=0)
            rhs = jnp.concatenate([_bd(binv, m0, m1), _bd(kinv, m0, m1)], axis=0)
            cur = (i, d, _mm_nt(lhs, rhs), _bd(vs_ref[rows, :], m0, m1))
            if pending is not None:
                finish(*pending)
            pending = cur
            yield
        finish(*pending)
        yield

    def stage_b(gi, slot):
        mats = [ls_ref[slot, i, d, 0:CH, :] for i, d in pd]
        es = []
        yield from _inv_levels(mats, [d for _, d in pd], lvl_ref, m0, m1, es)
        sols = []
        for e, (i, d) in zip(es, pd):
            rows = chunk_rows(gi, i, d)
            rhs = jnp.concatenate([dir_ref[d, 1, rows, :], ls_ref[slot, i, d, CH:2 * CH, :]], axis=1)
            sols.append(rhs + _mm(e, _bd(rhs, m0w, m1w)))
        yield
        for sol, (i, d) in zip(sols, pd):
            rows = chunk_rows(gi, i, d)
            cum = dir_ref[d, 0, rows, :]
            etot = jnp.exp(cum[CH - 1:CH, :] if d == 0 else cum[0:1, :])
            bdec, kdec = dir_ref[d, 3, rows, :] * etot, dir_ref[d, 4, rows, :] * etot
            ar = _mm(ls_ref[slot, i, d, 2 * CH:3 * CH, :], _bd(sol, m0w, m1w))
            pmat = _mm_tn(sol[:, :LANES], bdec)
            nmat = _mm_tn(jnp.concatenate([-sol[:, LANES:], vs_ref[rows, :]], axis=0),
                          jnp.concatenate([bdec, kdec], axis=0))
            p1_ref[slot, i, d, 0:LANES, :] = pmat * bdm
            p1_ref[slot, i, d, LANES:2 * LANES, :] = nmat * bdm
            p1_ref[slot, i, d, 2 * LANES:2 * LANES + CH, :] = dir_ref[d, 2, rows, :] - ar[:, :LANES]
            p1_ref[slot, i, d, 2 * LANES + CH:2 * LANES + 2 * CH, :] = (
                ls_ref[slot, i, d, 3 * CH:4 * CH, :] - ar[:, LANES:])
            p1_ref[slot, i, d, 2 * LANES + 2 * CH:2 * LANES + 2 * CH + SUBLANES, :] = jnp.broadcast_to(
                etot, (SUBLANES, LANES))
        yield

    def stage_c(gi, slot, st):
        for i, d in pd:
            rows = chunk_rows(gi, i, d)
            pmat = p1_ref[slot, i, d, 0:LANES, :]
            nmat = p1_ref[slot, i, d, LANES:2 * LANES, :]
            rmat = p1_ref[slot, i, d, 2 * LANES:2 * LANES + CH, :]
            ymat = p1_ref[slot, i, d, 2 * LANES + CH:2 * LANES + 2 * CH, :]
            etot = p1_ref[slot, i, d, 2 * LANES + 2 * CH:2 * LANES + 2 * CH + 1, :]
            yref = yf_ref if d == 0 else yb_ref
            yref[rows, :] = _mm_nt(rmat, st[d]) + ymat
            st[d] = st[d] * etot - _mm(st[d], pmat) + nmat
            yield

    gnw, gnb = vec_ref[7:8, :], vec_ref[8:9, :]

    def out_tile(_, uo, n):
        y = yf_ref[uo:uo + n, :] + yb_ref[uo:uo + n, :]
        mean = _mm_sel_r(y, bdm) * (1.0 / HD)
        yield
        yc = y - mean
        var = _mm(yc * yc, bdm) * (1.0 / HD)
        yield
        out = (yc * lax.rsqrt(var + RWKV_GN_EPS) * gnw + gnb + bvs_ref[uo:uo + n, :]) * gs_ref[uo:uo + n, :]
        if uo < tc:
            oc_ref[uo:uo + n, :] = out
        else:
            ol_ref[uo - tc:uo - tc + n, :] = out
        yield

    z = jnp.zeros((LANES, LANES), F32)
    _run_pipeline(n_grp, stage_a, stage_b, stage_c, (z, z), b_before_a=False, fill=fill,
                  drain=_drain_streams(tiles, sched, n_grp, out_tile))


def _rwkv_mixer(rw_c, rw_l, mu, w0, w_up, a0, a_up, g_up, k_k, k_a, r_k, gn_w, gn_b, consts):
    bsz, tc, _ = rw_c.shape
    tl = rw_l.shape[1]
    tall = tc + tl
    n_all = tall // CH
    W = RWKV_WIDTH
    nb = W // LANES
    zr = jnp.zeros((RWKV_RANK, W), F32)
    pad_dir = lambda u: jnp.stack([jnp.concatenate([u[0], zr], axis=0), jnp.concatenate([zr, u[1]], axis=0)])
    wup, aup = pad_dir(w_up), pad_dir(a_up)
    vec = _rows([k_k, k_a, r_k.reshape(-1), w0[0], w0[1], a0[0], a0[1], gn_w, gn_b], 16)
    mu_rkv = mu[:3 * W].reshape(3, 1, W)
    mux = jnp.pad(mu[3 * W:].reshape(3, LANES), ((0, SUBLANES - 3), (0, 0)))
    cm = lambda *blk: pl.BlockSpec(blk, lambda b, p: (0,) * len(blk))

    def tok(t, col0, per_pair=True):
        if per_pair:
            return pl.BlockSpec((None, t, LANES), lambda b, p: (b, 0, col0 + p))
        return pl.BlockSpec((None, t, LANES), lambda b, p: (b, 0, col0))
    stream = lambda t: [tok(t, 0), tok(t, nb), tok(t, 2 * nb), tok(t, 3 * nb, False),
                        tok(t, 3 * nb + 1, False), tok(t, 3 * nb + 2, False)]
    in_specs = stream(tc) + stream(tl) + [
        pl.BlockSpec((None, 1, LANES), lambda b, p: (0, 0, p)),
        pl.BlockSpec((None, 1, LANES), lambda b, p: (1, 0, p)),
        pl.BlockSpec((None, 1, LANES), lambda b, p: (2, 0, p)),
        cm(SUBLANES, LANES),
        pl.BlockSpec((2, LANES, LANES), lambda b, p: (0, 0, p)),
        pl.BlockSpec((2, LANES, LANES), lambda b, p: (0, 0, p)),
        pl.BlockSpec((RWKV_G_RANK, LANES), lambda b, p: (0, p)),
        pl.BlockSpec((16, LANES), lambda b, p: (0, p)),
        cm(2, CH, LANES), cm(2, CH, LANES), cm(2, 6, CH, LANES), cm(2, 1, LANES), cm(LANES, LANES),
        cm(2, ROWT, ROWT)]
    scratch = [pltpu.VMEM((tall, LANES), F32)] * 3 + [
        pltpu.VMEM((2, 5, tall, LANES), F32),
        pltpu.VMEM((2, _group_size(n_all), 2, 4 * CH, LANES), F32),
        pltpu.VMEM((2, _group_size(n_all), 2, 2 * LANES + 2 * CH + SUBLANES, LANES), F32),
        pltpu.VMEM((tall, LANES), F32), pltpu.VMEM((tall, LANES), F32)]
    return pl.pallas_call(
        _rwkv_kernel,
        grid=(bsz, N_PAIRS),
        in_specs=in_specs,
        out_specs=[pl.BlockSpec((None, tc, LANES), lambda b, p: (b, 0, p)),
                   pl.BlockSpec((None, tl, LANES), lambda b, p: (b, 0, p))],
        out_shape=[jax.ShapeDtypeStruct((bsz, tc, W), F32),
                   jax.ShapeDtypeStruct((bsz, tl, W), F32)],
        scratch_shapes=scratch,
        compiler_params=pltpu.CompilerParams(dimension_semantics=("arbitrary", "arbitrary"),
                                             vmem_limit_bytes=VMEM_LIMIT),
        name="rwkv_mixer",
    )(*([rw_c] * 6), *([rw_l] * 6), mu_rkv, mu_rkv, mu_rkv, mux, wup, aup, g_up, vec,
      consts["incl"], consts["strict"], consts["lvl"], consts["hm"], consts["bdm"], consts["tri_t"])


def _lru_kernel(xc_ref, gc_ref, xl_ref, gl_ref, cw_ref, vec_ref, wcat_ref, bcat_ref,
                oc_ref, ol_ref, xp_ref, ab_ref, hf_ref, hb_ref):
    tc, tl = xc_ref.shape[0], xl_ref.shape[0]
    tall = tc + tl
    rows = tl // GRID_W
    W = xc_ref.shape[1]
    z = jnp.zeros((GAP, W), F32)
    xp_ref[0:GAP, :] = z
    xp_ref[GAP:GAP + tc, :] = xc_ref[...]
    xp_ref[GAP + tc:2 * GAP + tc, :] = z
    base = 2 * GAP + tc
    for c in range(GRID_W):
        xp_ref[base + c * rows:base + (c + 1) * rows, :] = xl_ref[pl.ds(c, rows, stride=GRID_W), :]
    xp_ref[base + tl:base + tl + GAP, :] = z
    cb = vec_ref[0:1, :]
    for (po, uo, n) in _row_tiles(tc, tl):
        acc = cb + cw_ref[0:1, :] * xp_ref[po - 2:po - 2 + n, :]
        for tap in range(1, CONV_W):
            acc = acc + cw_ref[tap:tap + 1, :] * xp_ref[po - 2 + tap:po - 2 + tap + n, :]
        gates = _mm(acc, wcat_ref[...]) + bcat_ref[...]
        for d in range(2):
            rg = _sigmoid(gates[:, (2 * d) * W:(2 * d + 1) * W])
            ig = _sigmoid(gates[:, (2 * d + 1) * W:(2 * d + 2) * W])
            log_a = -LRU_C * rg * _softplus(-vec_ref[1 + d:2 + d, :])
            a = jnp.exp(log_a)
            m2 = -jnp.tanh(log_a) * (1.0 + a * a)
            mult = jnp.where(m2 > 0.0, m2 * lax.rsqrt(m2), 0.0)
            ab_ref[d, 0, uo:uo + n, :] = a
            ab_ref[d, 1, uo:uo + n, :] = mult * (ig * acc)

    sub = lax.broadcasted_iota(jnp.int32, (SUBLANES, W), 0)
    n_tiles_c, n_tiles = tc // SUBLANES, tall // SUBLANES

    def tile_scan(a, b, d):
        for sh in (1, 2, 4):
            if d == 0:
                ok = sub >= sh
                a_s = jnp.where(ok, pltpu.roll(a, sh, 0), 1.0)
                b_s = jnp.where(ok, pltpu.roll(b, sh, 0), 0.0)
            else:
                ok = sub < SUBLANES - sh
                a_s = jnp.where(ok, pltpu.roll(a, SUBLANES - sh, 0), 1.0)
                b_s = jnp.where(ok, pltpu.roll(b, SUBLANES - sh, 0), 0.0)
            b = b + a * b_s
            a = a * a_s
        return a, b

    unroll = 4 if (n_tiles_c % 4 == 0 and n_tiles % 4 == 0) else 1

    def scan_body(s4, carry):
        hf, hb = carry
        tiles = []
        for j in range(unroll):
            s = s4 * unroll + j
            r0 = pl.multiple_of(s * SUBLANES, SUBLANES)
            tb = jnp.where(s < n_tiles_c, n_tiles_c - 1 - s, n_tiles + n_tiles_c - 1 - s)
            r1 = pl.multiple_of(tb * SUBLANES, SUBLANES)
            tiles.append((r0, tile_scan(ab_ref[0, 0, pl.ds(r0, SUBLANES), :], ab_ref[0, 1, pl.ds(r0, SUBLANES), :], 0),
                          r1, tile_scan(ab_ref[1, 0, pl.ds(r1, SUBLANES), :], ab_ref[1, 1, pl.ds(r1, SUBLANES), :], 1)))
        for r0, (af, bf), r1, (ab, bb) in tiles:
            h = bf + af * hf
            hf_ref[pl.ds(r0, SUBLANES), :] = h
            hf = jnp.broadcast_to(h[SUBLANES - 1:SUBLANES, :], (SUBLANES, W))
            h = bb + ab * hb
            hb_ref[pl.ds(r1, SUBLANES), :] = h
            hb = jnp.broadcast_to(h[0:1, :], (SUBLANES, W))
        return hf, hb
    z8 = jnp.zeros((SUBLANES, W), F32)
    lax.fori_loop(0, n_tiles // unroll, scan_body, (z8, z8))

    oc_ref[...] = (hf_ref[0:tc, :] + hb_ref[0:tc, :]) * _gelu_tanh(gc_ref[...])
    for c in range(GRID_W):
        h = hf_ref[tc + c * rows:tc + (c + 1) * rows, :] + hb_ref[tc + c * rows:tc + (c + 1) * rows, :]
        ol_ref[pl.ds(c, rows, stride=GRID_W), :] = h * _gelu_tanh(gl_ref[pl.ds(c, rows, stride=GRID_W), :])


def _lru_mixer(x_c, g_c, x_l, g_l, conv_w, conv_b, w_a, b_a, w_x, b_x, lam):
    bsz, tc, W = x_c.shape
    tl = x_l.shape[1]
    tall = tc + tl
    nh = W // LANES
    bph = LANES // LRU_BW
    w6 = jnp.stack([w_a, w_x], axis=1).reshape(2, 2, nh, bph, LRU_BW, LRU_BW)
    wcat = jnp.einsum("dghnrc,nm->hnrdgmc", w6, jnp.eye(bph, dtype=F32)).reshape(nh, LANES, 4 * LANES)
    bcat = jnp.stack([b_a, b_x], axis=1).reshape(2, 2, nh, LANES).transpose(2, 0, 1, 3).reshape(nh, 1, 4 * LANES)
    vec = _rows([conv_b, lam[0], lam[1]], SUBLANES)
    tok = lambda t: pl.BlockSpec((None, t, LANES), lambda b, h: (b, 0, h))
    return pl.pallas_call(
        _lru_kernel,
        grid=(bsz, nh),
        in_specs=[tok(tc), tok(tc), tok(tl), tok(tl),
                  pl.BlockSpec((CONV_W, LANES), lambda b, h: (0, h)),
                  pl.BlockSpec((SUBLANES, LANES), lambda b, h: (0, h)),
                  pl.BlockSpec((None, LANES, 4 * LANES), lambda b, h: (h, 0, 0)),
                  pl.BlockSpec((None, 1, 4 * LANES), lambda b, h: (h, 0, 0))],
        out_specs=[tok(tc), tok(tl)],
        out_shape=[jax.ShapeDtypeStruct((bsz, tc, W), F32), jax.ShapeDtypeStruct((bsz, tl, W), F32)],
        scratch_shapes=[pltpu.VMEM((tall + 3 * GAP, LANES), F32),
                        pltpu.VMEM((2, 2, tall, LANES), F32),
                        pltpu.VMEM((tall, LANES), F32), pltpu.VMEM((tall, LANES), F32)],
        compiler_params=pltpu.CompilerParams(dimension_semantics=("arbitrary", "arbitrary"),
                                             vmem_limit_bytes=VMEM_LIMIT),
        name="lru_mixer",
    )(x_c, g_c, x_l, g_l, conv_w, vec, wcat.astype(BF16), bcat)


def _finish_kernel(x_ref, gdn_ref, lru_ref, rwk_ref, m2_ref, m3_ref, m4_ref, m5_ref, nrm_ref,
                   wo_ref, up_ref, dn_ref, o_ref):
    x = x_ref[...]
    o = (jnp.dot(gdn_ref[...].astype(BF16), wo_ref[0:GDN_WIDTH, :], preferred_element_type=F32)
         + jnp.dot(lru_ref[...].astype(BF16), wo_ref[GDN_WIDTH:GDN_WIDTH + LRU_WIDTH, :],
                   preferred_element_type=F32)
         + jnp.dot(rwk_ref[...].astype(BF16), wo_ref[GDN_WIDTH + LRU_WIDTH:, :], preferred_element_type=F32))
    x = x + m2_ref[...] * _rms(o, nrm_ref[0:1, :])
    h = (_rms(x, nrm_ref[1:2, :]) * (1.0 + m4_ref[...]) + m3_ref[...]).astype(BF16)
    f = jnp.zeros_like(x)
    fc = 1024
    for j in range(D_FF // fc):
        a = jnp.maximum(jnp.dot(h, up_ref[:, j * fc:(j + 1) * fc], preferred_element_type=F32), 0.0)
        f = f + jnp.dot((a * a).astype(BF16), dn_ref[j * fc:(j + 1) * fc, :], preferred_element_type=F32)
    o_ref[...] = x + m5_ref[...] * _rms(f, nrm_ref[2:3, :])


def _finish(x2, gdn, lru, rwk, mod_rows, rows_per_mod, norms, wo_bf, up_bf, dn_bf, layer):
    n = x2.shape[0]
    tm = DENSE_TM
    tiles_per_mod = rows_per_mod // tm
    modspec = lambda k: pl.BlockSpec((None, 1, D_MODEL), lambda i: (6 * (i // tiles_per_mod) + k, 0, 0))
    cm = lambda *blk: pl.BlockSpec((None,) + blk, lambda i: (layer,) + (0,) * len(blk),
                                   pipeline_mode=pl.Buffered(1))
    return pl.pallas_call(
        _finish_kernel,
        grid=(n // tm,),
        in_specs=[pl.BlockSpec((tm, D_MODEL), lambda i: (i, 0)),
                  pl.BlockSpec((tm, GDN_WIDTH), lambda i: (i, 0)),
                  pl.BlockSpec((tm, LRU_WIDTH), lambda i: (i, 0)),
                  pl.BlockSpec((tm, RWKV_WIDTH), lambda i: (i, 0)),
                  modspec(2), modspec(3), modspec(4), modspec(5),
                  cm(SUBLANES, D_MODEL), cm(D_MODEL, D_MODEL), cm(D_MODEL, D_FF), cm(D_FF, D_MODEL)],
        out_specs=pl.BlockSpec((tm, D_MODEL), lambda i: (i, 0)),
        out_shape=jax.ShapeDtypeStruct((n, D_MODEL), F32),
        compiler_params=pltpu.CompilerParams(dimension_semantics=("arbitrary",),
                                             vmem_limit_bytes=VMEM_LIMIT),
        name="finish",
    )(x2, gdn, lru, rwk, mod_rows, mod_rows, mod_rows, mod_rows, norms, wo_bf, up_bf, dn_bf)


def kernel(x, c, ctx, c_ctx, ada_w, ada_b, norm_mix_pre, norm_mix_post, norm_ffn_pre, norm_ffn_post, w_in, gdn_conv, gdn_a_log, gdn_dt_bias, gdn_norm, lru_conv, lru_conv_b, lru_wa, lru_ba, lru_wx, lru_bx, lru_lambda, rwkv_mu, rwkv_w0, rwkv_w_up, rwkv_a0, rwkv_a_up, rwkv_g_up, rwkv_k_k, rwkv_k_a, rwkv_r_k, rwkv_gn_w, rwkv_gn_b, w_out, ffn_up, ffn_down):
    bsz, tl, _ = x.shape
    tc = ctx.shape[1]
    depth = w_in.shape[0]
    consts = _device_consts()

    cvec = jnp.pad(jnp.concatenate([c, c_ctx[None, :]], axis=0), ((0, 16 - bsz - 1), (0, 0)))
    mods = _ada_mod(cvec, ada_w, ada_b).reshape(depth, 16, 6, D_MODEL)

    w_in_bf = w_in.astype(BF16)
    wo_bf, up_bf, dn_bf = w_out.astype(BF16), ffn_up.astype(BF16), ffn_down.astype(BF16)
    g_pre = norm_mix_pre.reshape(depth, 1, D_MODEL)
    norms = jnp.pad(jnp.stack([norm_mix_post, norm_ffn_pre, norm_ffn_post], axis=1),
                    ((0, 0), (0, SUBLANES - 3), (0, 0)))

    xl = x.reshape(bsz * tl, D_MODEL)
    xc = ctx.reshape(bsz * tc, D_MODEL)
    for i in range(depth):
        mod_l = mods[i, 0:bsz].reshape(bsz * 6, 1, D_MODEL)
        mod_c = mods[i, bsz:bsz + 1].reshape(6, 1, D_MODEL)
        pl_ = _inproj(xl, mod_l, tl, g_pre, w_in_bf, i)
        pc_ = _inproj(xc, mod_c, bsz * tc, g_pre, w_in_bf, i)
        r3 = lambda a, t: a.reshape(bsz, t, a.shape[-1])
        qkv_l, z_l, ba_l, lx_l, lg_l, rw_l = (r3(a, tl) for a in pl_)
        qkv_c, z_c, ba_c, lx_c, lg_c, rw_c = (r3(a, tc) for a in pc_)

        gdn_c, gdn_l = _gdn_mixer(qkv_c, z_c, ba_c, qkv_l, z_l, ba_l, gdn_conv[i], gdn_a_log[i],
                                  gdn_dt_bias[i], gdn_norm[i], consts)
        lru_c, lru_l = _lru_mixer(lx_c, lg_c, lx_l, lg_l, lru_conv[i], lru_conv_b[i], lru_wa[i], lru_ba[i],
                                  lru_wx[i], lru_bx[i], lru_lambda[i])
        rwk_c, rwk_l = _rwkv_mixer(rw_c, rw_l, rwkv_mu[i], rwkv_w0[i], rwkv_w_up[i], rwkv_a0[i],
                                   rwkv_a_up[i], rwkv_g_up[i], rwkv_k_k[i], rwkv_k_a[i], rwkv_r_k[i],
                                   rwkv_gn_w[i], rwkv_gn_b[i], consts)

        f2 = lambda a: a.reshape(-1, a.shape[-1])
        xl = _finish(xl, f2(gdn_l), f2(lru_l), f2(rwk_l), mod_l, tl, norms, wo_bf, up_bf, dn_bf, i)
        if i < depth - 1:
            xc = _finish(xc, f2(gdn_c), f2(lru_c), f2(rwk_c), mod_c, bsz * tc, norms, wo_bf, up_bf, dn_bf, i)
    return xl.reshape(bsz, tl, D_MODEL)
```

```python
import functools

import numpy as np
import jax
import jax.numpy as jnp
from jax import lax
from jax.experimental import pallas as pl
from jax.experimental.pallas import tpu as pltpu

F32 = jnp.float32
BF16 = jnp.bfloat16

LANES = 128
SUBLANES = 8
VMEM_LIMIT = 56 * 1024 * 1024

D_MODEL = 1024
DEPTH = 2
GRID_W = 64
CONV_W = 4
EPS = 1e-6
D_FF = 4 * D_MODEL
HD = 64
GDN_WIDTH = 3 * D_MODEL // 8
GDN_HEADS = GDN_WIDTH // HD
LRU_WIDTH = D_MODEL // 4
LRU_BLOCKS = 4
LRU_BW = LRU_WIDTH // LRU_BLOCKS
LRU_C = 8.0
RWKV_WIDTH = D_MODEL - GDN_WIDTH - LRU_WIDTH
RWKV_HEADS = RWKV_WIDTH // HD
RWKV_RANK = 64
RWKV_G_RANK = 128
RWKV_GN_EPS = 6.4e-4
RWKV_IN = 3 * RWKV_WIDTH + 2 * RWKV_RANK + 2 * RWKV_RANK + RWKV_G_RANK
N_PAIRS = GDN_HEADS // 2
BA_W = LANES
_W_OFF = np.cumsum([0, 3 * GDN_WIDTH, GDN_WIDTH, 2 * GDN_HEADS, 2 * GDN_HEADS, LRU_WIDTH, LRU_WIDTH, RWKV_IN])
D_INP = int(_W_OFF[-1])
P_SLICES = [(0, 3 * GDN_WIDTH), (int(_W_OFF[1]), GDN_WIDTH), (int(_W_OFF[2]), BA_W),
            (int(_W_OFF[4]), LRU_WIDTH), (int(_W_OFF[5]), LRU_WIDTH), (int(_W_OFF[6]), RWKV_IN)]

CH = 64
ROWT = 256
DENSE_TM = 512
GAP = SUBLANES


def _mm(a, b):
    return jnp.dot(a.astype(BF16), b.astype(BF16), preferred_element_type=F32)


def _mm_nt(a, b):
    return lax.dot_general(a.astype(BF16), b.astype(BF16), (((1,), (1,)), ((), ())),
                           preferred_element_type=F32)


def _mm_tn(a, b):
    return lax.dot_general(a.astype(BF16), b.astype(BF16), (((0,), (0,)), ((), ())),
                           preferred_element_type=F32)


def _split2(x):
    hi = x.astype(BF16)
    lo = (x - hi.astype(F32)).astype(BF16)
    return hi, lo


def _mm_sel_l(m01, x):
    mb = m01.astype(BF16)
    h, l = _split2(x)
    d = functools.partial(jnp.dot, preferred_element_type=F32)
    return d(mb, h) + d(mb, l)


def _mm_sel_r(x, m01):
    mb = m01.astype(BF16)
    h, l = _split2(x)
    d = functools.partial(jnp.dot, preferred_element_type=F32)
    return d(h, mb) + d(l, mb)


def _sigmoid(x):
    return 0.5 * jnp.tanh(0.5 * x) + 0.5


def _silu_of_twice(h):
    return h + h * jnp.tanh(h)


def _silu(x):
    return _silu_of_twice(0.5 * x)


def _softplus(x):
    return jnp.maximum(x, 0.0) + jnp.log(1.0 + jnp.exp(-jnp.abs(x)))


def _gelu_tanh(x):
    return 0.5 * x * (1.0 + jnp.tanh(0.7978845608028654 * (x + 0.044715 * (x * x * x))))


def _bd(y, m0, m1):
    yb = y.astype(BF16)
    return jnp.concatenate([yb * m0, yb * m1], axis=0)


def _pair_consts():
    i = np.arange(CH)[:, None]
    j = (np.arange(LANES) % HD)[None, :]
    incl = np.stack([i >= j, i <= j]).astype(np.float32)
    strict = np.stack([i > j, i < j]).astype(np.float32)
    lvls = []
    for d in range(2):
        per = []
        for m in (1, 2, 4, 8, 16, 32):
            same = (i // (2 * m)) == (j // (2 * m))
            lo_i, lo_j = (i % (2 * m)) < m, (j % (2 * m)) < m
            off = same & (~lo_i) & lo_j if d == 0 else same & lo_i & (~lo_j)
            per.append(off)
        lvls.append(np.stack(per))
    lvl = np.stack(lvls).astype(np.float32)
    lane = np.arange(LANES)
    hm = np.stack([lane < HD, lane >= HD]).astype(np.float32)[:, None, :]
    bdm = ((np.arange(LANES)[:, None] // HD) == (lane[None, :] // HD)).astype(np.float32)
    r = np.arange(ROWT)[:, None]
    c = np.arange(ROWT)[None, :]
    same = (r // CH) == (c // CH)
    tri_t = np.stack([same & (r >= c), same & (r <= c)]).astype(np.float32)
    return dict(incl=incl, strict=strict, lvl=lvl, hm=hm, bdm=bdm, tri_t=tri_t)


def _device_consts():
    out = {k: jnp.asarray(v) for k, v in _pair_consts().items()}
    out["tri_t"] = out["tri_t"].astype(BF16)
    return out


def _inv_levels(mats, dirs, lvl_ref, m0, m1, out):
    es = [-(a * lvl_ref[d, 0]) for a, d in zip(mats, dirs)]
    for k in range(1, 6):
        offs = [a * lvl_ref[d, k] for a, d in zip(mats, dirs)]
        xs = [off + p for off, p in zip(offs, _mm_pairs(es, [_bd(off, m0, m1) for off in offs]))]
        yield
        es = [e - x - p for e, x, p in zip(es, xs, _mm_pairs(xs, [_bd(e, m0, m1) for e in es]))]
        yield
    out.extend(es)


def _mm_pairs(lhs, rhs):
    out = []
    for j in range(0, len(lhs) - 1, 2):
        r = jnp.dot(jnp.concatenate([lhs[j], lhs[j + 1]], axis=0).astype(BF16),
                    jnp.concatenate([rhs[j], rhs[j + 1]], axis=1), preferred_element_type=F32)
        out += [r[:CH, :LANES], r[CH:, LANES:]]
    if len(lhs) % 2:
        out.append(_mm(lhs[-1], rhs[-1]))
    return out


def _rows(vectors, n_rows):
    m = jnp.stack([v.astype(F32) for v in vectors])
    return jnp.pad(m, ((0, n_rows - m.shape[0]), (0, 0)))


def _round_robin(gens):
    gens = list(gens)
    while gens:
        alive = []
        for g in gens:
            try:
                next(g)
                alive.append(g)
            except StopIteration:
                pass
        gens = alive


def _group_size(n_all):
    for g in (6, 4, 3, 2):
        if n_all % g == 0:
            return g
    return 1


def _chunk_order(s, n_ctx, n_all, d):
    if d == 0:
        return s
    if isinstance(s, int):
        return n_ctx - 1 - s if s < n_ctx else n_all + n_ctx - 1 - s
    return jnp.where(s < n_ctx, n_ctx - 1 - s, n_all + n_ctx - 1 - s)


def _chunk_rows(gi, grp, i, d, n_ctx, n_all):
    c = _chunk_order(gi * grp + i, n_ctx, n_all, d)
    return pl.ds(c * CH if isinstance(c, int) else pl.multiple_of(c * CH, CH), CH)


def _run_pipeline(n_grp, stage_a, stage_b, stage_c, st, b_before_a=True, fill=((), ()), drain=((), ())):
    def step(t, st, do_a, do_b, do_c, extra=()):
        par = t % 2 if isinstance(t, int) else lax.rem(t, 2)
        st = list(st)
        gens = []
        if do_c:
            gens.append(stage_c(t - 2, par, st))
        ab = ([stage_b(t - 1, 1 - par)] if do_b else []) + ([stage_a(t, par)] if do_a else [])
        _round_robin(gens + (ab if b_before_a else ab[::-1]) + list(extra))
        return tuple(st)
    assert n_grp >= 2
    st = step(0, st, True, False, False, fill[0])
    st = step(1, st, True, True, False, fill[1])
    st = lax.fori_loop(2, n_grp, lambda t, s: step(t, s, True, True, True), st)
    st = step(n_grp, st, False, True, True, drain[0])
    return step(n_grp + 1, st, False, False, True, drain[1])


def _tile_schedule(tc, tl, grp):
    n_ctx, n_all = tc // CH, (tc + tl) // CH
    first, last = {}, {}
    for s in range(n_all):
        g = s // grp
        for d in range(2):
            c = _chunk_order(s, n_ctx, n_all, d)
            first[c] = min(first.get(c, g), g)
            last[c] = max(last.get(c, (0, 0)), (g + 2, 2 * (s % grp) + d))
    out = []
    for (_, uo, n) in _row_tiles(tc, tl):
        cs = range(uo // CH, (uo + n) // CH)
        out.append((min(first[c] for c in cs), max(last[c] for c in cs)))
    return out


def _chain(gens):
    for g in gens:
        yield from g


def _delayed(gen, rounds):
    for _ in range(rounds):
        yield
    yield from gen


def _drain_streams(tiles, sched, n_grp, out_tile):
    drain = ([], [])
    for t, (_, (step, sub)) in zip(tiles, sched):
        if step < n_grp:
            drain[0].append(out_tile(*t))
        else:
            drain[step - n_grp].append(_delayed(out_tile(*t), sub + 1))
    return drain


def _fill_padded(dst_ref, src_c_ref, src_l_ref, tc, tl):
    w = dst_ref.shape[1]
    z = jnp.zeros((GAP, w), F32)
    dst_ref[0:GAP, :] = z
    dst_ref[GAP:GAP + tc, :] = src_c_ref[...]
    dst_ref[GAP + tc:2 * GAP + tc, :] = z
    dst_ref[2 * GAP + tc:2 * GAP + tc + tl, :] = src_l_ref[...]
    dst_ref[2 * GAP + tc + tl:3 * GAP + tc + tl, :] = z


def _row_tiles(tc, tl):
    out = []
    for base_p, base_u, n in ((GAP, 0, tc), (2 * GAP + tc, tc, tl)):
        for t0 in range(0, n, ROWT):
            out.append((base_p + t0, base_u + t0, min(ROWT, n - t0)))
    return out


def _ada_kernel(c_ref, w_ref, b_ref, o_ref):
    c = c_ref[...]
    o_ref[...] = _mm(_silu(c), w_ref[...]) + b_ref[...]


def _ada_mod(cvec, ada_w, ada_b):
    L = ada_w.shape[0]
    n = ada_w.shape[2]
    tn = 1536
    return pl.pallas_call(
        _ada_kernel,
        grid=(L, n // tn),
        in_specs=[pl.BlockSpec((16, D_MODEL), lambda l, j: (0, 0)),
                  pl.BlockSpec((None, D_MODEL, tn), lambda l, j: (l, 0, j)),
                  pl.BlockSpec((None, 1, tn), lambda l, j: (l, 0, j))],
        out_specs=pl.BlockSpec((None, 16, tn), lambda l, j: (l, 0, j)),
        out_shape=jax.ShapeDtypeStruct((L, 16, n), F32),
        compiler_params=pltpu.CompilerParams(dimension_semantics=("arbitrary", "arbitrary"),
                                             vmem_limit_bytes=VMEM_LIMIT),
        name="ada_mod",
    )(cvec, ada_w, ada_b.reshape(L, 1, n))


def _rms(x, g):
    return x * lax.rsqrt(jnp.mean(x * x, axis=-1, keepdims=True) + EPS) * g


def _inproj_kernel(x_ref, sh_ref, sc_ref, g_ref, w_ref, qkv_ref, z_ref, ba_ref, lx_ref, lg_ref, rw_ref):
    h = _rms(x_ref[...], g_ref[...]) * (1.0 + sc_ref[...]) + sh_ref[...]
    p = jnp.dot(h.astype(BF16), w_ref[...], preferred_element_type=F32)
    for ref, (start, width) in zip((qkv_ref, z_ref, ba_ref, lx_ref, lg_ref, rw_ref), P_SLICES):
        ref[...] = p[:, start:start + width]


def _inproj(x2, mod_rows, rows_per_mod, g_all, w_all, layer):
    n = x2.shape[0]
    tm = DENSE_TM
    tiles_per_mod = rows_per_mod // tm
    widths = [w for _, w in P_SLICES]
    return pl.pallas_call(
        _inproj_kernel,
        grid=(n // tm,),
        in_specs=[pl.BlockSpec((tm, D_MODEL), lambda i: (i, 0)),
                  pl.BlockSpec((None, 1, D_MODEL), lambda i: (6 * (i // tiles_per_mod), 0, 0)),
                  pl.BlockSpec((None, 1, D_MODEL), lambda i: (6 * (i // tiles_per_mod) + 1, 0, 0)),
                  pl.BlockSpec((None, 1, D_MODEL), lambda i: (layer, 0, 0)),
                  pl.BlockSpec((None, D_MODEL, D_INP), lambda i: (layer, 0, 0), pipeline_mode=pl.Buffered(1))],
        out_specs=[pl.BlockSpec((tm, w), lambda i: (i, 0)) for w in widths],
        out_shape=[jax.ShapeDtypeStruct((n, w), F32) for w in widths],
        compiler_params=pltpu.CompilerParams(dimension_semantics=("arbitrary",),
                                             vmem_limit_bytes=VMEM_LIMIT),
        name="inproj",
    )(x2, mod_rows, mod_rows, g_all, w_all)


def _gdn_kernel(qc_ref, kc_ref, vc_ref, zc_ref, bac_ref, ql_ref, kl_ref, vl_ref, zl_ref, bal_ref,
                cwq_ref, cwk_ref, cwv_ref, aux_ref, e_ref,
                incl_ref, strict_ref, lvl_ref, hm_ref, bdm_ref, trit_ref,
                oc_ref, ol_ref,
                qp_ref, kp_ref, vp_ref, qs_ref, ks_ref, vs_ref, dir_ref, ls_ref, p1_ref, of_ref, ob_ref):
    tc, tl = qc_ref.shape[0], ql_ref.shape[0]
    tall = tc + tl
    n_ctx, n_all = tc // CH, tall // CH
    grp = _group_size(n_all)
    m0, m1 = hm_ref[0].astype(BF16), hm_ref[1].astype(BF16)
    m0w = jnp.concatenate([m0, m0], axis=1)
    m1w = jnp.concatenate([m1, m1], axis=1)
    bdm = bdm_ref[...]

    _fill_padded(qp_ref, qc_ref, ql_ref, tc, tl)
    _fill_padded(kp_ref, kc_ref, kl_ref, tc, tl)
    _fill_padded(vp_ref, vc_ref, vl_ref, tc, tl)
    lane = lax.broadcasted_iota(jnp.int32, (1, LANES), 1)
    alog, dtb = aux_ref[0:1, :], aux_ref[1:2, :]

    def prep_tile(po, uo, n):
        def conv(src, cw_ref):
            cw = 0.5 * cw_ref[...]
            acc = cw[0:1, :] * src[po - 2:po - 2 + n, :]
            for tap in range(1, CONV_W):
                acc = acc + cw[tap:tap + 1, :] * src[po - 2 + tap:po - 2 + tap + n, :]
            return _silu_of_twice(acc)
        q = conv(qp_ref, cwq_ref)
        k = conv(kp_ref, cwk_ref)
        yield
        q = q * lax.rsqrt(_mm(q * q, bdm) + 1e-6) * (HD ** -0.5)
        k = k * lax.rsqrt(_mm(k * k, bdm) + 1e-6)
        qs_ref[uo:uo + n, :] = q
        ks_ref[uo:uo + n, :] = k
        vs_ref[uo:uo + n, :] = conv(vp_ref, cwv_ref)
        yield
        ba = bac_ref[uo:uo + n, :] if uo < tc else bal_ref[uo - tc:uo - tc + n, :]
        beta = _sigmoid(ba)
        gval = -jnp.exp(alog) * _softplus(ba + dtb)
        bgv = jnp.where(lane < 2 * GDN_HEADS, beta, gval)
        ex = _mm_sel_r(bgv, e_ref[...])
        incl_t = [jnp.concatenate([incl_ref[dd]] * (n // CH), axis=0) for dd in range(2)]
        yield
        for d in range(2):
            g = ex[:, (2 + d) * LANES:(3 + d) * LANES]
            gc = _mm_sel_l(trit_ref[d, 0:n, 0:n], g)
            gt = gc.T
            gr = jnp.concatenate(
                [jnp.broadcast_to(jnp.concatenate([gt[0:1, c * CH:(c + 1) * CH], gt[HD:HD + 1, c * CH:(c + 1) * CH]],
                                                  axis=1), (CH, LANES)) for c in range(n // CH)], axis=0)
            dir_ref[d, 0, uo:uo + n, :] = ex[:, d * LANES:(d + 1) * LANES]
            dir_ref[d, 1, uo:uo + n, :] = gc
            dir_ref[d, 2, uo:uo + n, :] = jnp.exp(jnp.minimum(gc - gr, 0.0)) * incl_t[d]
            yield

    tiles = _row_tiles(tc, tl)
    sched = _tile_schedule(tc, tl, grp)
    _round_robin([prep_tile(*t) for t, (need, _) in zip(tiles, sched) if need == 0])
    fill = ([prep_tile(*t) for t, (need, _) in zip(tiles, sched) if need == 1],
            [prep_tile(*t) for t, (need, _) in zip(tiles, sched) if need >= 2])

    n_grp = n_all // grp
    chunk_rows = lambda gi, i, d: _chunk_rows(gi, grp, i, d, n_ctx, n_all)
    pd = [(i, d) for i in range(grp) for d in range(2)]

    def stage_a(gi, slot):
        for i, d in pd:
            rows = chunk_rows(gi, i, d)
            q, k = qs_ref[rows, :], ks_ref[rows, :]
            beta, dec = dir_ref[d, 0, rows, :], dir_ref[d, 2, rows, :]
            la = _mm_nt(jnp.concatenate([k, q], axis=0), _bd(k, m0, m1))
            ls_ref[slot, i, d, 0:CH, :] = la[:CH] * dec * strict_ref[d] * beta
            ls_ref[slot, i, d, CH:2 * CH, :] = la[CH:] * dec
            yield

    def stage_b(gi, slot):
        mats = [ls_ref[slot, i, d, 0:CH, :] for i, d in pd]
        es = []
        yield from _inv_levels(mats, [d for _, d in pd], lvl_ref, m0, m1, es)
        sols, kds, egs, gls = [], [], [], []
        for e, (i, d) in zip(es, pd):
            rows = chunk_rows(gi, i, d)
            k, v = ks_ref[rows, :], vs_ref[rows, :]
            beta, gc = dir_ref[d, 0, rows, :], dir_ref[d, 1, rows, :]
            eg = jnp.exp(gc)
            rhs = jnp.concatenate([v * beta, k * beta * eg], axis=1)
            sols.append(rhs + _mm(e, _bd(rhs, m0w, m1w)))
            glast = gc[CH - 1:CH, :] if d == 0 else gc[0:1, :]
            kds.append(k * jnp.exp(glast - gc))
            egs.append(eg)
            gls.append(glast)
        yield
        for sol, kd, eg, glast, (i, d) in zip(sols, kds, egs, gls, pd):
            rows = chunk_rows(gi, i, d)
            attn = ls_ref[slot, i, d, CH:2 * CH, :]
            au = _mm(attn, _bd(sol, m0w, m1w))
            kn = _mm_tn(kd, sol)
            p1_ref[slot, i, d, 0:LANES, :] = kn[:, LANES:] * bdm
            p1_ref[slot, i, d, LANES:LANES + CH, :] = qs_ref[rows, :] * eg - au[:, LANES:]
            p1_ref[slot, i, d, LANES + CH:2 * LANES + CH, :] = kn[:, :LANES] * bdm
            p1_ref[slot, i, d, 2 * LANES + CH:2 * LANES + 2 * CH, :] = au[:, :LANES]
            p1_ref[slot, i, d, 2 * LANES + 2 * CH:2 * LANES + 2 * CH + SUBLANES, :] = jnp.broadcast_to(
                jnp.exp(glast), (SUBLANES, LANES))
        yield

    def stage_c(gi, slot, st):
        for i, d in pd:
            rows = chunk_rows(gi, i, d)
            kwq = p1_ref[slot, i, d, 0:LANES + CH, :]
            nmat = p1_ref[slot, i, d, LANES + CH:2 * LANES + CH, :]
            omat = p1_ref[slot, i, d, 2 * LANES + CH:2 * LANES + 2 * CH, :]
            egl = p1_ref[slot, i, d, 2 * LANES + 2 * CH:2 * LANES + 2 * CH + 1, :]
            ks = _mm(kwq, st[d])
            oref = of_ref if d == 0 else ob_ref
            oref[rows, :] = ks[LANES:] + omat
            st[d] = st[d] * egl - ks[:LANES] + nmat
            yield

    nw = aux_ref[2:3, :]

    def out_tile(po, uo, n):
        o = of_ref[uo:uo + n, :] + ob_ref[uo:uo + n, :]
        ms = _mm(o * o, bdm) * (1.0 / HD)
        yield
        if uo < tc:
            oc_ref[uo:uo + n, :] = o * lax.rsqrt(ms + EPS) * nw * _silu(zc_ref[uo:uo + n, :])
        else:
            lo = uo - tc
            ol_ref[lo:lo + n, :] = o * lax.rsqrt(ms + EPS) * nw * _silu(zl_ref[lo:lo + n, :])
        yield

    z = jnp.zeros((LANES, LANES), F32)
    _run_pipeline(n_grp, stage_a, stage_b, stage_c, (z, z), fill=fill,
                  drain=_drain_streams(tiles, sched, n_grp, out_tile))


def _gdn_expand_consts():
    e = np.zeros((N_PAIRS, BA_W, 4 * LANES), np.float32)
    for p in range(N_PAIRS):
        for blk in range(4):
            d, is_g = blk % 2, blk // 2
            for h in range(2):
                col = is_g * 2 * GDN_HEADS + d * GDN_HEADS + 2 * p + h
                e[p, col, blk * LANES + h * HD:blk * LANES + (h + 1) * HD] = 1.0
    return e


def _gdn_mixer(qkv_c, z_c, ba_c, qkv_l, z_l, ba_l, conv_w, a_log, dt_bias, norm_w, consts):
    bsz, tc, _ = qkv_c.shape
    tl = qkv_l.shape[1]
    tall = tc + tl
    n_all = tall // CH
    lane_pad = lambda a: jnp.pad(a.reshape(-1), (2 * GDN_HEADS, LANES - 4 * GDN_HEADS))
    aux = _rows([lane_pad(a_log), lane_pad(dt_bias), jnp.tile(norm_w, 2)], SUBLANES)
    e = jnp.asarray(_gdn_expand_consts())
    cm = lambda *blk: pl.BlockSpec(blk, lambda b, p: (0,) * len(blk))

    def tok(t, col0):
        return pl.BlockSpec((None, t, LANES), lambda b, p: (b, 0, col0 + p))
    in_specs = [tok(tc, 0), tok(tc, N_PAIRS), tok(tc, 2 * N_PAIRS), tok(tc, 0),
                pl.BlockSpec((None, tc, BA_W), lambda b, p: (b, 0, 0)),
                tok(tl, 0), tok(tl, N_PAIRS), tok(tl, 2 * N_PAIRS), tok(tl, 0),
                pl.BlockSpec((None, tl, BA_W), lambda b, p: (b, 0, 0)),
                pl.BlockSpec((CONV_W, LANES), lambda b, p: (0, p)),
                pl.BlockSpec((CONV_W, LANES), lambda b, p: (0, N_PAIRS + p)),
                pl.BlockSpec((CONV_W, LANES), lambda b, p: (0, 2 * N_PAIRS + p)),
                cm(SUBLANES, LANES),
                pl.BlockSpec((None, BA_W, 4 * LANES), lambda b, p: (p, 0, 0)),
                cm(2, CH, LANES), cm(2, CH, LANES), cm(2, 6, CH, LANES), cm(2, 1, LANES), cm(LANES, LANES),
                cm(2, ROWT, ROWT)]
    pad_rows = tall + 3 * GAP
    scratch = [pltpu.VMEM((pad_rows, LANES), F32)] * 3 + [pltpu.VMEM((tall, LANES), F32)] * 3 + [
        pltpu.VMEM((2, 3, tall, LANES), F32),
        pltpu.VMEM((2, _group_size(n_all), 2, 2 * CH, LANES), F32),
        pltpu.VMEM((2, _group_size(n_all), 2, 2 * LANES + 2 * CH + SUBLANES, LANES), F32),
        pltpu.VMEM((tall, LANES), F32), pltpu.VMEM((tall, LANES), F32)]
    return pl.pallas_call(
        _gdn_kernel,
        grid=(bsz, N_PAIRS),
        in_specs=in_specs,
        out_specs=[pl.BlockSpec((None, tc, LANES), lambda b, p: (b, 0, p)),
                   pl.BlockSpec((None, tl, LANES), lambda b, p: (b, 0, p))],
        out_shape=[jax.ShapeDtypeStruct((bsz, tc, GDN_WIDTH), F32),
                   jax.ShapeDtypeStruct((bsz, tl, GDN_WIDTH), F32)],
        scratch_shapes=scratch,
        compiler_params=pltpu.CompilerParams(dimension_semantics=("arbitrary", "arbitrary"),
                                             vmem_limit_bytes=VMEM_LIMIT),
        name="gdn_mixer",
    )(qkv_c, qkv_c, qkv_c, z_c, ba_c, qkv_l, qkv_l, qkv_l, z_l, ba_l,
      conv_w, conv_w, conv_w, aux, e,
      consts["incl"], consts["strict"], consts["lvl"], consts["hm"], consts["bdm"],
      consts["tri_t"])


def _rwkv_kernel(rc_ref, kc_ref, vc_ref, wdc_ref, adc_ref, gdc_ref,
                 rl_ref, kl_ref, vl_ref, wdl_ref, adl_ref, gdl_ref,
                 mur_ref, muk_ref, muv_ref, mux_ref, wup_ref, aup_ref, gup_ref, vec_ref,
                 incl_ref, strict_ref, lvl_ref, hm_ref, bdm_ref, trit_ref,
                 oc_ref, ol_ref,
                 vs_ref, gs_ref, bvs_ref, dir_ref, ls_ref, p1_ref, yf_ref, yb_ref):
    tc, tl = rc_ref.shape[0], rl_ref.shape[0]
    tall = tc + tl
    n_ctx, n_all = tc // CH, tall // CH
    grp = _group_size(n_all)
    m0, m1 = hm_ref[0].astype(BF16), hm_ref[1].astype(BF16)
    m0w = jnp.concatenate([m0, m0], axis=1)
    m1w = jnp.concatenate([m1, m1], axis=1)
    bdm = bdm_ref[...]

    kkw, kaw, rkw = vec_ref[0:1, :], vec_ref[1:2, :], vec_ref[2:3, :]

    def prep_tile(_, uo, n):
        ctx_tile = uo < tc
        lo = uo if ctx_tile else uo - tc
        t_len = tc if ctx_tile else tl
        rowi = lax.broadcasted_iota(jnp.int32, (n, LANES), 0)

        def lerp(src_c, src_l, mu):
            src = src_c if ctx_tile else src_l
            x = src[lo:lo + n, :]
            if lo > 0:
                xm = src[lo - 1:lo - 1 + n, :]
            else:
                xm = jnp.where(rowi == 0, 0.0, pltpu.roll(x, 1, 0))
            if lo + n < t_len:
                xp = src[lo + 1:lo + 1 + n, :]
            else:
                xp = jnp.where(rowi == n - 1, 0.0, pltpu.roll(x, n - 1, 0))
            return x + mu * (0.5 * (xm + xp) - x)
        r = lerp(rc_ref, rl_ref, mur_ref[...])
        k = lerp(kc_ref, kl_ref, muk_ref[...])
        v = lerp(vc_ref, vl_ref, muv_ref[...])
        wd = lerp(wdc_ref, wdl_ref, mux_ref[0:1, :])
        ad = lerp(adc_ref, adl_ref, mux_ref[1:2, :])
        gd = lerp(gdc_ref, gdl_ref, mux_ref[2:3, :])
        tw = jnp.tanh(wd)
        kkv = k * kkw
        yield
        kk = kkv * lax.rsqrt(_mm(kkv * kkv, bdm) + 1e-6)
        ksum = jnp.zeros_like(k)
        for d in range(2):
            lw = -0.6065306597126334 * _sigmoid(vec_ref[3 + d:4 + d, :] + _mm(tw, wup_ref[d]))
            a = _sigmoid(vec_ref[5 + d:6 + d, :] + _mm(ad, aup_ref[d]))
            kdir = k * (1.0 + (a - 1.0) * kaw)
            ksum = ksum + kdir
            cum = _mm_sel_l(trit_ref[d, 0:n, 0:n], lw)
            einv = jnp.exp(-cum)
            dir_ref[d, 0, uo:uo + n, :] = cum
            dir_ref[d, 1, uo:uo + n, :] = kk * jnp.exp(cum - lw)
            dir_ref[d, 2, uo:uo + n, :] = r * jnp.exp(cum)
            dir_ref[d, 3, uo:uo + n, :] = kk * a * einv
            dir_ref[d, 4, uo:uo + n, :] = kdir * einv
            yield
        bonus = _mm_sel_r(r * ksum * rkw, bdm)
        vs_ref[uo:uo + n, :] = v
        gs_ref[uo:uo + n, :] = _mm(_sigmoid(gd), gup_ref[...])
        bvs_ref[uo:uo + n, :] = bonus * v
        yield

    tiles = _row_tiles(tc, tl)
    sched = _tile_schedule(tc, tl, grp)
    _round_robin([_chain([prep_tile(*t) for t, (need, _) in zip(tiles, sched) if need == 0])])
    fill = ([_chain([prep_tile(*t) for t, (need, _) in zip(tiles, sched) if need == 1])],
            [_chain([prep_tile(*t) for t, (need, _) in zip(tiles, sched) if need >= 2])])

    n_grp = n_all // grp
    pd = [(i, d) for i in range(grp) for d in range(2)]
    chunk_rows = lambda gi, i, d: _chunk_rows(gi, grp, i, d, n_ctx, n_all)

    def stage_a(gi, slot):
        def finish(pi, pdir, m, vbd):
            tri2 = jnp.concatenate([strict_ref[pdir], incl_ref[pdir]], axis=0)
            av = _mm(m[:, LANES:] * tri2, vbd)
            ab = m[:, :LANES] * tri2
            ls_ref[slot, pi, pdir, 0:CH, :] = ab[:CH]
            ls_ref[slot, pi, pdir, CH:2 * CH, :] = av[:CH]
            ls_ref[slot, pi, pdir, 2 * CH:3 * CH, :] = ab[CH:]
            ls_ref[slot, pi, pdir, 3 * CH:4 * CH, :] = av[CH:]
        pending = None
        for i, d in pd:
            rows = chunk_rows(gi, i, d)
            kkq, rq, binv, kinv = (dir_ref[d, j, rows, :] for j in range(1, 5))
            lhs = jnp.concatenate([kkq, rq], axis=0)
            rhs = jnp.concatenate([_bd(binv, m0, m1), _bd(kinv, m0, m1)], axis=0)
            cur = (i, d, _mm_nt(lhs, rhs), _bd(vs_ref[rows, :], m0, m1))
            if pending is not None:
                finish(*pending)
            pending = cur
            yield
        finish(*pending)
        yield

    def stage_b(gi, slot):
        mats = [ls_ref[slot, i, d, 0:CH, :] for i, d in pd]
        es = []
        yield from _inv_levels(mats, [d for _, d in pd], lvl_ref, m0, m1, es)
        sols = []
        for e, (i, d) in zip(es, pd):
            rows = chunk_rows(gi, i, d)
            rhs = jnp.concatenate([dir_ref[d, 1, rows, :], ls_ref[slot, i, d, CH:2 * CH, :]], axis=1)
            sols.append(rhs + _mm(e, _bd(rhs, m0w, m1w)))
        yield
        for sol, (i, d) in zip(sols, pd):
            rows = chunk_rows(gi, i, d)
            cum = dir_ref[d, 0, rows, :]
            etot = jnp.exp(cum[CH - 1:CH, :] if d == 0 else cum[0:1, :])
            bdec, kdec = dir_ref[d, 3, rows, :] * etot, dir_ref[d, 4, rows, :] * etot
            ar = _mm(ls_ref[slot, i, d, 2 * CH:3 * CH, :], _bd(sol, m0w, m1w))
            pmat = _mm_tn(sol[:, :LANES], bdec)
            nmat = _mm_tn(jnp.concatenate([-sol[:, LANES:], vs_ref[rows, :]], axis=0),
                          jnp.concatenate([bdec, kdec], axis=0))
            p1_ref[slot, i, d, 0:LANES, :] = pmat * bdm
            p1_ref[slot, i, d, LANES:2 * LANES, :] = nmat * bdm
            p1_ref[slot, i, d, 2 * LANES:2 * LANES + CH, :] = dir_ref[d, 2, rows, :] - ar[:, :LANES]
            p1_ref[slot, i, d, 2 * LANES + CH:2 * LANES + 2 * CH, :] = (
                ls_ref[slot, i, d, 3 * CH:4 * CH, :] - ar[:, LANES:])
            p1_ref[slot, i, d, 2 * LANES + 2 * CH:2 * LANES + 2 * CH + SUBLANES, :] = jnp.broadcast_to(
                etot, (SUBLANES, LANES))
        yield

    def stage_c(gi, slot, st):
        for i, d in pd:
            rows = chunk_rows(gi, i, d)
            pmat = p1_ref[slot, i, d, 0:LANES, :]
            nmat = p1_ref[slot, i, d, LANES:2 * LANES, :]
            rmat = p1_ref[slot, i, d, 2 * LANES:2 * LANES + CH, :]
            ymat = p1_ref[slot, i, d, 2 * LANES + CH:2 * LANES + 2 * CH, :]
            etot = p1_ref[slot, i, d, 2 * LANES + 2 * CH:2 * LANES + 2 * CH + 1, :]
            yref = yf_ref if d == 0 else yb_ref
            yref[rows, :] = _mm_nt(rmat, st[d]) + ymat
            st[d] = st[d] * etot - _mm(st[d], pmat) + nmat
            yield

    gnw, gnb = vec_ref[7:8, :], vec_ref[8:9, :]

    def out_tile(_, uo, n):
        y = yf_ref[uo:uo + n, :] + yb_ref[uo:uo + n, :]
        mean = _mm_sel_r(y, bdm) * (1.0 / HD)
        yield
        yc = y - mean
        var = _mm(yc * yc, bdm) * (1.0 / HD)
        yield
        out = (yc * lax.rsqrt(var + RWKV_GN_EPS) * gnw + gnb + bvs_ref[uo:uo + n, :]) * gs_ref[uo:uo + n, :]
        if uo < tc:
            oc_ref[uo:uo + n, :] = out
        else:
            ol_ref[uo - tc:uo - tc + n, :] = out
        yield

    z = jnp.zeros((LANES, LANES), F32)
    _run_pipeline(n_grp, stage_a, stage_b, stage_c, (z, z), b_before_a=False, fill=fill,
                  drain=_drain_streams(tiles, sched, n_grp, out_tile))


def _rwkv_mixer(rw_c, rw_l, mu, w0, w_up, a0, a_up, g_up, k_k, k_a, r_k, gn_w, gn_b, consts):
    bsz, tc, _ = rw_c.shape
    tl = rw_l.shape[1]
    tall = tc + tl
    n_all = tall // CH
    W = RWKV_WIDTH
    nb = W // LANES
    zr = jnp.zeros((RWKV_RANK, W), F32)
    pad_dir = lambda u: jnp.stack([jnp.concatenate([u[0], zr], axis=0), jnp.concatenate([zr, u[1]], axis=0)])
    wup, aup = pad_dir(w_up), pad_dir(a_up)
    vec = _rows([k_k, k_a, r_k.reshape(-1), w0[0], w0[1], a0[0], a0[1], gn_w, gn_b], 16)
    mu_rkv = mu[:3 * W].reshape(3, 1, W)
    mux = jnp.pad(mu[3 * W:].reshape(3, LANES), ((0, SUBLANES - 3), (0, 0)))
    cm = lambda *blk: pl.BlockSpec(blk, lambda b, p: (0,) * len(blk))

    def tok(t, col0, per_pair=True):
        if per_pair:
            return pl.BlockSpec((None, t, LANES), lambda b, p: (b, 0, col0 + p))
        return pl.BlockSpec((None, t, LANES), lambda b, p: (b, 0, col0))
    stream = lambda t: [tok(t, 0), tok(t, nb), tok(t, 2 * nb), tok(t, 3 * nb, False),
                        tok(t, 3 * nb + 1, False), tok(t, 3 * nb + 2, False)]
    in_specs = stream(tc) + stream(tl) + [
        pl.BlockSpec((None, 1, LANES), lambda b, p: (0, 0, p)),
        pl.BlockSpec((None, 1, LANES), lambda b, p: (1, 0, p)),
        pl.BlockSpec((None, 1, LANES), lambda b, p: (2, 0, p)),
        cm(SUBLANES, LANES),
        pl.BlockSpec((2, LANES, LANES), lambda b, p: (0, 0, p)),
        pl.BlockSpec((2, LANES, LANES), lambda b, p: (0, 0, p)),
        pl.BlockSpec((RWKV_G_RANK, LANES), lambda b, p: (0, p)),
        pl.BlockSpec((16, LANES), lambda b, p: (0, p)),
        cm(2, CH, LANES), cm(2, CH, LANES), cm(2, 6, CH, LANES), cm(2, 1, LANES), cm(LANES, LANES),
        cm(2, ROWT, ROWT)]
    scratch = [pltpu.VMEM((tall, LANES), F32)] * 3 + [
        pltpu.VMEM((2, 5, tall, LANES), F32),
        pltpu.VMEM((2, _group_size(n_all), 2, 4 * CH, LANES), F32),
        pltpu.VMEM((2, _group_size(n_all), 2, 2 * LANES + 2 * CH + SUBLANES, LANES), F32),
        pltpu.VMEM((tall, LANES), F32), pltpu.VMEM((tall, LANES), F32)]
    return pl.pallas_call(
        _rwkv_kernel,
        grid=(bsz, N_PAIRS),
        in_specs=in_specs,
        out_specs=[pl.BlockSpec((None, tc, LANES), lambda b, p: (b, 0, p)),
                   pl.BlockSpec((None, tl, LANES), lambda b, p: (b, 0, p))],
        out_shape=[jax.ShapeDtypeStruct((bsz, tc, W), F32),
                   jax.ShapeDtypeStruct((bsz, tl, W), F32)],
        scratch_shapes=scratch,
        compiler_params=pltpu.CompilerParams(dimension_semantics=("arbitrary", "arbitrary"),
                                             vmem_limit_bytes=VMEM_LIMIT),
        name="rwkv_mixer",
    )(*([rw_c] * 6), *([rw_l] * 6), mu_rkv, mu_rkv, mu_rkv, mux, wup, aup, g_up, vec,
      consts["incl"], consts["strict"], consts["lvl"], consts["hm"], consts["bdm"], consts["tri_t"])


def _lru_kernel(xc0_ref, xc1_ref, gc0_ref, gc1_ref, xl0_ref, xl1_ref, gl0_ref, gl1_ref,
                cw_ref, vec_ref, wcat_ref, bcat_ref, oc_ref, ol_ref,
                xp0_ref, xp1_ref, ab0_ref, ab1_ref, hf0_ref, hf1_ref, hb0_ref, hb1_ref):
    tc, tl = xc0_ref.shape[0], xl0_ref.shape[0]
    tall = tc + tl
    rows = tl // GRID_W
    W = LANES
    halves = ((xc0_ref, gc0_ref, xl0_ref, gl0_ref, xp0_ref, ab0_ref, hf0_ref, hb0_ref),
              (xc1_ref, gc1_ref, xl1_ref, gl1_ref, xp1_ref, ab1_ref, hf1_ref, hb1_ref))
    def prep_half(hh):
        xc_ref, _, xl_ref, _, xp_ref, ab_ref, _, _ = halves[hh]
        ln = slice(hh * W, (hh + 1) * W)
        z = jnp.zeros((GAP, W), F32)
        xp_ref[0:GAP, :] = z
        xp_ref[GAP:GAP + tc, :] = xc_ref[...]
        xp_ref[GAP + tc:2 * GAP + tc, :] = z
        base = 2 * GAP + tc
        for c in range(GRID_W):
            xp_ref[base + c * rows:base + (c + 1) * rows, :] = xl_ref[pl.ds(c, rows, stride=GRID_W), :]
        xp_ref[base + tl:base + tl + GAP, :] = z
        cb = vec_ref[0:1, ln]
        for (po, uo, n) in _row_tiles(tc, tl):
            acc = cb + cw_ref[0:1, ln] * xp_ref[po - 2:po - 2 + n, :]
            for tap in range(1, CONV_W):
                acc = acc + cw_ref[tap:tap + 1, ln] * xp_ref[po - 2 + tap:po - 2 + tap + n, :]
            gates = _mm(acc, wcat_ref[hh]) + bcat_ref[hh]
            for d in range(2):
                rg = _sigmoid(gates[:, (2 * d) * W:(2 * d + 1) * W])
                ig = _sigmoid(gates[:, (2 * d + 1) * W:(2 * d + 2) * W])
                log_a = -LRU_C * rg * _softplus(-vec_ref[1 + d:2 + d, ln])
                a = jnp.exp(log_a)
                m2 = -jnp.tanh(log_a) * (1.0 + a * a)
                mult = jnp.where(m2 > 0.0, m2 * lax.rsqrt(m2), 0.0)
                ab_ref[d, 0, uo:uo + n, :] = a
                ab_ref[d, 1, uo:uo + n, :] = mult * (ig * acc)
            yield
    _round_robin([prep_half(0), prep_half(1)])

    sub = lax.broadcasted_iota(jnp.int32, (SUBLANES, W), 0)
    n_tiles_c, n_tiles = tc // SUBLANES, tall // SUBLANES

    def tile_scan(a, b, d):
        for sh in (1, 2, 4):
            if d == 0:
                ok = sub >= sh
                a_s = jnp.where(ok, pltpu.roll(a, sh, 0), 1.0)
                b_s = jnp.where(ok, pltpu.roll(b, sh, 0), 0.0)
            else:
                ok = sub < SUBLANES - sh
                a_s = jnp.where(ok, pltpu.roll(a, SUBLANES - sh, 0), 1.0)
                b_s = jnp.where(ok, pltpu.roll(b, SUBLANES - sh, 0), 0.0)
            b = b + a * b_s
            a = a * a_s
        return a, b

    unroll = 4 if (n_tiles_c % 4 == 0 and n_tiles % 4 == 0) else 1

    def scan_body(s4, carry):
        carry = list(carry)
        tiles = []
        for j in range(unroll):
            s = s4 * unroll + j
            r0 = pl.multiple_of(s * SUBLANES, SUBLANES)
            tb = jnp.where(s < n_tiles_c, n_tiles_c - 1 - s, n_tiles + n_tiles_c - 1 - s)
            r1 = pl.multiple_of(tb * SUBLANES, SUBLANES)
            for hh, (_, _, _, _, _, ab_ref, hf_ref, hb_ref) in enumerate(halves):
                tiles.append((2 * hh, hf_ref, r0, SUBLANES - 1,
                              tile_scan(ab_ref[0, 0, pl.ds(r0, SUBLANES), :], ab_ref[0, 1, pl.ds(r0, SUBLANES), :], 0)))
                tiles.append((2 * hh + 1, hb_ref, r1, 0,
                              tile_scan(ab_ref[1, 0, pl.ds(r1, SUBLANES), :], ab_ref[1, 1, pl.ds(r1, SUBLANES), :], 1)))
        for ci, href, r, last, (a, b) in tiles:
            h = b + a * carry[ci]
            href[pl.ds(r, SUBLANES), :] = h
            carry[ci] = jnp.broadcast_to(h[last:last + 1, :], (SUBLANES, W))
        return tuple(carry)
    z8 = jnp.zeros((SUBLANES, W), F32)
    lax.fori_loop(0, n_tiles // unroll, scan_body, (z8, z8, z8, z8))

    for hh, (_, gc_ref, _, gl_ref, _, _, hf_ref, hb_ref) in enumerate(halves):
        ln = slice(hh * W, (hh + 1) * W)
        oc_ref[:, ln] = (hf_ref[0:tc, :] + hb_ref[0:tc, :]) * _gelu_tanh(gc_ref[...])
        for r in range(rows):
            h = hf_ref[pl.ds(tc + r, GRID_W, stride=rows), :] + hb_ref[pl.ds(tc + r, GRID_W, stride=rows), :]
            ol_ref[r * GRID_W:(r + 1) * GRID_W, ln] = h * _gelu_tanh(gl_ref[r * GRID_W:(r + 1) * GRID_W, :])


def _lru_mixer(x_c, g_c, x_l, g_l, conv_w, conv_b, w_a, b_a, w_x, b_x, lam):
    bsz, tc, W = x_c.shape
    tl = x_l.shape[1]
    tall = tc + tl
    nh = W // LANES
    bph = LANES // LRU_BW
    w6 = jnp.stack([w_a, w_x], axis=1).reshape(2, 2, nh, bph, LRU_BW, LRU_BW)
    wcat = jnp.einsum("dghnrc,nm->hnrdgmc", w6, jnp.eye(bph, dtype=F32)).reshape(nh, LANES, 4 * LANES)
    bcat = jnp.stack([b_a, b_x], axis=1).reshape(2, 2, nh, LANES).transpose(2, 0, 1, 3).reshape(nh, 1, 4 * LANES)
    vec = _rows([conv_b, lam[0], lam[1]], SUBLANES)
    assert nh == 2
    half = lambda t, h: pl.BlockSpec((None, t, LANES), lambda b: (b, 0, h))
    full = lambda *blk: pl.BlockSpec(blk, lambda b: (0,) * len(blk))
    return pl.pallas_call(
        _lru_kernel,
        grid=(bsz,),
        in_specs=[half(tc, 0), half(tc, 1), half(tc, 0), half(tc, 1),
                  half(tl, 0), half(tl, 1), half(tl, 0), half(tl, 1),
                  full(CONV_W, W), full(SUBLANES, W), full(nh, LANES, 4 * LANES), full(nh, 1, 4 * LANES)],
        out_specs=[pl.BlockSpec((None, tc, W), lambda b: (b, 0, 0)), pl.BlockSpec((None, tl, W), lambda b: (b, 0, 0))],
        out_shape=[jax.ShapeDtypeStruct((bsz, tc, W), F32), jax.ShapeDtypeStruct((bsz, tl, W), F32)],
        scratch_shapes=[pltpu.VMEM((tall + 3 * GAP, LANES), F32)] * 2 + [pltpu.VMEM((2, 2, tall, LANES), F32)] * 2
        + [pltpu.VMEM((tall, LANES), F32)] * 4,
        compiler_params=pltpu.CompilerParams(dimension_semantics=("arbitrary",),
                                             vmem_limit_bytes=VMEM_LIMIT),
        name="lru_mixer",
    )(x_c, x_c, g_c, g_c, x_l, x_l, g_l, g_l, conv_w, vec, wcat.astype(BF16), bcat)


def _finish_kernel(x_ref, gdn_ref, lru_ref, rwk_ref, m2_ref, m3_ref, m4_ref, m5_ref, nrm_ref,
                   wo_ref, up_ref, dn_ref, o_ref):
    x = x_ref[...]
    o = (jnp.dot(gdn_ref[...].astype(BF16), wo_ref[0:GDN_WIDTH, :], preferred_element_type=F32)
         + jnp.dot(lru_ref[...].astype(BF16), wo_ref[GDN_WIDTH:GDN_WIDTH + LRU_WIDTH, :],
                   preferred_element_type=F32)
         + jnp.dot(rwk_ref[...].astype(BF16), wo_ref[GDN_WIDTH + LRU_WIDTH:, :], preferred_element_type=F32))
    x = x + m2_ref[...] * _rms(o, nrm_ref[0:1, :])
    h = (_rms(x, nrm_ref[1:2, :]) * (1.0 + m4_ref[...]) + m3_ref[...]).astype(BF16)
    f = jnp.zeros_like(x)
    fc = 1024
    for j in range(D_FF // fc):
        a = jnp.maximum(jnp.dot(h, up_ref[:, j * fc:(j + 1) * fc], preferred_element_type=F32), 0.0)
        f = f + jnp.dot((a * a).astype(BF16), dn_ref[j * fc:(j + 1) * fc, :], preferred_element_type=F32)
    o_ref[...] = x + m5_ref[...] * _rms(f, nrm_ref[2:3, :])


def _finish(x2, gdn, lru, rwk, mod_rows, rows_per_mod, norms, wo_bf, up_bf, dn_bf, layer):
    n = x2.shape[0]
    tm = DENSE_TM
    tiles_per_mod = rows_per_mod // tm
    modspec = lambda k: pl.BlockSpec((None, 1, D_MODEL), lambda i: (6 * (i // tiles_per_mod) + k, 0, 0))
    cm = lambda *blk: pl.BlockSpec((None,) + blk, lambda i: (layer,) + (0,) * len(blk),
                                   pipeline_mode=pl.Buffered(1))
    return pl.pallas_call(
        _finish_kernel,
        grid=(n // tm,),
        in_specs=[pl.BlockSpec((tm, D_MODEL), lambda i: (i, 0)),
                  pl.BlockSpec((tm, GDN_WIDTH), lambda i: (i, 0)),
                  pl.BlockSpec((tm, LRU_WIDTH), lambda i: (i, 0)),
                  pl.BlockSpec((tm, RWKV_WIDTH), lambda i: (i, 0)),
                  modspec(2), modspec(3), modspec(4), modspec(5),
                  cm(SUBLANES, D_MODEL), cm(D_MODEL, D_MODEL), cm(D_MODEL, D_FF), cm(D_FF, D_MODEL)],
        out_specs=pl.BlockSpec((tm, D_MODEL), lambda i: (i, 0)),
        out_shape=jax.ShapeDtypeStruct((n, D_MODEL), F32),
        compiler_params=pltpu.CompilerParams(dimension_semantics=("arbitrary",),
                                             vmem_limit_bytes=VMEM_LIMIT),
        name="finish",
    )(x2, gdn, lru, rwk, mod_rows, mod_rows, mod_rows, mod_rows, norms, wo_bf, up_bf, dn_bf)


def kernel(x, c, ctx, c_ctx, ada_w, ada_b, norm_mix_pre, norm_mix_post, norm_ffn_pre, norm_ffn_post, w_in, gdn_conv, gdn_a_log, gdn_dt_bias, gdn_norm, lru_conv, lru_conv_b, lru_wa, lru_ba, lru_wx, lru_bx, lru_lambda, rwkv_mu, rwkv_w0, rwkv_w_up, rwkv_a0, rwkv_a_up, rwkv_g_up, rwkv_k_k, rwkv_k_a, rwkv_r_k, rwkv_gn_w, rwkv_gn_b, w_out, ffn_up, ffn_down):
    bsz, tl, _ = x.shape
    tc = ctx.shape[1]
    depth = w_in.shape[0]
    consts = _device_consts()

    cvec = jnp.pad(jnp.concatenate([c, c_ctx[None, :]], axis=0), ((0, 16 - bsz - 1), (0, 0)))
    mods = _ada_mod(cvec, ada_w, ada_b).reshape(depth, 16, 6, D_MODEL)

    w_in_bf = w_in.astype(BF16)
    wo_bf, up_bf, dn_bf = w_out.astype(BF16), ffn_up.astype(BF16), ffn_down.astype(BF16)
    g_pre = norm_mix_pre.reshape(depth, 1, D_MODEL)
    norms = jnp.pad(jnp.stack([norm_mix_post, norm_ffn_pre, norm_ffn_post], axis=1),
                    ((0, 0), (0, SUBLANES - 3), (0, 0)))

    xl = x.reshape(bsz * tl, D_MODEL)
    xc = ctx.reshape(bsz * tc, D_MODEL)
    for i in range(depth):
        mod_l = mods[i, 0:bsz].reshape(bsz * 6, 1, D_MODEL)
        mod_c = mods[i, bsz:bsz + 1].reshape(6, 1, D_MODEL)
        pl_ = _inproj(xl, mod_l, tl, g_pre, w_in_bf, i)
        pc_ = _inproj(xc, mod_c, bsz * tc, g_pre, w_in_bf, i)
        r3 = lambda a, t: a.reshape(bsz, t, a.shape[-1])
        qkv_l, z_l, ba_l, lx_l, lg_l, rw_l = (r3(a, tl) for a in pl_)
        qkv_c, z_c, ba_c, lx_c, lg_c, rw_c = (r3(a, tc) for a in pc_)

        gdn_c, gdn_l = _gdn_mixer(qkv_c, z_c, ba_c, qkv_l, z_l, ba_l, gdn_conv[i], gdn_a_log[i],
                                  gdn_dt_bias[i], gdn_norm[i], consts)
        lru_c, lru_l = _lru_mixer(lx_c, lg_c, lx_l, lg_l, lru_conv[i], lru_conv_b[i], lru_wa[i], lru_ba[i],
                                  lru_wx[i], lru_bx[i], lru_lambda[i])
        rwk_c, rwk_l = _rwkv_mixer(rw_c, rw_l, rwkv_mu[i], rwkv_w0[i], rwkv_w_up[i], rwkv_a0[i],
                                   rwkv_a_up[i], rwkv_g_up[i], rwkv_k_k[i], rwkv_k_a[i], rwkv_r_k[i],
                                   rwkv_gn_w[i], rwkv_gn_b[i], consts)

        f2 = lambda a: a.reshape(-1, a.shape[-1])
        xl = _finish(xl, f2(gdn_l), f2(lru_l), f2(rwk_l), mod_l, tl, norms, wo_bf, up_bf, dn_bf, i)
        if i < depth - 1:
            xc = _finish(xc, f2(gdn_c), f2(lru_c), f2(rwk_c), mod_c, bsz * tc, norms, wo_bf, up_bf, dn_bf, i)
    return xl.reshape(bsz, tl, D_MODEL)
```
